```python
import math
import jax, jax.numpy as jnp
from jax import lax
import numpy as np

D_MODEL = 4096
BATCH = 2
SEQ = 4096
DEPTH = 2

HEAD_DIM = 128
Q_BLOCK = 128
NEG = -1e30
FORCE_SCORE = 1e4
LN_EPS = 1e-5
N_BUCKETS = 32
MAX_DISTANCE = 128
H_A = 8
Q_LORA = 768
KV_LORA = 256
H_IDX = 32
D_IDX = 128
TOPK_MAX = 256
H_B = 8
G_B = 2
CMP_BLOCK = 32
CMP_STRIDE = 16
CMP_HIDDEN = 128
SEL_BLOCK = 64
N_SEL = 16
WIN_B = 512
H_C = 16
KV_C = 2
HEAD_DIM_C = 64
WIN_C = 128
N_BRANCH = 3
BRANCH_W = 1024
H_TOTAL = H_A + H_B + H_C
D_FF = 11008
CONV_W = 3
D_PLE = 256
ALPHA = (2 * DEPTH) ** 0.25
BETA = (8 * DEPTH) ** -0.25
IN_SIZES = (Q_LORA, KV_LORA, D_IDX, H_IDX,
            H_B * HEAD_DIM, 6 * G_B * HEAD_DIM, 3 * H_B,
            H_C * HEAD_DIM_C, KV_C * HEAD_DIM_C, KV_C * HEAD_DIM_C,
            N_BRANCH * D_MODEL)

kernel_name = 'hybrid_dsa_nsa_swa_deepnorm_block'


def _split_points():
    pts, acc = [], 0
    for sz in IN_SIZES[:-1]:
        acc += sz
        pts.append(acc)
    return pts


def layer_norm(x, g, b):
    xf = x.astype(jnp.float32)
    mu = xf.mean(-1, keepdims=True)
    var = jnp.square(xf - mu).mean(-1, keepdims=True)
    return ((xf - mu) * lax.rsqrt(var + LN_EPS)).astype(x.dtype) * g + b


def rms_norm(x, g):
    xf = x.astype(jnp.float32)
    return (xf * lax.rsqrt(jnp.square(xf).mean(-1, keepdims=True) + LN_EPS)).astype(x.dtype) * g


def t5_bucket(dist):
    dist = jnp.maximum(dist, 0)
    max_exact = N_BUCKETS // 2
    d = jnp.maximum(dist, 1).astype(jnp.float32)
    large = max_exact + (jnp.log(d / max_exact) / math.log(MAX_DISTANCE / max_exact)
                         * (N_BUCKETS - max_exact)).astype(jnp.int32)
    large = jnp.minimum(large, N_BUCKETS - 1)
    return jnp.where(dist < max_exact, dist, large)


def masked_softmax(s, mask):
    s = jnp.where(mask, s.astype(jnp.float32), NEG)
    return jax.nn.softmax(s, axis=-1) * jnp.any(mask, axis=-1, keepdims=True)


def _merge_blocks(out):
    nb, b, qb = out.shape[:3]
    return jnp.moveaxis(out, 0, 1).reshape((b, nb * qb) + out.shape[3:])


def banded_attention(q, k, v, window, bias_tab, sinks):
    b, s, g, hpg, d = q.shape
    kp = jnp.pad(k, ((0, 0), (window, 0), (0, 0), (0, 0)))
    vp = jnp.pad(v, ((0, 0), (window, 0), (0, 0), (0, 0)))
    span = Q_BLOCK + window

    def block(blk):
        qs = blk * Q_BLOCK
        tq = qs + jnp.arange(Q_BLOCK)
        kpos = qs - window + jnp.arange(span)
        qb = lax.dynamic_slice_in_dim(q, qs, Q_BLOCK, axis=1)
        kb = lax.dynamic_slice_in_dim(kp, qs, span, axis=1)
        vb = lax.dynamic_slice_in_dim(vp, qs, span, axis=1)
        dist = tq[:, None] - kpos[None, :]
        mask = (dist >= 0) & (dist < window) & (kpos[None, :] >= 0)
        bias = bias_tab[t5_bucket(dist)].reshape(Q_BLOCK, span, g, hpg).transpose(2, 3, 0, 1)
        logit = jnp.einsum('bqgjd,bkgd->bgjqk', qb, kb) * d ** -0.5 + bias
        if sinks is None:
            prob = masked_softmax(logit, mask)
        else:
            full = jnp.where(mask, logit.astype(jnp.float32), NEG)
            sink = jnp.broadcast_to(sinks.astype(jnp.float32).reshape(1, g, hpg, 1, 1), full.shape[:-1] + (1,))
            prob = jax.nn.softmax(jnp.concatenate([full, sink], axis=-1), axis=-1)[..., :-1]
        return jnp.einsum('bgjqk,bkgd->bqgjd', prob.astype(vb.dtype), vb)

    return _merge_blocks(lax.map(block, jnp.arange(s // Q_BLOCK)))


def dsa_mixer(c_q, c_kv, k_idx, w_idx, w_qidx, w_uq, w_uk, w_uv, bias_tab):
    b, s, _ = c_q.shape
    topk = min(TOPK_MAX, s // 4)
    q_idx = (c_q @ w_qidx).reshape(b, s, H_IDX, D_IDX)
    q = (c_q @ w_uq).reshape(b, s, H_A, HEAD_DIM)
    q_lat = jnp.einsum('bshd,chd->bshc', q, w_uk)
    w_idx = w_idx * (H_IDX ** -0.5)
    kpos = jnp.arange(s)
    gather = jax.vmap(lambda c, i: c[i])

    def block(blk):
        qs = blk * Q_BLOCK
        tq = qs + jnp.arange(Q_BLOCK)
        qi = lax.dynamic_slice_in_dim(q_idx, qs, Q_BLOCK, axis=1)
        wi = lax.dynamic_slice_in_dim(w_idx, qs, Q_BLOCK, axis=1)
        ql = lax.dynamic_slice_in_dim(q_lat, qs, Q_BLOCK, axis=1)
        rel = jax.nn.relu(jnp.einsum('bqhd,bkd->bqhk', qi, k_idx) * D_IDX ** -0.5)
        score = jnp.einsum('bqh,bqhk->bqk', wi, rel).astype(jnp.float32)
        score = jnp.where(kpos[None, None, :] <= tq[None, :, None], score, -jnp.inf)
        _, idx = lax.top_k(score, topk)
        valid = idx <= tq[None, :, None]
        kv = gather(c_kv, idx)
        logit = jnp.einsum('bqhc,bqkc->bqhk', ql, kv) * HEAD_DIM ** -0.5
        bias = jnp.moveaxis(bias_tab[t5_bucket(tq[None, :, None] - idx)], -1, 2)
        prob = masked_softmax(logit + bias, valid[:, :, None, :]).astype(kv.dtype)
        return jnp.einsum('bqhk,bqkc->bqhc', prob, kv)

    o_lat = _merge_blocks(lax.map(block, jnp.arange(s // Q_BLOCK)))
    o = jnp.einsum('bshc,chd->bshd', o_lat, w_uv)
    return o.reshape(b, s, H_A * HEAD_DIM)


def nsa_mixer(q, kv_all, gate_logits, cmp_pe, cmp_w1, cmp_w2, bias_tab):
    b, s, _ = q.shape
    hpg = H_B // G_B
    q = q.reshape(b, s, G_B, hpg, HEAD_DIM)
    k_cmp, v_cmp, k_slc, v_slc, k_win, v_win = [t.reshape(b, s, G_B, HEAD_DIM) for t in jnp.split(kv_all, 6, axis=-1)]
    scale = HEAD_DIM ** -0.5
    pos = jnp.arange(s)
    n_cmp = (s - CMP_BLOCK) // CMP_STRIDE + 1
    cmp_start = jnp.arange(n_cmp) * CMP_STRIDE
    win_idx = cmp_start[:, None] + jnp.arange(CMP_BLOCK)[None, :]

    def compress(t, pe, w1, w2):
        blocks = t[:, win_idx] + pe[None, None, :, None, :]
        h = jax.nn.gelu(jnp.einsum('bnlgd,ldf->bngf', blocks, w1))
        return jnp.einsum('bngf,fd->bngd', h, w2)

    kc = compress(k_cmp, cmp_pe[0], cmp_w1[0], cmp_w2[0])
    vc = compress(v_cmp, cmp_pe[1], cmp_w1[1], cmp_w2[1])
    cmp_mask = (cmp_start + CMP_BLOCK - 1)[None, :] <= pos[:, None]
    p_cmp = masked_softmax(jnp.einsum('bsgjd,bngd->bgjsn', q, kc) * scale, cmp_mask)
    o_cmp = jnp.einsum('bgjsn,bngd->bsgjd', p_cmp.astype(q.dtype), vc)
    n_blk = s // SEL_BLOCK
    n_pick = min(N_SEL, n_blk)
    blk_start = jnp.arange(n_blk) * SEL_BLOCK
    overlap = ((cmp_start[:, None] < blk_start[None, :] + SEL_BLOCK)
               & (cmp_start[:, None] + CMP_BLOCK > blk_start[None, :])).astype(jnp.float32)
    imp = jnp.einsum('bgjsn,nm->bgsm', p_cmp, overlap)
    blk_id = jnp.arange(n_blk)[None, :]
    cur = (pos // SEL_BLOCK)[:, None]
    forced = (blk_id == 0) | (blk_id == cur) | (blk_id == cur - 1)
    imp = jnp.where(blk_id > cur, NEG, jnp.where(forced, FORCE_SCORE, imp))
    _, sel = lax.top_k(imp, n_pick)
    kb = k_slc.reshape(b, n_blk, SEL_BLOCK, G_B, HEAD_DIM).transpose(0, 3, 1, 2, 4)
    vb = v_slc.reshape(b, n_blk, SEL_BLOCK, G_B, HEAD_DIM).transpose(0, 3, 1, 2, 4)
    gather = jax.vmap(jax.vmap(lambda arr, i: arr[i]))
    tab_g = bias_tab.reshape(N_BUCKETS, G_B, hpg).transpose(1, 0, 2)
    g_ix = jnp.arange(G_B)[None, :, None, None, None]

    def sel_block(blk):
        qs = blk * Q_BLOCK
        tq = qs + jnp.arange(Q_BLOCK)
        qb = lax.dynamic_slice_in_dim(q, qs, Q_BLOCK, axis=1)
        ib = lax.dynamic_slice_in_dim(sel, qs, Q_BLOCK, axis=2)
        kg = gather(kb, ib)
        vg = gather(vb, ib)
        kpos = ib[..., None] * SEL_BLOCK + jnp.arange(SEL_BLOCK)
        dist = tq[None, None, :, None, None] - kpos
        logit = jnp.einsum('bqgjd,bgqnld->bgjqnl', qb, kg) * scale
        bias = jnp.moveaxis(tab_g[g_ix, t5_bucket(dist)], -1, 2)
        nk = n_pick * SEL_BLOCK
        prob = masked_softmax((logit + bias).reshape(b, G_B, hpg, Q_BLOCK, nk),
                              (dist >= 0).reshape(b, G_B, 1, Q_BLOCK, nk))
        return jnp.einsum('bgjqk,bgqkd->bqgjd', prob.astype(vg.dtype), vg.reshape(b, G_B, Q_BLOCK, nk, HEAD_DIM))

    o_slc = _merge_blocks(lax.map(sel_block, jnp.arange(s // Q_BLOCK)))
    o_win = banded_attention(q, k_win, v_win, WIN_B, bias_tab, None)
    g = jax.nn.sigmoid(gate_logits).reshape(b, s, G_B, hpg, 3)
    o = g[..., 0:1] * o_cmp + g[..., 1:2] * o_slc + g[..., 2:3] * o_win
    return o.reshape(b, s, H_B * HEAD_DIM)


def hybrid_layer(x, p_i, w_in, g_cq, g_ckv, g_kidx, b_kidx, w_qidx, w_uq, w_uk, w_uv,
                 cmp_pe, cmp_w1, cmp_w2, sinks, w_br, w_o, ln1_g, ln1_b,
                 w_up, conv_w, conv_b, w_down, w_pg, b_pg, w_pp, ln2_g, ln2_b, rel_bias):
    b, s, _ = x.shape
    z = x @ w_in
    (c_q, c_kv, k_idx, w_idx, q_b, kv_b, g_b, q_c, k_c, v_c, g_merge) = jnp.split(z, _split_points(), axis=-1)
    c_q = rms_norm(c_q, g_cq)
    c_kv = rms_norm(c_kv, g_ckv)
    k_idx = layer_norm(k_idx, g_kidx, b_kidx)
    o_a = dsa_mixer(c_q, c_kv, k_idx, w_idx, w_qidx, w_uq, w_uk, w_uv, rel_bias[:, :H_A])
    o_b = nsa_mixer(q_b, kv_b, g_b, cmp_pe, cmp_w1, cmp_w2, rel_bias[:, H_A:H_A + H_B])
    o_c = banded_attention(q_c.reshape(b, s, KV_C, H_C // KV_C, HEAD_DIM_C),
                           k_c.reshape(b, s, KV_C, HEAD_DIM_C), v_c.reshape(b, s, KV_C, HEAD_DIM_C),
                           WIN_C, rel_bias[:, H_A + H_B:], sinks).reshape(b, s, BRANCH_W)
    gates = jax.nn.sigmoid(g_merge).reshape(b, s, N_BRANCH, D_MODEL)
    merged = (gates[:, :, 0] * (o_a @ w_br[0]) + gates[:, :, 1] * (o_b @ w_br[1])
              + gates[:, :, 2] * (o_c @ w_br[2]))
    x = layer_norm(ALPHA * x + merged @ w_o, ln1_g, ln1_b)
    h = x @ w_up
    hp = jnp.pad(h, ((0, 0), (CONV_W - 1, 0), (0, 0)))
    h = sum((hp[:, k:k + s] * conv_w[k] for k in range(CONV_W)), conv_b)
    hg, hv = jnp.split(h, 2, axis=-1)
    ffn = (jax.nn.gelu(hg) * hv) @ w_down
    ple = jax.nn.sigmoid(x @ w_pg + b_pg) * (p_i @ w_pp)
    return layer_norm(ALPHA * x + ffn + ple, ln2_g, ln2_b)


def setup_inputs(seed: int = 0) -> dict:
    key = jax.random.key(seed)
    ks = iter(jax.random.split(key, 32))

    def nrm(shape, scale):
        return scale * jax.random.normal(next(ks), shape, jnp.float32)

    n_in = sum(IN_SIZES)
    return {
        'x': nrm((BATCH, SEQ, D_MODEL), 1.0),
        'p': nrm((DEPTH, BATCH, SEQ, D_PLE), 1.0),
        'w_in': nrm((DEPTH, D_MODEL, n_in), D_MODEL ** -0.5),
        'g_cq': 1.0 + nrm((DEPTH, Q_LORA), 0.02),
        'g_ckv': 1.0 + nrm((DEPTH, KV_LORA), 0.02),
        'g_kidx': 1.0 + nrm((DEPTH, D_IDX), 0.02),
        'b_kidx': nrm((DEPTH, D_IDX), 0.02),
        'w_qidx': nrm((DEPTH, Q_LORA, H_IDX * D_IDX), Q_LORA ** -0.5),
        'w_uq': nrm((DEPTH, Q_LORA, H_A * HEAD_DIM), Q_LORA ** -0.5),
        'w_uk': nrm((DEPTH, KV_LORA, H_A, HEAD_DIM), KV_LORA ** -0.5),
        'w_uv': nrm((DEPTH, KV_LORA, H_A, HEAD_DIM), KV_LORA ** -0.5),
        'cmp_pe': nrm((DEPTH, 2, CMP_BLOCK, HEAD_DIM), 0.02),
        'cmp_w1': nrm((DEPTH, 2, CMP_BLOCK, HEAD_DIM, CMP_HIDDEN), (CMP_BLOCK * HEAD_DIM) ** -0.5),
        'cmp_w2': nrm((DEPTH, 2, CMP_HIDDEN, HEAD_DIM), CMP_HIDDEN ** -0.5),
        'sinks': nrm((DEPTH, H_C), 0.5),
        'w_br': nrm((DEPTH, N_BRANCH, BRANCH_W, D_MODEL), BETA * BRANCH_W ** -0.5),
        'w_o': nrm((DEPTH, D_MODEL, D_MODEL), BETA * D_MODEL ** -0.5),
        'ln1_g': 1.0 + nrm((DEPTH, D_MODEL), 0.02),
        'ln1_b': nrm((DEPTH, D_MODEL), 0.02),
        'w_up': nrm((DEPTH, D_MODEL, 2 * D_FF), D_MODEL ** -0.5),
        'conv_w': nrm((DEPTH, CONV_W, 2 * D_FF), CONV_W ** -0.5),
        'conv_b': nrm((DEPTH, 2 * D_FF), 0.02),
        'w_down': nrm((DEPTH, D_FF, D_MODEL), BETA * D_FF ** -0.5),
        'w_pg': nrm((DEPTH, D_MODEL, D_MODEL), D_MODEL ** -0.5),
        'b_pg': nrm((DEPTH, D_MODEL), 0.02),
        'w_pp': nrm((DEPTH, D_PLE, D_MODEL), BETA * D_PLE ** -0.5),
        'ln2_g': 1.0 + nrm((DEPTH, D_MODEL), 0.02),
        'ln2_b': nrm((DEPTH, D_MODEL), 0.02),
        'rel_bias': nrm((N_BUCKETS, H_TOTAL), 0.5),
    }


def reference(x, p, w_in, g_cq, g_ckv, g_kidx, b_kidx, w_qidx, w_uq, w_uk, w_uv,
              cmp_pe, cmp_w1, cmp_w2, sinks, w_br, w_o, ln1_g, ln1_b,
              w_up, conv_w, conv_b, w_down, w_pg, b_pg, w_pp, ln2_g, ln2_b, rel_bias):
    for i in range(DEPTH):
        x = hybrid_layer(x, p[i], w_in[i], g_cq[i], g_ckv[i], g_kidx[i], b_kidx[i], w_qidx[i], w_uq[i],
                         w_uk[i], w_uv[i], cmp_pe[i], cmp_w1[i], cmp_w2[i], sinks[i], w_br[i], w_o[i],
                         ln1_g[i], ln1_b[i], w_up[i], conv_w[i], conv_b[i], w_down[i], w_pg[i], b_pg[i],
                         w_pp[i], ln2_g[i], ln2_b[i], rel_bias)
    return x
```

```python
import functools
import math

import jax
import jax.numpy as jnp
import numpy as np
from jax import lax
from jax.experimental import pallas as pl
from jax.experimental.pallas import tpu as pltpu

HEAD_DIM = 128
NEG = -1e30
FORCE_SCORE = 1e4
LN_EPS = 1e-5
N_BUCKETS = 32
MAX_DISTANCE = 128
H_A = 8
Q_LORA = 768
KV_LORA = 256
H_IDX = 32
D_IDX = 128
TOPK_MAX = 256
H_B = 8
G_B = 2
CMP_BLOCK = 32
CMP_STRIDE = 16
CMP_HIDDEN = 128
SEL_BLOCK = 64
N_SEL = 16
WIN_B = 512
H_C = 16
KV_C = 2
HEAD_DIM_C = 64
WIN_C = 128
N_BRANCH = 3
BRANCH_W = 1024
CONV_W = 3

LANES = 128
TQ = 128
TK = 512
VMEM_LIMIT = 56 * 1024 * 1024
INT_MIN = -2 ** 31

OFF_QB = 0
OFF_QC = 1024
OFF_CQ = 2048
OFF_CKV = 2816
OFF_KIDX = 3072
OFF_KVB = 3200
OFF_KC = 4736
OFF_VC = 4864
OFF_WIDX = 4992
N_SMALL = 5120

BF = jnp.bfloat16
F32 = jnp.float32


def _cparams(sem, vmem=VMEM_LIMIT):
    return pltpu.CompilerParams(dimension_semantics=sem, vmem_limit_bytes=vmem)


def _dot(a, b):
    return jnp.dot(a, b, preferred_element_type=F32)


def _dot_t(a, b):
    return lax.dot_general(a, b, (((1,), (1,)), ((), ())), preferred_element_type=F32)


def _mm_kernel(a_ref, b_ref, o_ref, acc_ref, *, nk):
    part = _dot(a_ref[...], b_ref[...].astype(BF))
    if nk == 1:
        o_ref[...] = part.astype(o_ref.dtype)
    else:
        k = pl.program_id(2)

        @pl.when(k == 0)
        def _():
            acc_ref[...] = part

        @pl.when(k > 0)
        def _():
            acc_ref[...] += part

        @pl.when(k == nk - 1)
        def _():
            o_ref[...] = acc_ref[...].astype(o_ref.dtype)


def _mm(a, b, *, tm, tn, tk=None, out_dtype=F32, name="mm"):
    m, kdim = a.shape
    n = b.shape[1]
    tk = kdim if tk is None else tk
    nk = kdim // tk
    assert m % tm == 0 and kdim % tk == 0
    acc_shape = (tm, tn) if nk > 1 else (8, LANES)
    return pl.pallas_call(
        functools.partial(_mm_kernel, nk=nk),
        grid=(m // tm, pl.cdiv(n, tn), nk),
        in_specs=[pl.BlockSpec((tm, tk), lambda i, j, k: (i, k)),
                  pl.BlockSpec((tk, tn), lambda i, j, k: (k, j))],
        out_specs=pl.BlockSpec((tm, tn), lambda i, j, k: (i, j)),
        out_shape=jax.ShapeDtypeStruct((m, n), out_dtype),
        scratch_shapes=[pltpu.VMEM(acc_shape, F32)],
        compiler_params=_cparams(("parallel", "parallel", "arbitrary")),
        name=name,
    )(a, b)


def _prep_kernel(c_ref, k_ref, gq_ref, gkv_ref, gk_ref, bk_ref, cq_o, ckv_o, kidx_o):
    c = c_ref[...]
    cq = c[:, :Q_LORA]
    ckv = c[:, Q_LORA:]
    cq = cq * lax.rsqrt(jnp.mean(cq * cq, axis=-1, keepdims=True) + LN_EPS) * gq_ref[...]
    ckv = ckv * lax.rsqrt(jnp.mean(ckv * ckv, axis=-1, keepdims=True) + LN_EPS) * gkv_ref[...]
    k = k_ref[...]
    mu = jnp.mean(k, axis=-1, keepdims=True)
    kc = k - mu
    var = jnp.mean(kc * kc, axis=-1, keepdims=True)
    kn = kc * lax.rsqrt(var + LN_EPS) * gk_ref[...] + bk_ref[...]
    cq_o[...] = cq.astype(BF)
    ckv_o[...] = ckv.astype(BF)
    kidx_o[...] = kn.astype(BF)


def _prep(z, g_cq, g_ckv, g_kidx, b_kidx, *, tm=512):
    m = z.shape[0]
    wc = Q_LORA + KV_LORA
    row = lambda a: a.reshape(1, -1)
    full = lambda n: pl.BlockSpec((1, n), lambda i: (0, 0))
    return pl.pallas_call(
        _prep_kernel,
        grid=(m // tm,),
        in_specs=[pl.BlockSpec((tm, wc), lambda i: (i, OFF_CQ // wc)),
                  pl.BlockSpec((tm, D_IDX), lambda i: (i, OFF_KIDX // D_IDX)),
                  full(Q_LORA), full(KV_LORA), full(D_IDX), full(D_IDX)],
        out_specs=[pl.BlockSpec((tm, Q_LORA), lambda i: (i, 0)),
                   pl.BlockSpec((tm, KV_LORA), lambda i: (i, 0)),
                   pl.BlockSpec((tm, D_IDX), lambda i: (i, 0))],
        out_shape=[jax.ShapeDtypeStruct((m, Q_LORA), BF),
                   jax.ShapeDtypeStruct((m, KV_LORA), BF),
                   jax.ShapeDtypeStruct((m, D_IDX), BF)],
        compiler_params=_cparams(("parallel",)),
        name="prep_norms",
    )(z, z, row(g_cq), row(g_ckv), row(g_kidx), row(b_kidx))


def _to_key(x):
    b = lax.bitcast_convert_type(x, jnp.int32)
    return b ^ ((b >> 31) & jnp.int32(0x7FFFFFFF))


def _kth_largest_key(count_ge, k, rows):
    kf = jnp.float32(k)
    zero = jnp.zeros((rows, 1), jnp.int32)
    t0 = jnp.where(count_ge(zero) >= kf, zero, jnp.full((rows, 1), INT_MIN, jnp.int32))

    def body(it, t):
        cand = t + jnp.left_shift(jnp.int32(1), jnp.int32(30) - it)
        return jnp.where(count_ge(cand) >= kf, cand, t)

    t = lax.fori_loop(0, 31, body, t0)
    return jnp.maximum(t, jnp.int32(INT_MIN + 1))


def _fold_lanes(x):
    n = x.shape[1] // LANES
    out = x[:, :LANES]
    for c in range(1, n):
        out = out + x[:, c * LANES:(c + 1) * LANES]
    return out


def _online_update(carry, s, mask, v):
    m, l, acc = carry
    m_new = jnp.maximum(m, jnp.max(jnp.where(mask, s, NEG), axis=-1, keepdims=True))
    alpha = jnp.exp(m - m_new)
    p = jnp.where(mask, jnp.exp(s - m_new), 0.0)
    l = alpha * l + jnp.sum(p, axis=-1, keepdims=True)
    acc = alpha * acc + _dot(p.astype(BF), v)
    return m_new, l, acc


def _finish(l, acc):
    return acc * jnp.where(l > 0.0, 1.0 / l, 0.0)


def _dsa_kernel(far_ref, qi_ref, q_ref, w_ref, kidx_ref, ckv_ref, wuk_ref, wuv_ref, near_ref,
                o_ref, keyf_ref, keyn_ref, wb_ref, ql_ref, *, topk):
    i = pl.program_id(1)
    tq = TQ
    far_end = jnp.maximum(i - 1, 0) * tq
    nfar = (far_end + TK - 1) // TK

    wsc = w_ref[...] * (D_IDX ** -0.5 * H_IDX ** -0.5)
    for h in range(H_IDX):
        wb_ref[h] = jnp.broadcast_to(wsc[:, h:h + 1], (tq, LANES))

    def scores(krows):
        n = krows.shape[0]
        acc = jnp.zeros((tq, n), F32)
        for h in range(H_IDX):
            s = _dot_t(qi_ref[:, h * D_IDX:(h + 1) * D_IDX], krows)
            wbh = wb_ref[h]
            acc = acc + jnp.maximum(s, 0.0) * jnp.concatenate([wbh] * (n // LANES), axis=1)
        return acc

    qrow = lax.broadcasted_iota(jnp.int32, (tq, 2 * tq), 0)
    jcol = lax.broadcasted_iota(jnp.int32, (tq, 2 * tq), 1)
    valid_n = (tq + qrow - jcol >= 0) & ((i - 1) * tq + jcol >= 0)
    row0 = pl.multiple_of(i * tq, tq)
    sn = scores(kidx_ref[pl.ds(row0, 2 * tq), :])
    keyn_ref[...] = jnp.where(valid_n, _to_key(sn), jnp.int32(INT_MIN))

    def far_scores(kt, c):
        r0 = pl.multiple_of(tq + kt * TK, tq)
        s = scores(kidx_ref[pl.ds(r0, TK), :])
        pos = kt * TK + lax.broadcasted_iota(jnp.int32, (tq, TK), 1)
        keyf_ref[kt] = jnp.where(pos < far_end, _to_key(s), jnp.int32(INT_MIN))
        return c

    lax.fori_loop(0, nfar, far_scores, 0)

    def count_ge(cand):
        part = _fold_lanes(jnp.where(keyn_ref[...] >= cand, 1.0, 0.0))

        def body(kt, a):
            return a + _fold_lanes(jnp.where(keyf_ref[kt] >= cand, 1.0, 0.0))

        part = lax.fori_loop(0, nfar, body, part)
        return jnp.sum(part, axis=-1, keepdims=True)

    thr = _kth_largest_key(count_ge, topk, tq)

    for h in range(H_A):
        ql = _dot(q_ref[:, h * HEAD_DIM:(h + 1) * HEAD_DIM], wuk_ref[h]) * (HEAD_DIM ** -0.5)
        ql_ref[h] = ql.astype(BF)

    kvn = ckv_ref[pl.ds(row0, 2 * tq), :]
    seln = keyn_ref[...] >= thr
    for h in range(H_A):
        qh = ql_ref[h]
        init = (jnp.full((tq, 1), NEG, F32), jnp.zeros((tq, 1), F32), jnp.zeros((tq, KV_LORA), F32))
        carry = _online_update(init, _dot_t(qh, kvn) + near_ref[h], seln, kvn)
        bias_far = far_ref[h]

        def far_attn(kt, c):
            r0 = pl.multiple_of(tq + kt * TK, tq)
            kv = ckv_ref[pl.ds(r0, TK), :]
            return _online_update(c, _dot_t(qh, kv) + bias_far, keyf_ref[kt] >= thr, kv)

        _, l, acc = lax.fori_loop(0, nfar, far_attn, carry)
        o_lat = _finish(l, acc)
        o_ref[:, h * HEAD_DIM:(h + 1) * HEAD_DIM] = _dot(o_lat.astype(BF), wuv_ref[h]).astype(o_ref.dtype)


def _dsa(qq, z, kidx_pad, ckv_pad, wuk_t, wuv_r, near_a, far_a, *, seq):
    bsz = qq.shape[0]
    topk = min(TOPK_MAX, seq // 4)
    nqi = H_IDX * D_IDX
    return pl.pallas_call(
        functools.partial(_dsa_kernel, topk=topk),
        grid=(bsz, seq // TQ),
        in_specs=[pl.BlockSpec(memory_space=pltpu.SMEM),
                  pl.BlockSpec((None, TQ, nqi), lambda b, i: (b, i, 0)),
                  pl.BlockSpec((None, TQ, H_A * HEAD_DIM), lambda b, i: (b, i, nqi // (H_A * HEAD_DIM))),
                  pl.BlockSpec((None, TQ, LANES), lambda b, i: (b, i, OFF_WIDX // LANES)),
                  pl.BlockSpec((None, seq + TQ, D_IDX), lambda b, i: (b, 0, 0)),
                  pl.BlockSpec((None, seq + TQ, KV_LORA), lambda b, i: (b, 0, 0)),
                  pl.BlockSpec((H_A, HEAD_DIM, KV_LORA), lambda b, i: (0, 0, 0)),
                  pl.BlockSpec((H_A, KV_LORA, HEAD_DIM), lambda b, i: (0, 0, 0)),
                  pl.BlockSpec((H_A, TQ, 2 * TQ), lambda b, i: (0, 0, 0))],
        out_specs=pl.BlockSpec((None, TQ, H_A * HEAD_DIM), lambda b, i: (b, i, 0)),
        out_shape=jax.ShapeDtypeStruct((bsz, seq, H_A * HEAD_DIM), BF),
        scratch_shapes=[pltpu.VMEM((seq // TK, TQ, TK), jnp.int32),
                        pltpu.VMEM((TQ, 2 * TQ), jnp.int32),
                        pltpu.VMEM((H_IDX, TQ, LANES), F32),
                        pltpu.VMEM((H_A, TQ, KV_LORA), BF)],
        compiler_params=_cparams(("parallel", "parallel")),
        name="dsa_mixer",
    )(far_a, qq, qq, z, kidx_pad, ckv_pad, wuk_t, wuv_r, near_a)


def _cmp_kernel(x_ref, pe_ref, w1_ref, w2_ref, o_ref, xp_ref, *, seq, ncp):
    xp_ref[0:seq, :] = x_ref[...]
    xp_ref[seq:seq + CMP_STRIDE, :] = jnp.zeros((CMP_STRIDE, HEAD_DIM), F32)
    acc = jnp.zeros((ncp, CMP_HIDDEN), F32)
    for l in range(CMP_BLOCK):
        rows = xp_ref[pl.ds(l, ncp, stride=CMP_STRIDE), :]
        blk = rows + pe_ref[l:l + 1, :]
        acc = acc + _dot(blk.astype(BF), w1_ref[l].astype(BF))
    hdn = jax.nn.gelu(acc)
    o_ref[...] = _dot(hdn.astype(BF), w2_ref[...].astype(BF)).astype(o_ref.dtype)


def _compress(z, cmp_pe, cmp_w1, cmp_w2, *, seq):
    bsz = z.shape[0]
    ncp = seq // CMP_STRIDE
    return pl.pallas_call(
        functools.partial(_cmp_kernel, seq=seq, ncp=ncp),
        grid=(bsz, 2, G_B),
        in_specs=[pl.BlockSpec((None, seq, HEAD_DIM), lambda b, c, g: (b, 0, OFF_KVB // HEAD_DIM + c * G_B + g)),
                  pl.BlockSpec((None, CMP_BLOCK, HEAD_DIM), lambda b, c, g: (c, 0, 0)),
                  pl.BlockSpec((None, CMP_BLOCK, HEAD_DIM, CMP_HIDDEN), lambda b, c, g: (c, 0, 0, 0)),
                  pl.BlockSpec((None, CMP_HIDDEN, HEAD_DIM), lambda b, c, g: (c, 0, 0))],
        out_specs=pl.BlockSpec((None, None, None, ncp, HEAD_DIM), lambda b, c, g: (b, c, g, 0, 0)),
        out_shape=jax.ShapeDtypeStruct((bsz, 2, G_B, ncp, HEAD_DIM), BF),
        scratch_shapes=[pltpu.VMEM((seq + CMP_STRIDE, HEAD_DIM), F32)],
        compiler_params=_cparams(("parallel", "parallel", "parallel")),
        name="nsa_compress",
    )(z, cmp_pe, cmp_w1, cmp_w2)


def _nsa_kernel(far_ref, q_ref, gb_ref, kcv_ref, ksl_ref, vsl_ref, kw_ref, vw_ref, et_ref, ov_ref,
                nears_ref, nearw_ref, o_ref, msel_ref, qs_ref, *, seq, npick):
    i = pl.program_id(1)
    tq = TQ
    hpg = H_B // G_B
    ncp = seq // CMP_STRIDE
    far_end = jnp.maximum(i - 1, 0) * tq
    nfar = (far_end + TK - 1) // TK
    row0 = pl.multiple_of(i * tq, tq)
    tpos = i * tq + lax.broadcasted_iota(jnp.int32, (tq, 1), 0)

    sig = jax.nn.sigmoid(gb_ref[...])

    def gate(h, c):
        col = H_IDX + 3 * h + c
        return sig[:, col:col + 1]

    for h in range(H_B):
        qs_ref[h] = (q_ref[:, h * HEAD_DIM:(h + 1) * HEAD_DIM] * (HEAD_DIM ** -0.5)).astype(BF)

    ncol = lax.broadcasted_iota(jnp.int32, (tq, ncp), 1)
    cmp_mask = ncol * CMP_STRIDE + (CMP_BLOCK - 1) <= tpos
    anyc = jnp.where(tpos >= CMP_BLOCK - 1, 1.0, 0.0)
    blk = lax.broadcasted_iota(jnp.int32, (tq, LANES), 1)
    cur = tpos // SEL_BLOCK
    forced = (blk == 0) | (blk == cur) | (blk == cur - 1)
    for g in range(G_B):
        kc = kcv_ref[0, g]
        vc = kcv_ref[1, g]
        psum = jnp.zeros((tq, ncp), F32)
        for j in range(hpg):
            h = g * hpg + j
            s = jnp.where(cmp_mask, _dot_t(qs_ref[h], kc), NEG)
            e = jnp.exp(s - jnp.max(s, axis=-1, keepdims=True))
            p = e * (anyc / jnp.sum(e, axis=-1, keepdims=True))
            psum = psum + p
            o_ref[:, h * HEAD_DIM:(h + 1) * HEAD_DIM] = gate(h, 0) * _dot(p.astype(BF), vc)
        hi = psum.astype(BF)
        lo = (psum - hi.astype(F32)).astype(BF)
        imp = _dot(hi, ov_ref[...]) + _dot(lo, ov_ref[...])
        imp = jnp.where(blk > cur, NEG, jnp.where(forced, FORCE_SCORE, imp))
        keys = _to_key(imp)

        def count_ge(cand, keys=keys):
            return jnp.sum(jnp.where(keys >= cand, 1.0, 0.0), axis=-1, keepdims=True)

        thr = _kth_largest_key(count_ge, npick, tq)
        msel_ref[g] = jnp.where(keys >= thr, 1.0, 0.0).astype(BF)

    qrow = lax.broadcasted_iota(jnp.int32, (tq, 2 * tq), 0)
    jcol = lax.broadcasted_iota(jnp.int32, (tq, 2 * tq), 1)
    causal_n = tq + qrow - jcol >= 0
    etn = et_ref[pl.ds(row0, 2 * tq), :]
    for h in range(H_B):
        g = h // hpg
        lanes = slice(g * HEAD_DIM, (g + 1) * HEAD_DIM)
        qh = qs_ref[h]
        msel = msel_ref[g]
        kn = ksl_ref[pl.ds(row0, 2 * tq), lanes]
        vn = vsl_ref[pl.ds(row0, 2 * tq), lanes]
        mask_n = causal_n & (_dot_t(msel, etn) > 0.5)
        init = (jnp.full((tq, 1), NEG, F32), jnp.zeros((tq, 1), F32), jnp.zeros((tq, HEAD_DIM), F32))
        carry = _online_update(init, _dot_t(qh, kn) + nears_ref[h], mask_n, vn)
        bias_far = far_ref[h]

        def far_attn(kt, c, qh=qh, msel=msel, lanes=lanes, bias_far=bias_far):
            r0 = pl.multiple_of(tq + kt * TK, tq)
            kf = ksl_ref[pl.ds(r0, TK), lanes]
            vf = vsl_ref[pl.ds(r0, TK), lanes]
            pos = kt * TK + lax.broadcasted_iota(jnp.int32, (tq, TK), 1)
            mask = (_dot_t(msel, et_ref[pl.ds(r0, TK), :]) > 0.5) & (pos < far_end)
            return _online_update(c, _dot_t(qh, kf) + bias_far, mask, vf)

        _, l, acc = lax.fori_loop(0, nfar, far_attn, carry)
        o_ref[:, h * HEAD_DIM:(h + 1) * HEAD_DIM] += gate(h, 1) * _finish(l, acc)

    span = WIN_B + tq
    qrw = lax.broadcasted_iota(jnp.int32, (tq, span), 0)
    jcw = lax.broadcasted_iota(jnp.int32, (tq, span), 1)
    distw = WIN_B + qrw - jcw
    mask_w = (distw >= 0) & (distw < WIN_B) & (i * tq - WIN_B + jcw >= 0)
    for h in range(H_B):
        g = h // hpg
        lanes = slice(g * HEAD_DIM, (g + 1) * HEAD_DIM)
        kw = kw_ref[pl.ds(row0, span), lanes]
        vw = vw_ref[pl.ds(row0, span), lanes]
        s = jnp.where(mask_w, _dot_t(qs_ref[h], kw) + nearw_ref[h], NEG)
        e = jnp.exp(s - jnp.max(s, axis=-1, keepdims=True))
        p = e / jnp.sum(e, axis=-1, keepdims=True)
        o_ref[:, h * HEAD_DIM:(h + 1) * HEAD_DIM] += gate(h, 2) * _dot(p.astype(BF), vw)


def _nsa(z, kcv, ksl_pad, vsl_pad, kw_pad, vw_pad, et_pad, overlap, near_s, near_w, far_s, *, seq):
    bsz = z.shape[0]
    ncp = seq // CMP_STRIDE
    npick = min(N_SEL, seq // SEL_BLOCK)
    gw = G_B * HEAD_DIM
    whole = lambda shape: pl.BlockSpec(shape, lambda b, i: (0,) * len(shape))
    perb = lambda rows, w: pl.BlockSpec((None, rows, w), lambda b, i: (b, 0, 0))
    return pl.pallas_call(
        functools.partial(_nsa_kernel, seq=seq, npick=npick),
        grid=(bsz, seq // TQ),
        in_specs=[pl.BlockSpec(memory_space=pltpu.SMEM),
                  pl.BlockSpec((None, TQ, H_B * HEAD_DIM), lambda b, i: (b, i, OFF_QB // (H_B * HEAD_DIM))),
                  pl.BlockSpec((None, TQ, LANES), lambda b, i: (b, i, OFF_WIDX // LANES)),
                  pl.BlockSpec((None, 2, G_B, ncp, HEAD_DIM), lambda b, i: (b, 0, 0, 0, 0)),
                  perb(seq + TQ, gw), perb(seq + TQ, gw), perb(seq + WIN_B, gw), perb(seq + WIN_B, gw),
                  whole((seq + TQ, LANES)), whole((ncp, LANES)),
                  whole((H_B, TQ, 2 * TQ)), whole((H_B, TQ, WIN_B + TQ))],
        out_specs=pl.BlockSpec((None, TQ, H_B * HEAD_DIM), lambda b, i: (b, i, 0)),
        out_shape=jax.ShapeDtypeStruct((bsz, seq, H_B * HEAD_DIM), F32),
        scratch_shapes=[pltpu.VMEM((G_B, TQ, LANES), BF),
                        pltpu.VMEM((H_B, TQ, HEAD_DIM), BF)],
        compiler_params=_cparams(("parallel", "parallel")),
        name="nsa_mixer",
    )(far_s, z, z, kcv, ksl_pad, vsl_pad, kw_pad, vw_pad, et_pad, overlap, near_s, near_w)


def _swa_kernel(sink_ref, q_ref, k_ref, v_ref, near_ref, o_ref):
    i = pl.program_id(1)
    tq = TQ
    hpg = H_C // KV_C
    row0 = pl.multiple_of(i * tq, tq)
    qrow = lax.broadcasted_iota(jnp.int32, (tq, 2 * tq), 0)
    jcol = lax.broadcasted_iota(jnp.int32, (tq, 2 * tq), 1)
    dist = WIN_C + qrow - jcol
    mask = (dist >= 0) & (dist < WIN_C) & (i * tq - WIN_C + jcol >= 0)
    kk = k_ref[pl.ds(row0, 2 * tq), :]
    vv = v_ref[pl.ds(row0, 2 * tq), :]
    for h in range(H_C):
        g = h // hpg
        kg = kk[:, g * HEAD_DIM_C:(g + 1) * HEAD_DIM_C]
        vg = vv[:, g * HEAD_DIM_C:(g + 1) * HEAD_DIM_C]
        qh = (q_ref[:, h * HEAD_DIM_C:(h + 1) * HEAD_DIM_C] * (HEAD_DIM_C ** -0.5)).astype(BF)
        s = jnp.where(mask, _dot_t(qh, kg) + near_ref[h], NEG)
        sink = sink_ref[h]
        m = jnp.maximum(jnp.max(s, axis=-1, keepdims=True), sink)
        e = jnp.exp(s - m)
        p = e / (jnp.sum(e, axis=-1, keepdims=True) + jnp.exp(sink - m))
        o_ref[:, h * HEAD_DIM_C:(h + 1) * HEAD_DIM_C] = _dot(p.astype(BF), vg).astype(o_ref.dtype)


def _swa(z, kc_pad, vc_pad, near_c, sinks, *, seq):
    bsz = z.shape[0]
    kvw = KV_C * HEAD_DIM_C
    return pl.pallas_call(
        _swa_kernel,
        grid=(bsz, seq // TQ),
        in_specs=[pl.BlockSpec(memory_space=pltpu.SMEM),
                  pl.BlockSpec((None, TQ, BRANCH_W), lambda b, i: (b, i, OFF_QC // BRANCH_W)),
                  pl.BlockSpec((None, seq + WIN_C, kvw), lambda b, i: (b, 0, 0)),
                  pl.BlockSpec((None, seq + WIN_C, kvw), lambda b, i: (b, 0, 0)),
                  pl.BlockSpec((H_C, TQ, 2 * TQ), lambda b, i: (0, 0, 0))],
        out_specs=pl.BlockSpec((None, TQ, BRANCH_W), lambda b, i: (b, i, 0)),
        out_shape=jax.ShapeDtypeStruct((bsz, seq, BRANCH_W), BF),
        compiler_params=_cparams(("parallel", "parallel")),
        name="swa_mixer",
    )(sinks, z, kc_pad, vc_pad, near_c)


def _merge_kernel(x_ref, wg_ref, o_ref, wbr_ref, out_ref, acc_ref):
    br = pl.program_id(2)
    gate = jax.nn.sigmoid(_dot(x_ref[...], wg_ref[...]))
    val = _dot(o_ref[...], wbr_ref[...].astype(BF))

    @pl.when(br == 0)
    def _():
        acc_ref[...] = gate * val

    @pl.when(br > 0)
    def _():
        acc_ref[...] += gate * val

    @pl.when(br == N_BRANCH - 1)
    def _():
        out_ref[...] = acc_ref[...].astype(out_ref.dtype)


def _merge(xb, wg, o_all, w_br, *, tm, tn):
    m, d = xb.shape
    nj = d // tn
    return pl.pallas_call(
        _merge_kernel,
        grid=(m // tm, nj, N_BRANCH),
        in_specs=[pl.BlockSpec((tm, d), lambda i, j, r: (i, 0)),
                  pl.BlockSpec((d, tn), lambda i, j, r: (0, r * nj + j)),
                  pl.BlockSpec((None, tm, BRANCH_W), lambda i, j, r: (r, i, 0)),
                  pl.BlockSpec((None, BRANCH_W, tn), lambda i, j, r: (r, 0, j))],
        out_specs=pl.BlockSpec((tm, tn), lambda i, j, r: (i, j)),
        out_shape=jax.ShapeDtypeStruct((m, d), BF),
        scratch_shapes=[pltpu.VMEM((tm, tn), F32)],
        compiler_params=_cparams(("parallel", "parallel", "arbitrary")),
        name="merge_gates",
    )(xb, wg, o_all, w_br)


def _resln_kernel(*refs, nj, tn, alpha, gated):
    if gated:
        a_ref, w_ref, res_ref, g_ref, b_ref, bias_ref, p_ref, wp_ref, add_ref, of_ref, ob_ref, y_ref = refs
    else:
        a_ref, w_ref, res_ref, g_ref, b_ref, of_ref, ob_ref, y_ref = refs
    j = pl.program_id(1)
    y = _dot(a_ref[...], w_ref[...].astype(BF))
    if gated:
        y = jax.nn.sigmoid(y + bias_ref[...]) * _dot(p_ref[...], wp_ref[...].astype(BF)) + add_ref[...]
    y_ref[j] = alpha * res_ref[...] + y

    @pl.when(j == nj - 1)
    def _():
        tm = y_ref.shape[1]
        d = nj * tn
        tot = jnp.zeros((tm, 1), F32)
        for c in range(nj):
            tot = tot + jnp.sum(y_ref[c], axis=-1, keepdims=True)
        mu = tot / d
        sq = jnp.zeros((tm, 1), F32)
        for c in range(nj):
            dc = y_ref[c] - mu
            sq = sq + jnp.sum(dc * dc, axis=-1, keepdims=True)
        rstd = lax.rsqrt(sq / d + LN_EPS)
        for c in range(nj):
            cols = slice(c * tn, (c + 1) * tn)
            o = (y_ref[c] - mu) * rstd * g_ref[:, cols] + b_ref[:, cols]
            of_ref[:, cols] = o
            ob_ref[:, cols] = o.astype(BF)


def _resln(a, w, res, g, b, *, alpha, tm, tn, gated=None, name="res_ln"):
    m, kdim = a.shape
    d = w.shape[1]
    nj = d // tn
    in_specs = [pl.BlockSpec((tm, kdim), lambda i, j: (i, 0)),
                pl.BlockSpec((kdim, tn), lambda i, j: (0, j)),
                pl.BlockSpec((tm, tn), lambda i, j: (i, j)),
                pl.BlockSpec((1, d), lambda i, j: (0, 0)),
                pl.BlockSpec((1, d), lambda i, j: (0, 0))]
    args = [a, w, res, g.reshape(1, d), b.reshape(1, d)]
    if gated is not None:
        bias, p, wp, add = gated
        dp = p.shape[1]
        in_specs += [pl.BlockSpec((1, tn), lambda i, j: (0, j)),
                     pl.BlockSpec((tm, dp), lambda i, j: (i, 0)),
                     pl.BlockSpec((dp, tn), lambda i, j: (0, j)),
                     pl.BlockSpec((tm, tn), lambda i, j: (i, j))]
        args += [bias.reshape(1, d), p, wp, add]
    return pl.pallas_call(
        functools.partial(_resln_kernel, nj=nj, tn=tn, alpha=alpha, gated=gated is not None),
        grid=(m // tm, nj),
        in_specs=in_specs,
        out_specs=[pl.BlockSpec((tm, d), lambda i, j: (i, 0)),
                   pl.BlockSpec((tm, d), lambda i, j: (i, 0))],
        out_shape=[jax.ShapeDtypeStruct((m, d), F32), jax.ShapeDtypeStruct((m, d), BF)],
        scratch_shapes=[pltpu.VMEM((nj, tm, tn), F32)],
        compiler_params=_cparams(("parallel", "arbitrary")),
        name=name,
    )(*args)


HALO = 8


def _ffn_up_kernel(x_ref, wg_ref, wv_ref, cwg_ref, cwv_ref, cbg_ref, cbv_ref, a_ref, hg_ref, hv_ref, *, tm, tiles_per_seq):
    i = pl.program_id(1)
    first = (i % tiles_per_seq) == 0
    x = x_ref[...]

    def conv(h_ref, w_ref, cw_ref, cb_ref):
        @pl.when(first)
        def _():
            h_ref[0:HALO, :] = jnp.zeros((HALO, h_ref.shape[1]), F32)

        @pl.when(jnp.logical_not(first))
        def _():
            h_ref[0:HALO, :] = h_ref[tm:tm + HALO, :]

        h_ref[HALO:HALO + tm, :] = _dot(x, w_ref[...].astype(BF))
        out = cb_ref[...]
        for k in range(CONV_W):
            off = HALO - (CONV_W - 1) + k
            out = out + h_ref[off:off + tm, :] * cw_ref[k:k + 1, :]
        return out

    hg = conv(hg_ref, wg_ref, cwg_ref, cbg_ref)
    hv = conv(hv_ref, wv_ref, cwv_ref, cbv_ref)
    a_ref[...] = (jax.nn.gelu(hg) * hv).astype(a_ref.dtype)


def _ffn_up(xb, w_up, conv_w, conv_b, *, seq, tm, tn):
    m, d = xb.shape
    dff = w_up.shape[1] // 2
    nj = dff // tn
    assert dff % tn == 0 and seq % tm == 0
    cb = conv_b.reshape(1, 2 * dff)
    return pl.pallas_call(
        functools.partial(_ffn_up_kernel, tm=tm, tiles_per_seq=seq // tm),
        grid=(nj, m // tm),
        in_specs=[pl.BlockSpec((tm, d), lambda j, i: (i, 0)),
                  pl.BlockSpec((d, tn), lambda j, i: (0, j)),
                  pl.BlockSpec((d, tn), lambda j, i: (0, j + nj)),
                  pl.BlockSpec((CONV_W, tn), lambda j, i: (0, j)),
                  pl.BlockSpec((CONV_W, tn), lambda j, i: (0, j + nj)),
                  pl.BlockSpec((1, tn), lambda j, i: (0, j)),
                  pl.BlockSpec((1, tn), lambda j, i: (0, j + nj))],
        out_specs=pl.BlockSpec((tm, tn), lambda j, i: (i, j)),
        out_shape=jax.ShapeDtypeStruct((m, dff), BF),
        scratch_shapes=[pltpu.VMEM((tm + HALO, tn), F32), pltpu.VMEM((tm + HALO, tn), F32)],
        compiler_params=_cparams(("parallel", "arbitrary")),
        name="ffn_up_conv_geglu",
    )(xb, w_up, w_up, conv_w, conv_w, cb, cb)


def _t5_bucket(dist):
    dist = jnp.maximum(dist, 0)
    max_exact = N_BUCKETS // 2
    d = jnp.maximum(dist, 1).astype(F32)
    large = max_exact + (jnp.log(d / max_exact) / math.log(MAX_DISTANCE / max_exact)
                         * (N_BUCKETS - max_exact)).astype(jnp.int32)
    large = jnp.minimum(large, N_BUCKETS - 1)
    return jnp.where(dist < max_exact, dist, large)


def _near_bias(tab, offset, width):
    q = np.arange(TQ)[:, None]
    j = np.arange(width)[None, :]
    dist = np.clip(offset + q - j, 0, MAX_DISTANCE)
    return jnp.transpose(tab[_t5_bucket(jnp.asarray(dist, jnp.int32))], (2, 0, 1))


def _pad_front(a, n):
    return jnp.pad(a, ((0, 0), (n, 0), (0, 0)))


def _layer(x, xb, p_b, w_in, g_cq, g_ckv, g_kidx, b_kidx, w_qidx, w_uq, w_uk, w_uv,
           cmp_pe, cmp_w1, cmp_w2, sinks, w_br, w_o, ln1_g, ln1_b,
           w_up, conv_w, conv_b, w_down, w_pg, b_pg, w_pp, ln2_g, ln2_b, rel_bias, consts, *, bsz, seq, alpha):
    m, d = x.shape
    s_cq, s_ckv, s_kidx, s_widx = 0, Q_LORA, Q_LORA + KV_LORA, Q_LORA + KV_LORA + D_IDX
    s_qb = s_widx + H_IDX
    s_kvb = s_qb + H_B * HEAD_DIM
    s_gb = s_kvb + 6 * G_B * HEAD_DIM
    s_qc = s_gb + 3 * H_B
    s_kc = s_qc + H_C * HEAD_DIM_C
    s_vc = s_kc + KV_C * HEAD_DIM_C
    s_gm = s_vc + KV_C * HEAD_DIM_C
    cols = lambda a, b: w_in[:, a:b]
    w_small = jnp.concatenate(
        [cols(s_qb, s_kvb), cols(s_qc, s_kc), cols(s_cq, s_kidx), cols(s_kidx, s_widx), cols(s_kvb, s_gb),
         cols(s_kc, s_vc), cols(s_vc, s_gm), cols(s_widx, s_qb), cols(s_gb, s_qc),
         jnp.zeros((d, N_SMALL - s_gm), w_in.dtype)], axis=1).astype(BF)
    w_gate = w_in[:, s_gm:].astype(BF)

    z = _mm(xb, w_small, tm=1024, tn=512, name="in_proj")
    cq_n, ckv_n, kidx_n = _prep(z, g_cq, g_ckv, g_kidx, b_kidx)
    w_qq = jnp.concatenate([w_qidx, w_uq], axis=1).astype(BF)
    qq = _mm(cq_n, w_qq, tm=1024, tn=512, out_dtype=BF, name="q_proj")

    z3 = z.reshape(bsz, seq, N_SMALL)
    b3 = lambda a: a.reshape(bsz, seq, a.shape[-1])
    seg = lambda off, w: z3[:, :, off:off + w].astype(BF)

    tab_a = rel_bias[:, :H_A]
    o_a = _dsa(b3(qq), z3, _pad_front(b3(kidx_n), TQ), _pad_front(b3(ckv_n), TQ),
               jnp.transpose(w_uk, (1, 2, 0)).astype(BF), jnp.transpose(w_uv, (1, 0, 2)).astype(BF),
               _near_bias(tab_a, TQ, 2 * TQ), tab_a[N_BUCKETS - 1], seq=seq)

    tab_b = rel_bias[:, H_A:H_A + H_B]
    gw = G_B * HEAD_DIM
    kcv = _compress(z3, cmp_pe, cmp_w1, cmp_w2, seq=seq)
    o_b = _nsa(z3, kcv,
               _pad_front(seg(OFF_KVB + 2 * gw, gw), TQ), _pad_front(seg(OFF_KVB + 3 * gw, gw), TQ),
               _pad_front(seg(OFF_KVB + 4 * gw, gw), WIN_B), _pad_front(seg(OFF_KVB + 5 * gw, gw), WIN_B),
               consts["et_pad"], consts["overlap"],
               _near_bias(tab_b, TQ, 2 * TQ), _near_bias(tab_b, WIN_B, WIN_B + TQ), tab_b[N_BUCKETS - 1], seq=seq)

    tab_c = rel_bias[:, H_A + H_B:]
    kvw = KV_C * HEAD_DIM_C
    o_c = _swa(z3, _pad_front(seg(OFF_KC, kvw), WIN_C), _pad_front(seg(OFF_VC, kvw), WIN_C),
               _near_bias(tab_c, WIN_C, 2 * TQ), sinks, seq=seq)

    o_all = jnp.stack([o_a.reshape(m, BRANCH_W), o_b.reshape(m, BRANCH_W).astype(BF), o_c.reshape(m, BRANCH_W)])
    merged = _merge(xb, w_gate, o_all, w_br, tm=1024, tn=512)
    x1, x1b = _resln(merged, w_o.astype(BF), x, ln1_g, ln1_b, alpha=alpha, tm=256, tn=512, name="attn_out_ln1")

    a = _ffn_up(x1b, w_up, conv_w, conv_b, seq=seq, tm=min(1024, seq), tn=256)
    dff = w_down.shape[0]
    ffn = _mm(a, w_down.astype(BF), tm=1024, tn=512, tk=dff // 2 if (dff // 2) % LANES == 0 else dff, name="ffn_down")
    return _resln(x1b, w_pg.astype(BF), x1, ln2_g, ln2_b, alpha=alpha, tm=256, tn=512,
                  gated=(b_pg, p_b, w_pp, ffn), name="ple_ln2")


def kernel(x, p, w_in, g_cq, g_ckv, g_kidx, b_kidx, w_qidx, w_uq, w_uk, w_uv, cmp_pe, cmp_w1, cmp_w2, sinks, w_br, w_o, ln1_g, ln1_b, w_up, conv_w, conv_b, w_down, w_pg, b_pg, w_pp, ln2_g, ln2_b, rel_bias):
    bsz, seq, d = x.shape
    depth = p.shape[0]
    alpha = (2 * depth) ** 0.25
    m = bsz * seq
    assert seq % TK == 0 and d % 512 == 0

    key = np.arange(seq)[:, None]
    blk = np.arange(LANES)[None, :]
    et = (key // SEL_BLOCK == blk).astype(np.float32)
    et_pad = jnp.asarray(np.concatenate([np.zeros((TQ, LANES), np.float32), et]), BF)
    ncp = seq // CMP_STRIDE
    cs = np.arange(ncp)[:, None] * CMP_STRIDE
    bs = blk * SEL_BLOCK
    n_cmp = (seq - CMP_BLOCK) // CMP_STRIDE + 1
    ov = ((cs < bs + SEL_BLOCK) & (cs + CMP_BLOCK > bs) & (blk < seq // SEL_BLOCK)
          & (np.arange(ncp)[:, None] < n_cmp)).astype(np.float32)
    consts = {"et_pad": et_pad, "overlap": jnp.asarray(ov, BF)}

    xf = x.reshape(m, d)
    xb = xf.astype(BF)
    for i in range(depth):
        xf, xb = _layer(xf, xb, p[i].reshape(m, -1).astype(BF), w_in[i], g_cq[i], g_ckv[i], g_kidx[i], b_kidx[i],
                        w_qidx[i], w_uq[i], w_uk[i], w_uv[i], cmp_pe[i], cmp_w1[i], cmp_w2[i], sinks[i],
                        w_br[i], w_o[i], ln1_g[i], ln1_b[i], w_up[i], conv_w[i], conv_b[i], w_down[i],
                        w_pg[i], b_pg[i], w_pp[i], ln2_g[i], ln2_b[i], rel_bias, consts,
                        bsz=bsz, seq=seq, alpha=alpha)
    return xf.reshape(bsz, seq, d)
```

```python
import functools
import math

import jax
import jax.numpy as jnp
import numpy as np
from jax import lax
from jax.experimental import pallas as pl
from jax.experimental.pallas import tpu as pltpu

HEAD_DIM = 128
NEG = -1e30
FORCE_SCORE = 1e4
LN_EPS = 1e-5
N_BUCKETS = 32
MAX_DISTANCE = 128
H_A = 8
Q_LORA = 768
KV_LORA = 256
H_IDX = 32
D_IDX = 128
TOPK_MAX = 256
H_B = 8
G_B = 2
CMP_BLOCK = 32
CMP_STRIDE = 16
CMP_HIDDEN = 128
SEL_BLOCK = 64
N_SEL = 16
WIN_B = 512
H_C = 16
KV_C = 2
HEAD_DIM_C = 64
WIN_C = 128
N_BRANCH = 3
BRANCH_W = 1024
CONV_W = 3

LANES = 128
TQ = 128
TK = 512
VMEM_LIMIT = 56 * 1024 * 1024
INT_MIN = -2 ** 31

OFF_QB = 0
OFF_QC = 1024
OFF_CQ = 2048
OFF_CKV = 2816
OFF_KIDX = 3072
OFF_KVB = 3200
OFF_KC = 4736
OFF_VC = 4864
OFF_WIDX = 4992
N_SMALL = 5120

BF = jnp.bfloat16
F32 = jnp.float32


def _cparams(sem, vmem=VMEM_LIMIT):
    return pltpu.CompilerParams(dimension_semantics=sem, vmem_limit_bytes=vmem)


def _dot(a, b):
    return jnp.dot(a, b, preferred_element_type=F32)


def _dot_t(a, b):
    return lax.dot_general(a, b, (((1,), (1,)), ((), ())), preferred_element_type=F32)


def _mm_kernel(a_ref, b_ref, o_ref, acc_ref, *, nk):
    part = _dot(a_ref[...], b_ref[...].astype(BF))
    if nk == 1:
        o_ref[...] = part.astype(o_ref.dtype)
    else:
        k = pl.program_id(2)

        @pl.when(k == 0)
        def _():
            acc_ref[...] = part

        @pl.when(k > 0)
        def _():
            acc_ref[...] += part

        @pl.when(k == nk - 1)
        def _():
            o_ref[...] = acc_ref[...].astype(o_ref.dtype)


def _mm(a, b, *, tm, tn, tk=None, out_dtype=F32, name="mm"):
    m, kdim = a.shape
    n = b.shape[1]
    tk = kdim if tk is None else tk
    nk = kdim // tk
    assert m % tm == 0 and kdim % tk == 0
    acc_shape = (tm, tn) if nk > 1 else (8, LANES)
    return pl.pallas_call(
        functools.partial(_mm_kernel, nk=nk),
        grid=(m // tm, pl.cdiv(n, tn), nk),
        in_specs=[pl.BlockSpec((tm, tk), lambda i, j, k: (i, k)),
                  pl.BlockSpec((tk, tn), lambda i, j, k: (k, j))],
        out_specs=pl.BlockSpec((tm, tn), lambda i, j, k: (i, j)),
        out_shape=jax.ShapeDtypeStruct((m, n), out_dtype),
        scratch_shapes=[pltpu.VMEM(acc_shape, F32)],
        compiler_params=_cparams(("parallel", "parallel", "arbitrary")),
        name=name,
    )(a, b)


def _prep_kernel(c_ref, k_ref, gq_ref, gkv_ref, gk_ref, bk_ref, cq_o, ckv_o, kidx_o):
    c = c_ref[...]
    cq = c[:, :Q_LORA]
    ckv = c[:, Q_LORA:]
    cq = cq * lax.rsqrt(jnp.mean(cq * cq, axis=-1, keepdims=True) + LN_EPS) * gq_ref[...]
    ckv = ckv * lax.rsqrt(jnp.mean(ckv * ckv, axis=-1, keepdims=True) + LN_EPS) * gkv_ref[...]
    k = k_ref[...]
    mu = jnp.mean(k, axis=-1, keepdims=True)
    kc = k - mu
    var = jnp.mean(kc * kc, axis=-1, keepdims=True)
    kn = kc * lax.rsqrt(var + LN_EPS) * gk_ref[...] + bk_ref[...]
    cq_o[...] = cq.astype(BF)
    ckv_o[...] = ckv.astype(BF)
    kidx_o[...] = kn.astype(BF)


def _prep(z, g_cq, g_ckv, g_kidx, b_kidx, *, tm=512):
    m = z.shape[0]
    wc = Q_LORA + KV_LORA
    row = lambda a: a.reshape(1, -1)
    full = lambda n: pl.BlockSpec((1, n), lambda i: (0, 0))
    return pl.pallas_call(
        _prep_kernel,
        grid=(m // tm,),
        in_specs=[pl.BlockSpec((tm, wc), lambda i: (i, OFF_CQ // wc)),
                  pl.BlockSpec((tm, D_IDX), lambda i: (i, OFF_KIDX // D_IDX)),
                  full(Q_LORA), full(KV_LORA), full(D_IDX), full(D_IDX)],
        out_specs=[pl.BlockSpec((tm, Q_LORA), lambda i: (i, 0)),
                   pl.BlockSpec((tm, KV_LORA), lambda i: (i, 0)),
                   pl.BlockSpec((tm, D_IDX), lambda i: (i, 0))],
        out_shape=[jax.ShapeDtypeStruct((m, Q_LORA), BF),
                   jax.ShapeDtypeStruct((m, KV_LORA), BF),
                   jax.ShapeDtypeStruct((m, D_IDX), BF)],
        compiler_params=_cparams(("parallel",)),
        name="prep_norms",
    )(z, z, row(g_cq), row(g_ckv), row(g_kidx), row(b_kidx))


def _to_key(x):
    b = lax.bitcast_convert_type(x, jnp.int32)
    return b ^ ((b >> 31) & jnp.int32(0x7FFFFFFF))


def _kth_largest_key(count_ge, k, rows):
    kf = jnp.float32(k)
    zero = jnp.zeros((rows, 1), jnp.int32)
    t0 = jnp.where(count_ge(zero) >= kf, zero, jnp.full((rows, 1), INT_MIN, jnp.int32))

    def body(it, t):
        cand = t + jnp.left_shift(jnp.int32(1), jnp.int32(30) - it)
        return jnp.where(count_ge(cand) >= kf, cand, t)

    t = lax.fori_loop(0, 31, body, t0)
    return jnp.maximum(t, jnp.int32(INT_MIN + 1))


def _fold_lanes(x):
    n = x.shape[1] // LANES
    out = x[:, :LANES]
    for c in range(1, n):
        out = out + x[:, c * LANES:(c + 1) * LANES]
    return out


MASKED = 2.0 * NEG


def _online_init(m_ref, l_ref, acc_ref, h):
    m_ref[h] = jnp.full(m_ref.shape[1:], NEG, F32)
    l_ref[h] = jnp.zeros(l_ref.shape[1:], F32)
    acc_ref[h] = jnp.zeros(acc_ref.shape[1:], F32)


def _online_update(m_ref, l_ref, acc_ref, heads, logits, mask, v):
    scaled = []
    for h, s in zip(heads, logits):
        s = jnp.where(mask, s, MASKED)
        m_old = m_ref[h]
        m_new = jnp.maximum(m_old, jnp.max(s, axis=-1, keepdims=True))
        alpha = jnp.exp(m_old - m_new)
        p = jnp.exp(s - m_new)
        l_ref[h] = alpha * l_ref[h] + jnp.sum(p, axis=-1, keepdims=True)
        m_ref[h] = m_new
        scaled.append((alpha, p.astype(BF)))
    for h, (alpha, p) in zip(heads, scaled):
        acc_ref[h] = alpha * acc_ref[h] + _dot(p, v)


def _online_result(l_ref, acc_ref, h):
    l = l_ref[h]
    return acc_ref[h] * jnp.where(l > 0.0, 1.0 / l, 0.0)


def _dsa_kernel(far_ref, qi_ref, q_ref, w_ref, kidx_ref, ckv_ref, wuk_ref, wuv_ref, near_ref,
                o_ref, keyf_ref, keyn_ref, wb_ref, ql_ref, m_ref, l_ref, acc_ref, *, topk):
    i = pl.program_id(1)
    tq = TQ
    far_end = jnp.maximum(i - 1, 0) * tq
    nfar = (far_end + TK - 1) // TK

    wsc = w_ref[...] * (D_IDX ** -0.5 * H_IDX ** -0.5)
    for h in range(H_IDX):
        wb_ref[h] = jnp.broadcast_to(wsc[:, h:h + 1], (tq, LANES))

    def scores(krows):
        n = krows.shape[0]
        acc = jnp.zeros((tq, n), F32)
        for h in range(H_IDX):
            s = _dot_t(qi_ref[:, h * D_IDX:(h + 1) * D_IDX], krows)
            wbh = wb_ref[h]
            acc = acc + jnp.maximum(s, 0.0) * jnp.concatenate([wbh] * (n // LANES), axis=1)
        return acc

    qrow = lax.broadcasted_iota(jnp.int32, (tq, 2 * tq), 0)
    jcol = lax.broadcasted_iota(jnp.int32, (tq, 2 * tq), 1)
    valid_n = (tq + qrow - jcol >= 0) & ((i - 1) * tq + jcol >= 0)
    row0 = pl.multiple_of(i * tq, tq)
    sn = scores(kidx_ref[pl.ds(row0, 2 * tq), :])
    keyn_ref[...] = jnp.where(valid_n, _to_key(sn), jnp.int32(INT_MIN))

    def far_scores(kt, c):
        r0 = pl.multiple_of(tq + kt * TK, tq)
        s = scores(kidx_ref[pl.ds(r0, TK), :])
        pos = kt * TK + lax.broadcasted_iota(jnp.int32, (tq, TK), 1)
        keyf_ref[kt] = jnp.where(pos < far_end, _to_key(s), jnp.int32(INT_MIN))
        return c

    lax.fori_loop(0, nfar, far_scores, 0)

    def count_ge(cand):
        part = _fold_lanes(jnp.where(keyn_ref[...] >= cand, 1.0, 0.0))

        def body(kt, a):
            return a + _fold_lanes(jnp.where(keyf_ref[kt] >= cand, 1.0, 0.0))

        part = lax.fori_loop(0, nfar, body, part)
        return jnp.sum(part, axis=-1, keepdims=True)

    thr = _kth_largest_key(count_ge, topk, tq)

    for h in range(H_A):
        ql = _dot(q_ref[:, h * HEAD_DIM:(h + 1) * HEAD_DIM], wuk_ref[h]) * (HEAD_DIM ** -0.5)
        ql_ref[h] = ql.astype(BF)

    kvn = ckv_ref[pl.ds(row0, 2 * tq), :]
    seln = keyn_ref[...] >= thr
    heads = range(H_A)
    for h in heads:
        _online_init(m_ref, l_ref, acc_ref, h)
    _online_update(m_ref, l_ref, acc_ref, heads, [_dot_t(ql_ref[h], kvn) + near_ref[h] for h in heads], seln, kvn)

    def far_attn(kt, c):
        r0 = pl.multiple_of(tq + kt * TK, tq)
        kv = ckv_ref[pl.ds(r0, TK), :]
        _online_update(m_ref, l_ref, acc_ref, heads, [_dot_t(ql_ref[h], kv) + far_ref[h] for h in heads],
                       keyf_ref[kt] >= thr, kv)
        return c

    lax.fori_loop(0, nfar, far_attn, 0)
    for h in range(H_A):
        o_lat = _online_result(l_ref, acc_ref, h)
        o_ref[:, h * HEAD_DIM:(h + 1) * HEAD_DIM] = _dot(o_lat.astype(BF), wuv_ref[h]).astype(o_ref.dtype)


def _dsa(qq, z, kidx_pad, ckv_pad, wuk_t, wuv_r, near_a, far_a, *, seq):
    bsz = qq.shape[0]
    topk = min(TOPK_MAX, seq // 4)
    nqi = H_IDX * D_IDX
    return pl.pallas_call(
        functools.partial(_dsa_kernel, topk=topk),
        grid=(bsz, seq // TQ),
        in_specs=[pl.BlockSpec(memory_space=pltpu.SMEM),
                  pl.BlockSpec((None, TQ, nqi), lambda b, i: (b, i, 0)),
                  pl.BlockSpec((None, TQ, H_A * HEAD_DIM), lambda b, i: (b, i, nqi // (H_A * HEAD_DIM))),
                  pl.BlockSpec((None, TQ, LANES), lambda b, i: (b, i, OFF_WIDX // LANES)),
                  pl.BlockSpec((None, seq + TQ, D_IDX), lambda b, i: (b, 0, 0)),
                  pl.BlockSpec((None, seq + TQ, KV_LORA), lambda b, i: (b, 0, 0)),
                  pl.BlockSpec((H_A, HEAD_DIM, KV_LORA), lambda b, i: (0, 0, 0)),
                  pl.BlockSpec((H_A, KV_LORA, HEAD_DIM), lambda b, i: (0, 0, 0)),
                  pl.BlockSpec((H_A, TQ, 2 * TQ), lambda b, i: (0, 0, 0))],
        out_specs=pl.BlockSpec((None, TQ, H_A * HEAD_DIM), lambda b, i: (b, i, 0)),
        out_shape=jax.ShapeDtypeStruct((bsz, seq, H_A * HEAD_DIM), BF),
        scratch_shapes=[pltpu.VMEM((seq // TK, TQ, TK), jnp.int32),
                        pltpu.VMEM((TQ, 2 * TQ), jnp.int32),
                        pltpu.VMEM((H_IDX, TQ, LANES), F32),
                        pltpu.VMEM((H_A, TQ, KV_LORA), BF),
                        pltpu.VMEM((H_A, TQ, 1), F32),
                        pltpu.VMEM((H_A, TQ, 1), F32),
                        pltpu.VMEM((H_A, TQ, KV_LORA), F32)],
        compiler_params=_cparams(("parallel", "parallel")),
        name="dsa_mixer",
    )(far_a, qq, qq, z, kidx_pad, ckv_pad, wuk_t, wuv_r, near_a)


def _cmp_kernel(x_ref, pe_ref, w1_ref, w2_ref, o_ref, xp_ref, *, seq, ncp):
    xp_ref[0:seq, :] = x_ref[...]
    xp_ref[seq:seq + CMP_STRIDE, :] = jnp.zeros((CMP_STRIDE, HEAD_DIM), F32)
    acc = jnp.zeros((ncp, CMP_HIDDEN), F32)
    for l in range(CMP_BLOCK):
        rows = xp_ref[pl.ds(l, ncp, stride=CMP_STRIDE), :]
        blk = rows + pe_ref[l:l + 1, :]
        acc = acc + _dot(blk.astype(BF), w1_ref[l].astype(BF))
    hdn = jax.nn.gelu(acc)
    o_ref[...] = _dot(hdn.astype(BF), w2_ref[...].astype(BF)).astype(o_ref.dtype)


def _compress(z, cmp_pe, cmp_w1, cmp_w2, *, seq):
    bsz = z.shape[0]
    ncp = seq // CMP_STRIDE
    return pl.pallas_call(
        functools.partial(_cmp_kernel, seq=seq, ncp=ncp),
        grid=(bsz, 2, G_B),
        in_specs=[pl.BlockSpec((None, seq, HEAD_DIM), lambda b, c, g: (b, 0, OFF_KVB // HEAD_DIM + c * G_B + g)),
                  pl.BlockSpec((None, CMP_BLOCK, HEAD_DIM), lambda b, c, g: (c, 0, 0)),
                  pl.BlockSpec((None, CMP_BLOCK, HEAD_DIM, CMP_HIDDEN), lambda b, c, g: (c, 0, 0, 0)),
                  pl.BlockSpec((None, CMP_HIDDEN, HEAD_DIM), lambda b, c, g: (c, 0, 0))],
        out_specs=pl.BlockSpec((None, None, None, ncp, HEAD_DIM), lambda b, c, g: (b, c, g, 0, 0)),
        out_shape=jax.ShapeDtypeStruct((bsz, 2, G_B, ncp, HEAD_DIM), BF),
        scratch_shapes=[pltpu.VMEM((seq + CMP_STRIDE, HEAD_DIM), F32)],
        compiler_params=_cparams(("parallel", "parallel", "parallel")),
        name="nsa_compress",
    )(z, cmp_pe, cmp_w1, cmp_w2)


def _nsa_kernel(far_ref, q_ref, gb_ref, kcv_ref, ksl_ref, vsl_ref, kw_ref, vw_ref, et_ref, ov_ref,
                nears_ref, nearw_ref, o_ref, msel_ref, qs_ref, m_ref, l_ref, acc_ref, *, seq, npick):
    i = pl.program_id(1)
    tq = TQ
    hpg = H_B // G_B
    ncp = seq // CMP_STRIDE
    far_end = jnp.maximum(i - 1, 0) * tq
    nfar = (far_end + TK - 1) // TK
    row0 = pl.multiple_of(i * tq, tq)
    tpos = i * tq + lax.broadcasted_iota(jnp.int32, (tq, 1), 0)

    sig = jax.nn.sigmoid(gb_ref[...])

    def gate(h, c):
        col = H_IDX + 3 * h + c
        return sig[:, col:col + 1]

    for h in range(H_B):
        qs_ref[h] = (q_ref[:, h * HEAD_DIM:(h + 1) * HEAD_DIM] * (HEAD_DIM ** -0.5)).astype(BF)

    ncol = lax.broadcasted_iota(jnp.int32, (tq, ncp), 1)
    cmp_mask = ncol * CMP_STRIDE + (CMP_BLOCK - 1) <= tpos
    anyc = jnp.where(tpos >= CMP_BLOCK - 1, 1.0, 0.0)
    blk = lax.broadcasted_iota(jnp.int32, (tq, LANES), 1)
    cur = tpos // SEL_BLOCK
    forced = (blk == 0) | (blk == cur) | (blk == cur - 1)
    for g in range(G_B):
        kc = kcv_ref[0, g]
        vc = kcv_ref[1, g]
        psum = jnp.zeros((tq, ncp), F32)
        for j in range(hpg):
            h = g * hpg + j
            s = jnp.where(cmp_mask, _dot_t(qs_ref[h], kc), NEG)
            e = jnp.exp(s - jnp.max(s, axis=-1, keepdims=True))
            p = e * (anyc / jnp.sum(e, axis=-1, keepdims=True))
            psum = psum + p
            o_ref[:, h * HEAD_DIM:(h + 1) * HEAD_DIM] = gate(h, 0) * _dot(p.astype(BF), vc)
        hi = psum.astype(BF)
        lo = (psum - hi.astype(F32)).astype(BF)
        imp = _dot(hi, ov_ref[...]) + _dot(lo, ov_ref[...])
        imp = jnp.where(blk > cur, NEG, jnp.where(forced, FORCE_SCORE, imp))
        keys = _to_key(imp)

        def count_ge(cand, keys=keys):
            return jnp.sum(jnp.where(keys >= cand, 1.0, 0.0), axis=-1, keepdims=True)

        thr = _kth_largest_key(count_ge, npick, tq)
        msel_ref[g] = jnp.where(keys >= thr, 1.0, 0.0).astype(BF)

    qrow = lax.broadcasted_iota(jnp.int32, (tq, 2 * tq), 0)
    jcol = lax.broadcasted_iota(jnp.int32, (tq, 2 * tq), 1)
    causal_n = tq + qrow - jcol >= 0
    etn = et_ref[pl.ds(row0, 2 * tq), :]
    for g in range(G_B):
        lanes = slice(g * HEAD_DIM, (g + 1) * HEAD_DIM)
        kn = ksl_ref[pl.ds(row0, 2 * tq), lanes]
        vn = vsl_ref[pl.ds(row0, 2 * tq), lanes]
        mask_n = causal_n & (_dot_t(msel_ref[g], etn) > 0.5)
        heads = range(g * hpg, (g + 1) * hpg)
        for h in heads:
            _online_init(m_ref, l_ref, acc_ref, h)
        _online_update(m_ref, l_ref, acc_ref, heads, [_dot_t(qs_ref[h], kn) + nears_ref[h] for h in heads], mask_n, vn)

    def far_attn(kt, c):
        r0 = pl.multiple_of(tq + kt * TK, tq)
        pos = kt * TK + lax.broadcasted_iota(jnp.int32, (tq, TK), 1)
        ett = et_ref[pl.ds(r0, TK), :]
        for g in range(G_B):
            lanes = slice(g * HEAD_DIM, (g + 1) * HEAD_DIM)
            kf = ksl_ref[pl.ds(r0, TK), lanes]
            vf = vsl_ref[pl.ds(r0, TK), lanes]
            mask = (_dot_t(msel_ref[g], ett) > 0.5) & (pos < far_end)
            heads = range(g * hpg, (g + 1) * hpg)
            _online_update(m_ref, l_ref, acc_ref, heads, [_dot_t(qs_ref[h], kf) + far_ref[h] for h in heads], mask, vf)
        return c

    lax.fori_loop(0, nfar, far_attn, 0)
    for h in range(H_B):
        o_ref[:, h * HEAD_DIM:(h + 1) * HEAD_DIM] += gate(h, 1) * _online_result(l_ref, acc_ref, h)

    span = WIN_B + tq
    qrw = lax.broadcasted_iota(jnp.int32, (tq, span), 0)
    jcw = lax.broadcasted_iota(jnp.int32, (tq, span), 1)
    distw = WIN_B + qrw - jcw
    mask_w = (distw >= 0) & (distw < WIN_B) & (i * tq - WIN_B + jcw >= 0)
    for h in range(H_B):
        g = h // hpg
        lanes = slice(g * HEAD_DIM, (g + 1) * HEAD_DIM)
        kw = kw_ref[pl.ds(row0, span), lanes]
        vw = vw_ref[pl.ds(row0, span), lanes]
        s = jnp.where(mask_w, _dot_t(qs_ref[h], kw) + nearw_ref[h], NEG)
        e = jnp.exp(s - jnp.max(s, axis=-1, keepdims=True))
        p = e / jnp.sum(e, axis=-1, keepdims=True)
        o_ref[:, h * HEAD_DIM:(h + 1) * HEAD_DIM] += gate(h, 2) * _dot(p.astype(BF), vw)


def _nsa(z, kcv, ksl_pad, vsl_pad, kw_pad, vw_pad, et_pad, overlap, near_s, near_w, far_s, *, seq):
    bsz = z.shape[0]
    ncp = seq // CMP_STRIDE
    npick = min(N_SEL, seq // SEL_BLOCK)
    gw = G_B * HEAD_DIM
    whole = lambda shape: pl.BlockSpec(shape, lambda b, i: (0,) * len(shape))
    perb = lambda rows, w: pl.BlockSpec((None, rows, w), lambda b, i: (b, 0, 0))
    return pl.pallas_call(
        functools.partial(_nsa_kernel, seq=seq, npick=npick),
        grid=(bsz, seq // TQ),
        in_specs=[pl.BlockSpec(memory_space=pltpu.SMEM),
                  pl.BlockSpec((None, TQ, H_B * HEAD_DIM), lambda b, i: (b, i, OFF_QB // (H_B * HEAD_DIM))),
                  pl.BlockSpec((None, TQ, LANES), lambda b, i: (b, i, OFF_WIDX // LANES)),
                  pl.BlockSpec((None, 2, G_B, ncp, HEAD_DIM), lambda b, i: (b, 0, 0, 0, 0)),
                  perb(seq + TQ, gw), perb(seq + TQ, gw), perb(seq + WIN_B, gw), perb(seq + WIN_B, gw),
                  whole((seq + TQ, LANES)), whole((ncp, LANES)),
                  whole((H_B, TQ, 2 * TQ)), whole((H_B, TQ, WIN_B + TQ))],
        out_specs=pl.BlockSpec((None, TQ, H_B * HEAD_DIM), lambda b, i: (b, i, 0)),
        out_shape=jax.ShapeDtypeStruct((bsz, seq, H_B * HEAD_DIM), F32),
        scratch_shapes=[pltpu.VMEM((G_B, TQ, LANES), BF),
                        pltpu.VMEM((H_B, TQ, HEAD_DIM), BF),
                        pltpu.VMEM((H_B, TQ, 1), F32),
                        pltpu.VMEM((H_B, TQ, 1), F32),
                        pltpu.VMEM((H_B, TQ, HEAD_DIM), F32)],
        compiler_params=_cparams(("parallel", "parallel")),
        name="nsa_mixer",
    )(far_s, z, z, kcv, ksl_pad, vsl_pad, kw_pad, vw_pad, et_pad, overlap, near_s, near_w)


def _swa_kernel(sink_ref, q_ref, k_ref, v_ref, near_ref, o_ref):
    i = pl.program_id(1)
    tq = TQ
    hpg = H_C // KV_C
    row0 = pl.multiple_of(i * tq, tq)
    qrow = lax.broadcasted_iota(jnp.int32, (tq, 2 * tq), 0)
    jcol = lax.broadcasted_iota(jnp.int32, (tq, 2 * tq), 1)
    dist = WIN_C + qrow - jcol
    mask = (dist >= 0) & (dist < WIN_C) & (i * tq - WIN_C + jcol >= 0)
    kk = k_ref[pl.ds(row0, 2 * tq), :]
    vv = v_ref[pl.ds(row0, 2 * tq), :]
    for h in range(H_C):
        g = h // hpg
        kg = kk[:, g * HEAD_DIM_C:(g + 1) * HEAD_DIM_C]
        vg = vv[:, g * HEAD_DIM_C:(g + 1) * HEAD_DIM_C]
        qh = (q_ref[:, h * HEAD_DIM_C:(h + 1) * HEAD_DIM_C] * (HEAD_DIM_C ** -0.5)).astype(BF)
        s = jnp.where(mask, _dot_t(qh, kg) + near_ref[h], NEG)
        sink = sink_ref[h]
        m = jnp.maximum(jnp.max(s, axis=-1, keepdims=True), sink)
        e = jnp.exp(s - m)
        p = e / (jnp.sum(e, axis=-1, keepdims=True) + jnp.exp(sink - m))
        o_ref[:, h * HEAD_DIM_C:(h + 1) * HEAD_DIM_C] = _dot(p.astype(BF), vg).astype(o_ref.dtype)


def _swa(z, kc_pad, vc_pad, near_c, sinks, *, seq):
    bsz = z.shape[0]
    kvw = KV_C * HEAD_DIM_C
    return pl.pallas_call(
        _swa_kernel,
        grid=(bsz, seq // TQ),
        in_specs=[pl.BlockSpec(memory_space=pltpu.SMEM),
                  pl.BlockSpec((None, TQ, BRANCH_W), lambda b, i: (b, i, OFF_QC // BRANCH_W)),
                  pl.BlockSpec((None, seq + WIN_C, kvw), lambda b, i: (b, 0, 0)),
                  pl.BlockSpec((None, seq + WIN_C, kvw), lambda b, i: (b, 0, 0)),
                  pl.BlockSpec((H_C, TQ, 2 * TQ), lambda b, i: (0, 0, 0))],
        out_specs=pl.BlockSpec((None, TQ, BRANCH_W), lambda b, i: (b, i, 0)),
        out_shape=jax.ShapeDtypeStruct((bsz, seq, BRANCH_W), BF),
        compiler_params=_cparams(("parallel", "parallel")),
        name="swa_mixer",
    )(sinks, z, kc_pad, vc_pad, near_c)


def _merge_kernel(x_ref, wg_ref, o_ref, wbr_ref, out_ref, acc_ref):
    br = pl.program_id(2)
    gate = jax.nn.sigmoid(_dot(x_ref[...], wg_ref[...]))
    val = _dot(o_ref[...], wbr_ref[...].astype(BF))

    @pl.when(br == 0)
    def _():
        acc_ref[...] = gate * val

    @pl.when(br > 0)
    def _():
        acc_ref[...] += gate * val

    @pl.when(br == N_BRANCH - 1)
    def _():
        out_ref[...] = acc_ref[...].astype(out_ref.dtype)


def _merge(xb, wg, o_all, w_br, *, tm, tn):
    m, d = xb.shape
    nj = d // tn
    return pl.pallas_call(
        _merge_kernel,
        grid=(m // tm, nj, N_BRANCH),
        in_specs=[pl.BlockSpec((tm, d), lambda i, j, r: (i, 0)),
                  pl.BlockSpec((d, tn), lambda i, j, r: (0, r * nj + j)),
                  pl.BlockSpec((None, tm, BRANCH_W), lambda i, j, r: (r, i, 0)),
                  pl.BlockSpec((None, BRANCH_W, tn), lambda i, j, r: (r, 0, j))],
        out_specs=pl.BlockSpec((tm, tn), lambda i, j, r: (i, j)),
        out_shape=jax.ShapeDtypeStruct((m, d), BF),
        scratch_shapes=[pltpu.VMEM((tm, tn), F32)],
        compiler_params=_cparams(("parallel", "parallel", "arbitrary")),
        name="merge_gates",
    )(xb, wg, o_all, w_br)


def _resln_kernel(*refs, nj, tn, alpha, gated):
    if gated:
        (a_ref, w_ref, res_ref, g_ref, b_ref, bias_ref, p_ref, wp_ref, add_ref,
         of_ref, ob_ref, y_ref, mu_ref, rs_ref) = refs
    else:
        a_ref, w_ref, res_ref, g_ref, b_ref, of_ref, ob_ref, y_ref, mu_ref, rs_ref = refs
    j = pl.program_id(1)

    @pl.when(j < nj)
    def _():
        y = _dot(a_ref[...], w_ref[...].astype(BF))
        if gated:
            y = jax.nn.sigmoid(y + bias_ref[...]) * _dot(p_ref[...], wp_ref[...].astype(BF)) + add_ref[...]
        y_ref[j] = alpha * res_ref[...] + y

    @pl.when(j == nj - 1)
    def _():
        tm = y_ref.shape[1]
        d = nj * tn
        tot = jnp.zeros((tm, 1), F32)
        for c in range(nj):
            tot = tot + jnp.sum(y_ref[c], axis=-1, keepdims=True)
        mu = tot / d
        sq = jnp.zeros((tm, 1), F32)
        for c in range(nj):
            dc = y_ref[c] - mu
            sq = sq + jnp.sum(dc * dc, axis=-1, keepdims=True)
        mu_ref[...] = mu
        rs_ref[...] = lax.rsqrt(sq / d + LN_EPS)

    @pl.when(j >= nj)
    def _():
        o = (y_ref[j - nj] - mu_ref[...]) * rs_ref[...] * g_ref[...] + b_ref[...]
        of_ref[...] = o
        ob_ref[...] = o.astype(BF)


def _resln(a, w, res, g, b, *, alpha, tm, tn, gated=None, name="res_ln"):
    m, kdim = a.shape
    d = w.shape[1]
    nj = d // tn
    mm_col = lambda i, j: (0, jnp.minimum(j, nj - 1))
    mm_blk = lambda i, j: (i, jnp.minimum(j, nj - 1))
    out_col = lambda i, j: (0, jnp.maximum(j - nj, 0))
    out_blk = lambda i, j: (i, jnp.maximum(j - nj, 0))
    in_specs = [pl.BlockSpec((tm, kdim), lambda i, j: (i, 0)),
                pl.BlockSpec((kdim, tn), mm_col),
                pl.BlockSpec((tm, tn), mm_blk),
                pl.BlockSpec((1, tn), out_col),
                pl.BlockSpec((1, tn), out_col)]
    args = [a, w, res, g.reshape(1, d), b.reshape(1, d)]
    if gated is not None:
        bias, p, wp, add = gated
        dp = p.shape[1]
        in_specs += [pl.BlockSpec((1, tn), mm_col),
                     pl.BlockSpec((tm, dp), lambda i, j: (i, 0)),
                     pl.BlockSpec((dp, tn), mm_col),
                     pl.BlockSpec((tm, tn), mm_blk)]
        args += [bias.reshape(1, d), p, wp, add]
    return pl.pallas_call(
        functools.partial(_resln_kernel, nj=nj, tn=tn, alpha=alpha, gated=gated is not None),
        grid=(m // tm, 2 * nj),
        in_specs=in_specs,
        out_specs=[pl.BlockSpec((tm, tn), out_blk), pl.BlockSpec((tm, tn), out_blk)],
        out_shape=[jax.ShapeDtypeStruct((m, d), F32), jax.ShapeDtypeStruct((m, d), BF)],
        scratch_shapes=[pltpu.VMEM((nj, tm, tn), F32), pltpu.VMEM((tm, 1), F32), pltpu.VMEM((tm, 1), F32)],
        compiler_params=_cparams(("parallel", "arbitrary")),
        name=name,
    )(*args)


HALO = 8


def _ffn_up_kernel(x_ref, wg_ref, wv_ref, cwg_ref, cwv_ref, cbg_ref, cbv_ref, a_ref, hg_ref, hv_ref, *, tm, ch, tiles_per_seq):
    i = pl.program_id(1)
    first = (i % tiles_per_seq) == 0

    @pl.when(first)
    def _():
        hg_ref[0:HALO, :] = jnp.zeros((HALO, hg_ref.shape[1]), F32)
        hv_ref[0:HALO, :] = jnp.zeros((HALO, hv_ref.shape[1]), F32)

    @pl.when(jnp.logical_not(first))
    def _():
        hg_ref[0:HALO, :] = hg_ref[tm:tm + HALO, :]
        hv_ref[0:HALO, :] = hv_ref[tm:tm + HALO, :]

    wg = wg_ref[...].astype(BF)
    wv = wv_ref[...].astype(BF)

    def conv(h_ref, c, cw_ref, cb_ref):
        out = cb_ref[...]
        for k in range(CONV_W):
            off = HALO + c * ch - (CONV_W - 1) + k
            out = out + h_ref[off:off + ch, :] * cw_ref[k:k + 1, :]
        return out

    def up(c):
        xc = x_ref[c * ch:(c + 1) * ch, :]
        hg_ref[HALO + c * ch:HALO + (c + 1) * ch, :] = _dot(xc, wg)
        hv_ref[HALO + c * ch:HALO + (c + 1) * ch, :] = _dot(xc, wv)

    up(0)
    for c in range(tm // ch):
        if c + 1 < tm // ch:
            up(c + 1)
        a_ref[c * ch:(c + 1) * ch, :] = (jax.nn.gelu(conv(hg_ref, c, cwg_ref, cbg_ref))
                                         * conv(hv_ref, c, cwv_ref, cbv_ref)).astype(a_ref.dtype)


def _ffn_up(xb, w_up, conv_w, conv_b, *, seq, tm, tn, ch=256):
    m, d = xb.shape
    dff = w_up.shape[1] // 2
    nj = dff // tn
    assert dff % tn == 0 and seq % tm == 0 and tm % ch == 0
    cb = conv_b.reshape(1, 2 * dff)
    return pl.pallas_call(
        functools.partial(_ffn_up_kernel, tm=tm, ch=ch, tiles_per_seq=seq // tm),
        grid=(nj, m // tm),
        in_specs=[pl.BlockSpec((tm, d), lambda j, i: (i, 0)),
                  pl.BlockSpec((d, tn), lambda j, i: (0, j)),
                  pl.BlockSpec((d, tn), lambda j, i: (0, j + nj)),
                  pl.BlockSpec((CONV_W, tn), lambda j, i: (0, j)),
                  pl.BlockSpec((CONV_W, tn), lambda j, i: (0, j + nj)),
                  pl.BlockSpec((1, tn), lambda j, i: (0, j)),
                  pl.BlockSpec((1, tn), lambda j, i: (0, j + nj))],
        out_specs=pl.BlockSpec((tm, tn), lambda j, i: (i, j)),
        out_shape=jax.ShapeDtypeStruct((m, dff), BF),
        scratch_shapes=[pltpu.VMEM((tm + HALO, tn), F32), pltpu.VMEM((tm + HALO, tn), F32)],
        compiler_params=_cparams(("parallel", "arbitrary")),
        name="ffn_up_conv_geglu",
    )(xb, w_up, w_up, conv_w, conv_w, cb, cb)


def _t5_bucket(dist):
    dist = jnp.maximum(dist, 0)
    max_exact = N_BUCKETS // 2
    d = jnp.maximum(dist, 1).astype(F32)
    large = max_exact + (jnp.log(d / max_exact) / math.log(MAX_DISTANCE / max_exact)
                         * (N_BUCKETS - max_exact)).astype(jnp.int32)
    large = jnp.minimum(large, N_BUCKETS - 1)
    return jnp.where(dist < max_exact, dist, large)


def _near_bias(tab, offset, width):
    period = TQ + width
    idx = np.arange(period)
    k = np.where(idx < width, idx, idx - period)
    dist = np.clip(offset - k, 0, MAX_DISTANCE)
    u = tab[_t5_bucket(jnp.asarray(dist, jnp.int32))].T
    flat = jnp.tile(u, (1, TQ))[:, :TQ * (period - 1)]
    return flat.reshape(-1, TQ, period - 1)[:, :, :width]


def _pad_front(a, n):
    return jnp.pad(a, ((0, 0), (n, 0), (0, 0)))


def _layer(x, xb, p_b, w_in, g_cq, g_ckv, g_kidx, b_kidx, w_qidx, w_uq, w_uk, w_uv,
           cmp_pe, cmp_w1, cmp_w2, sinks, w_br, w_o, ln1_g, ln1_b,
           w_up, conv_w, conv_b, w_down, w_pg, b_pg, w_pp, ln2_g, ln2_b, rel_bias, consts, *, bsz, seq, alpha):
    m, d = x.shape
    s_cq, s_ckv, s_kidx, s_widx = 0, Q_LORA, Q_LORA + KV_LORA, Q_LORA + KV_LORA + D_IDX
    s_qb = s_widx + H_IDX
    s_kvb = s_qb + H_B * HEAD_DIM
    s_gb = s_kvb + 6 * G_B * HEAD_DIM
    s_qc = s_gb + 3 * H_B
    s_kc = s_qc + H_C * HEAD_DIM_C
    s_vc = s_kc + KV_C * HEAD_DIM_C
    s_gm = s_vc + KV_C * HEAD_DIM_C
    cols = lambda a, b: w_in[:, a:b]
    w_small = jnp.concatenate(
        [cols(s_qb, s_kvb), cols(s_qc, s_kc), cols(s_cq, s_kidx), cols(s_kidx, s_widx), cols(s_kvb, s_gb),
         cols(s_kc, s_vc), cols(s_vc, s_gm), cols(s_widx, s_qb), cols(s_gb, s_qc),
         jnp.zeros((d, N_SMALL - s_gm), w_in.dtype)], axis=1).astype(BF)
    w_gate = w_in[:, s_gm:].astype(BF)

    z = _mm(xb, w_small, tm=1024, tn=512, name="in_proj")
    cq_n, ckv_n, kidx_n = _prep(z, g_cq, g_ckv, g_kidx, b_kidx)
    w_qq = jnp.concatenate([w_qidx, w_uq], axis=1).astype(BF)
    qq = _mm(cq_n, w_qq, tm=1024, tn=512, out_dtype=BF, name="q_proj")

    z3 = z.reshape(bsz, seq, N_SMALL)
    b3 = lambda a: a.reshape(bsz, seq, a.shape[-1])
    seg = lambda off, w: z3[:, :, off:off + w].astype(BF)

    o_a = _dsa(b3(qq), z3, _pad_front(b3(kidx_n), TQ), _pad_front(b3(ckv_n), TQ),
               jnp.transpose(w_uk, (1, 2, 0)).astype(BF), jnp.transpose(w_uv, (1, 0, 2)).astype(BF),
               consts["near_a"], consts["far_a"], seq=seq)

    gw = G_B * HEAD_DIM
    kcv = _compress(z3, cmp_pe, cmp_w1, cmp_w2, seq=seq)
    o_b = _nsa(z3, kcv,
               _pad_front(seg(OFF_KVB + 2 * gw, gw), TQ), _pad_front(seg(OFF_KVB + 3 * gw, gw), TQ),
               _pad_front(seg(OFF_KVB + 4 * gw, gw), WIN_B), _pad_front(seg(OFF_KVB + 5 * gw, gw), WIN_B),
               consts["et_pad"], consts["overlap"],
               consts["near_s"], consts["near_w"], consts["far_s"], seq=seq)

    kvw = KV_C * HEAD_DIM_C
    o_c = _swa(z3, _pad_front(seg(OFF_KC, kvw), WIN_C), _pad_front(seg(OFF_VC, kvw), WIN_C),
               consts["near_c"], sinks, seq=seq)

    o_all = jnp.stack([o_a.reshape(m, BRANCH_W), o_b.reshape(m, BRANCH_W).astype(BF), o_c.reshape(m, BRANCH_W)])
    merged = _merge(xb, w_gate, o_all, w_br, tm=1024, tn=512)
    x1, x1b = _resln(merged, w_o.astype(BF), x, ln1_g, ln1_b, alpha=alpha, tm=512, tn=512, name="attn_out_ln1")

    a = _ffn_up(x1b, w_up, conv_w, conv_b, seq=seq, tm=min(1024, seq), tn=256)
    dff = w_down.shape[0]
    ffn = _mm(a, w_down.astype(BF), tm=1024, tn=512, tk=dff // 2 if (dff // 2) % LANES == 0 else dff, name="ffn_down")
    return _resln(x1b, w_pg.astype(BF), x1, ln2_g, ln2_b, alpha=alpha, tm=512, tn=512,
                  gated=(b_pg, p_b, w_pp, ffn), name="ple_ln2")


def kernel(x, p, w_in, g_cq, g_ckv, g_kidx, b_kidx, w_qidx, w_uq, w_uk, w_uv, cmp_pe, cmp_w1, cmp_w2, sinks, w_br, w_o, ln1_g, ln1_b, w_up, conv_w, conv_b, w_down, w_pg, b_pg, w_pp, ln2_g, ln2_b, rel_bias):
    bsz, seq, d = x.shape
    depth = p.shape[0]
    alpha = (2 * depth) ** 0.25
    m = bsz * seq
    assert seq % TK == 0 and d % 512 == 0

    key = np.arange(seq)[:, None]
    blk = np.arange(LANES)[None, :]
    et = (key // SEL_BLOCK == blk).astype(np.float32)
    et_pad = jnp.asarray(np.concatenate([np.zeros((TQ, LANES), np.float32), et]), BF)
    ncp = seq // CMP_STRIDE
    cs = np.arange(ncp)[:, None] * CMP_STRIDE
    bs = blk * SEL_BLOCK
    n_cmp = (seq - CMP_BLOCK) // CMP_STRIDE + 1
    ov = ((cs < bs + SEL_BLOCK) & (cs + CMP_BLOCK > bs) & (blk < seq // SEL_BLOCK)
          & (np.arange(ncp)[:, None] < n_cmp)).astype(np.float32)
    tab_a, tab_b, tab_c = rel_bias[:, :H_A], rel_bias[:, H_A:H_A + H_B], rel_bias[:, H_A + H_B:]
    consts = {"et_pad": et_pad, "overlap": jnp.asarray(ov, BF),
              "near_a": _near_bias(tab_a, TQ, 2 * TQ), "far_a": tab_a[N_BUCKETS - 1],
              "near_s": _near_bias(tab_b, TQ, 2 * TQ), "near_w": _near_bias(tab_b, WIN_B, WIN_B + TQ),
              "far_s": tab_b[N_BUCKETS - 1], "near_c": _near_bias(tab_c, WIN_C, 2 * TQ)}

    xf = x.reshape(m, d)
    xb = xf.astype(BF)
    for i in range(depth):
        xf, xb = _layer(xf, xb, p[i].reshape(m, -1).astype(BF), w_in[i], g_cq[i], g_ckv[i], g_kidx[i], b_kidx[i],
                        w_qidx[i], w_uq[i], w_uk[i], w_uv[i], cmp_pe[i], cmp_w1[i], cmp_w2[i], sinks[i],
                        w_br[i], w_o[i], ln1_g[i], ln1_b[i], w_up[i], conv_w[i], conv_b[i], w_down[i],
                        w_pg[i], b_pg[i], w_pp[i], ln2_g[i], ln2_b[i], rel_bias, consts,
                        bsz=bsz, seq=seq, alpha=alpha)
    return xf.reshape(bsz, seq, d)
```

```python
import functools
import math

import jax
import jax.numpy as jnp
import numpy as np
from jax import lax
from jax.experimental import pallas as pl
from jax.experimental.pallas import tpu as pltpu

HEAD_DIM = 128
NEG = -1e30
FORCE_SCORE = 1e4
LN_EPS = 1e-5
N_BUCKETS = 32
MAX_DISTANCE = 128
H_A = 8
Q_LORA = 768
KV_LORA = 256
H_IDX = 32
D_IDX = 128
TOPK_MAX = 256
H_B = 8
G_B = 2
CMP_BLOCK = 32
CMP_STRIDE = 16
CMP_HIDDEN = 128
SEL_BLOCK = 64
N_SEL = 16
WIN_B = 512
H_C = 16
KV_C = 2
HEAD_DIM_C = 64
WIN_C = 128
N_BRANCH = 3
BRANCH_W = 1024
CONV_W = 3

LANES = 128
TQ = 128
TK = 512
VMEM_LIMIT = 56 * 1024 * 1024
INT_MIN = -2 ** 31

OFF_QB = 0
OFF_QC = 1024
OFF_CQ = 2048
OFF_CKV = 2816
OFF_KIDX = 3072
OFF_KVB = 3200
OFF_KC = 4736
OFF_VC = 4864
OFF_WIDX = 4992
N_SMALL = 5120

BF = jnp.bfloat16
F32 = jnp.float32


def _cparams(sem, vmem=VMEM_LIMIT):
    return pltpu.CompilerParams(dimension_semantics=sem, vmem_limit_bytes=vmem)


def _dot(a, b):
    return jnp.dot(a, b, preferred_element_type=F32)


def _layered(w):
    return (w[0], (w[1],)) if isinstance(w, tuple) else (w, ())


def _wspec(lead, block, imap):
    return pl.BlockSpec((None,) * len(lead) + block, lambda *g: lead + imap(*g))


def _dot_t(a, b):
    return lax.dot_general(a, b, (((1,), (1,)), ((), ())), preferred_element_type=F32)


def _mm_kernel(a_ref, b_ref, o_ref, acc_ref, *, nk):
    part = _dot(a_ref[...], b_ref[...].astype(BF))
    if nk == 1:
        o_ref[...] = part.astype(o_ref.dtype)
    else:
        k = pl.program_id(2)

        @pl.when(k == 0)
        def _():
            acc_ref[...] = part

        @pl.when(k > 0)
        def _():
            acc_ref[...] += part

        @pl.when(k == nk - 1)
        def _():
            o_ref[...] = acc_ref[...].astype(o_ref.dtype)


def _mm(a, b, *, tm, tn, tk=None, out_dtype=F32, name="mm"):
    m, kdim = a.shape
    b, lead = _layered(b)
    n = b.shape[-1]
    tk = kdim if tk is None else tk
    nk = kdim // tk
    assert m % tm == 0 and kdim % tk == 0
    acc_shape = (tm, tn) if nk > 1 else (8, LANES)
    return pl.pallas_call(
        functools.partial(_mm_kernel, nk=nk),
        grid=(m // tm, pl.cdiv(n, tn), nk),
        in_specs=[pl.BlockSpec((tm, tk), lambda i, j, k: (i, k)),
                  _wspec(lead, (tk, tn), lambda i, j, k: (k, j))],
        out_specs=pl.BlockSpec((tm, tn), lambda i, j, k: (i, j)),
        out_shape=jax.ShapeDtypeStruct((m, n), out_dtype),
        scratch_shapes=[pltpu.VMEM(acc_shape, F32)],
        compiler_params=_cparams(("parallel", "parallel", "arbitrary")),
        name=name,
    )(a, b)


def _prep_kernel(c_ref, k_ref, gq_ref, gkv_ref, gk_ref, bk_ref, cq_o, ckv_o, kidx_o):
    c = c_ref[...]
    cq = c[:, :Q_LORA]
    ckv = c[:, Q_LORA:]
    cq = cq * lax.rsqrt(jnp.mean(cq * cq, axis=-1, keepdims=True) + LN_EPS) * gq_ref[...]
    ckv = ckv * lax.rsqrt(jnp.mean(ckv * ckv, axis=-1, keepdims=True) + LN_EPS) * gkv_ref[...]
    k = k_ref[...]
    mu = jnp.mean(k, axis=-1, keepdims=True)
    kc = k - mu
    var = jnp.mean(kc * kc, axis=-1, keepdims=True)
    kn = kc * lax.rsqrt(var + LN_EPS) * gk_ref[...] + bk_ref[...]
    cq_o[...] = cq.astype(BF)
    ckv_o[...] = ckv.astype(BF)
    kidx_o[...] = kn.astype(BF)


def _prep(z, g_cq, g_ckv, g_kidx, b_kidx, *, tm=512):
    m = z.shape[0]
    wc = Q_LORA + KV_LORA
    row = lambda a: a.reshape(1, -1)
    full = lambda n: pl.BlockSpec((1, n), lambda i: (0, 0))
    return pl.pallas_call(
        _prep_kernel,
        grid=(m // tm,),
        in_specs=[pl.BlockSpec((tm, wc), lambda i: (i, OFF_CQ // wc)),
                  pl.BlockSpec((tm, D_IDX), lambda i: (i, OFF_KIDX // D_IDX)),
                  full(Q_LORA), full(KV_LORA), full(D_IDX), full(D_IDX)],
        out_specs=[pl.BlockSpec((tm, Q_LORA), lambda i: (i, 0)),
                   pl.BlockSpec((tm, KV_LORA), lambda i: (i, 0)),
                   pl.BlockSpec((tm, D_IDX), lambda i: (i, 0))],
        out_shape=[jax.ShapeDtypeStruct((m, Q_LORA), BF),
                   jax.ShapeDtypeStruct((m, KV_LORA), BF),
                   jax.ShapeDtypeStruct((m, D_IDX), BF)],
        compiler_params=_cparams(("parallel",)),
        name="prep_norms",
    )(z, z, row(g_cq), row(g_ckv), row(g_kidx), row(b_kidx))


def _to_key(x):
    b = lax.bitcast_convert_type(x, jnp.int32)
    return b ^ ((b >> 31) & jnp.int32(0x7FFFFFFF))


def _kth_largest_key(count_ge, k, shape):
    kf = jnp.float32(k)
    zero = jnp.zeros(shape, jnp.int32)
    t0 = jnp.where(count_ge(zero) >= kf, zero, jnp.full(shape, INT_MIN, jnp.int32))

    def body(it, t):
        cand = t + jnp.left_shift(jnp.int32(1), jnp.int32(30) - it)
        return jnp.where(count_ge(cand) >= kf, cand, t)

    t = lax.fori_loop(0, 31, body, t0)
    return jnp.maximum(t, jnp.int32(INT_MIN + 1))


FOLD_ROWS = 64


def _fold_rows(x):
    n, w = x.shape
    return x.reshape(n // FOLD_ROWS, FOLD_ROWS, w).sum(axis=0)


MASKED = 2.0 * NEG


def _online_update_t(m_ref, l_ref, acc_ref, slots, logits, masks, v_ts):
    scaled = []
    for sl, s, mask in zip(slots, logits, masks):
        s = jnp.where(mask, s, MASKED)
        m_old = m_ref[sl]
        m_new = jnp.maximum(m_old, jnp.max(s, axis=0, keepdims=True))
        alpha = jnp.exp(m_old - m_new)
        p = jnp.exp(s - m_new)
        l_ref[sl] = alpha * l_ref[sl] + jnp.sum(p, axis=0, keepdims=True)
        m_ref[sl] = m_new
        scaled.append((alpha, p.astype(BF)))
    for sl, (alpha, p), v_t in zip(slots, scaled, v_ts):
        acc_ref[sl] = alpha * acc_ref[sl] + _dot(v_t, p)


def _dsa_kernel(brow_ref, qi_ref, q_ref, w_ref, kidx_ref, ckv_ref, kvtf_ref, kvtn_ref, wuk_ref, wuvt_ref, neart_ref,
                o_ref, keyf_ref, keyn_ref, qstk_ref, qlstk_ref, m_ref, l_ref, acc_ref, *, topk):
    i = pl.program_id(1)
    tq = TQ
    npair = H_A // 2
    far_end = jnp.maximum(i - 1, 0) * tq
    nfar = (far_end + TK - 1) // TK
    row0 = pl.multiple_of(i * tq, tq)

    for h in range(H_IDX):
        qstk_ref[h * tq:(h + 1) * tq, :] = qi_ref[:, h * D_IDX:(h + 1) * D_IDX]
    w_t = (w_ref[...] * (D_IDX ** -0.5 * H_IDX ** -0.5)).T

    def scores_t(krows):
        acc = jnp.zeros((krows.shape[0], tq), F32)
        for hp in range(H_IDX // 2):
            s = _dot_t(krows, qstk_ref[2 * hp * tq:(2 * hp + 2) * tq, :])
            acc = (acc + jnp.maximum(s[:, :tq], 0.0) * w_t[2 * hp:2 * hp + 1, :]
                   + jnp.maximum(s[:, tq:], 0.0) * w_t[2 * hp + 1:2 * hp + 2, :])
        return acc

    rrow = lax.broadcasted_iota(jnp.int32, (2 * tq, tq), 0)
    ccol = lax.broadcasted_iota(jnp.int32, (2 * tq, tq), 1)
    valid_n = (tq + ccol - rrow >= 0) & ((i - 1) * tq + rrow >= 0)
    keyn_ref[...] = jnp.where(valid_n, _to_key(scores_t(kidx_ref[pl.ds(row0, 2 * tq), :])), jnp.int32(INT_MIN))

    def far_scores(kt, c):
        r0 = pl.multiple_of(tq + kt * TK, tq)
        s = scores_t(kidx_ref[pl.ds(r0, TK), :])
        pos = kt * TK + lax.broadcasted_iota(jnp.int32, (TK, tq), 0)
        keyf_ref[kt] = jnp.where(pos < far_end, _to_key(s), jnp.int32(INT_MIN))
        return c

    lax.fori_loop(0, nfar, far_scores, 0)

    def count_ge(cand):
        part = _fold_rows(jnp.where(keyn_ref[...] >= cand, 1.0, 0.0))

        def body(kt, a):
            return a + _fold_rows(jnp.where(keyf_ref[kt] >= cand, 1.0, 0.0))

        part = lax.fori_loop(0, nfar, body, part)
        return jnp.sum(part, axis=0, keepdims=True)

    thr = _kth_largest_key(count_ge, topk, (1, tq))

    for h in range(H_A):
        ql = _dot(q_ref[:, h * HEAD_DIM:(h + 1) * HEAD_DIM], wuk_ref[h]) * (HEAD_DIM ** -0.5)
        qlstk_ref[h * tq:(h + 1) * tq, :] = ql.astype(BF)

    for hp in range(npair):
        m_ref[hp] = jnp.full((1, 2 * tq), NEG, F32)
        l_ref[hp] = jnp.zeros((1, 2 * tq), F32)
        acc_ref[hp] = jnp.zeros((KV_LORA, 2 * tq), F32)

    pairs = range(npair)

    def update(kv, kv_t, bias, sel):
        sel2 = jnp.concatenate([sel, sel], axis=1)
        logits = [_dot_t(kv, qlstk_ref[2 * hp * tq:(2 * hp + 2) * tq, :]) + bias(hp) for hp in pairs]
        _online_update_t(m_ref, l_ref, acc_ref, pairs, logits, [sel2] * npair, [kv_t] * npair)

    update(ckv_ref[pl.ds(row0, 2 * tq), :], jnp.concatenate([kvtn_ref[i], kvtn_ref[i + 1]], axis=1),
           lambda hp: neart_ref[hp], keyn_ref[...] >= thr)

    def far_attn(kt, c):
        r0 = pl.multiple_of(tq + kt * TK, tq)
        update(ckv_ref[pl.ds(r0, TK), :], kvtf_ref[kt], lambda hp: brow_ref[hp], keyf_ref[kt] >= thr)
        return c

    lax.fori_loop(0, nfar, far_attn, 0)

    for hp in range(npair):
        l = l_ref[hp]
        o_lat_t = (acc_ref[hp] * jnp.where(l > 0.0, 1.0 / l, 0.0)).astype(BF)
        for hh in range(2):
            h = 2 * hp + hh
            o_t = _dot(wuvt_ref[h], o_lat_t[:, hh * tq:(hh + 1) * tq])
            o_ref[:, h * HEAD_DIM:(h + 1) * HEAD_DIM] = o_t.T.astype(o_ref.dtype)


def _dsa(qq, z, kidx_n, ckv_n, w_uk, w_uv, near_a, far_a, *, seq):
    bsz = qq.shape[0]
    topk = min(TOPK_MAX, seq // 4)
    nqi = H_IDX * D_IDX
    npair = H_A // 2
    kidx_pad = _pad_front(kidx_n, TQ)
    ckv_pad = _pad_front(ckv_n, TQ)
    kvt_far = jnp.transpose(ckv_n.reshape(bsz, seq // TK, TK, KV_LORA), (0, 1, 3, 2))
    kvt_near = jnp.transpose(ckv_pad.reshape(bsz, seq // TQ + 1, TQ, KV_LORA), (0, 1, 3, 2))
    wuk_t = jnp.transpose(w_uk, (1, 2, 0)).astype(BF)
    wuv_t = jnp.transpose(w_uv, (1, 2, 0)).astype(BF)
    brow = jnp.repeat(far_a.reshape(npair, 2), TQ, axis=1).reshape(npair, 1, 2 * TQ)
    near_t = jnp.transpose(jnp.transpose(near_a, (0, 2, 1)).reshape(npair, 2, 2 * TQ, TQ),
                           (0, 2, 1, 3)).reshape(npair, 2 * TQ, 2 * TQ)
    whole = lambda shape: pl.BlockSpec(shape, lambda b, i: (0,) * len(shape))
    perb = lambda shape: pl.BlockSpec((None,) + shape, lambda b, i: (b,) + (0,) * len(shape))
    return pl.pallas_call(
        functools.partial(_dsa_kernel, topk=topk),
        grid=(bsz, seq // TQ),
        in_specs=[whole((npair, 1, 2 * TQ)),
                  pl.BlockSpec((None, TQ, nqi), lambda b, i: (b, i, 0)),
                  pl.BlockSpec((None, TQ, H_A * HEAD_DIM), lambda b, i: (b, i, nqi // (H_A * HEAD_DIM))),
                  pl.BlockSpec((None, TQ, LANES), lambda b, i: (b, i, OFF_WIDX // LANES)),
                  perb((seq + TQ, D_IDX)), perb((seq + TQ, KV_LORA)),
                  perb((seq // TK, KV_LORA, TK)), perb((seq // TQ + 1, KV_LORA, TQ)),
                  whole((H_A, HEAD_DIM, KV_LORA)), whole((H_A, HEAD_DIM, KV_LORA)),
                  whole((npair, 2 * TQ, 2 * TQ))],
        out_specs=pl.BlockSpec((None, TQ, H_A * HEAD_DIM), lambda b, i: (b, i, 0)),
        out_shape=jax.ShapeDtypeStruct((bsz, seq, H_A * HEAD_DIM), BF),
        scratch_shapes=[pltpu.VMEM((seq // TK, TK, TQ), jnp.int32),
                        pltpu.VMEM((2 * TQ, TQ), jnp.int32),
                        pltpu.VMEM((H_IDX * TQ, D_IDX), BF),
                        pltpu.VMEM((H_A * TQ, KV_LORA), BF),
                        pltpu.VMEM((npair, 1, 2 * TQ), F32),
                        pltpu.VMEM((npair, 1, 2 * TQ), F32),
                        pltpu.VMEM((npair, KV_LORA, 2 * TQ), F32)],
        compiler_params=_cparams(("parallel", "parallel")),
        name="dsa_mixer",
    )(brow, qq, qq, z, kidx_pad, ckv_pad, kvt_far, kvt_near, wuk_t, wuv_t, near_t)


def _cmp_kernel(x_ref, pe_ref, w1_ref, w2_ref, o_ref, xp_ref, *, seq, ncp):
    xp_ref[0:seq, :] = x_ref[...]
    xp_ref[seq:seq + CMP_STRIDE, :] = jnp.zeros((CMP_STRIDE, HEAD_DIM), F32)
    acc = jnp.zeros((ncp, CMP_HIDDEN), F32)
    for l in range(CMP_BLOCK):
        rows = xp_ref[pl.ds(l, ncp, stride=CMP_STRIDE), :]
        blk = rows + pe_ref[l:l + 1, :]
        acc = acc + _dot(blk.astype(BF), w1_ref[l].astype(BF))
    hdn = jax.nn.gelu(acc)
    o_ref[...] = _dot(hdn.astype(BF), w2_ref[...].astype(BF)).astype(o_ref.dtype)


def _compress(z, cmp_pe, cmp_w1, cmp_w2, *, seq):
    bsz = z.shape[0]
    ncp = seq // CMP_STRIDE
    cmp_pe, lead_pe = _layered(cmp_pe)
    cmp_w1, lead_w1 = _layered(cmp_w1)
    cmp_w2, lead_w2 = _layered(cmp_w2)
    return pl.pallas_call(
        functools.partial(_cmp_kernel, seq=seq, ncp=ncp),
        grid=(bsz, 2, G_B),
        in_specs=[pl.BlockSpec((None, seq, HEAD_DIM), lambda b, c, g: (b, 0, OFF_KVB // HEAD_DIM + c * G_B + g)),
                  _wspec(lead_pe, (None, CMP_BLOCK, HEAD_DIM), lambda b, c, g: (c, 0, 0)),
                  _wspec(lead_w1, (None, CMP_BLOCK, HEAD_DIM, CMP_HIDDEN), lambda b, c, g: (c, 0, 0, 0)),
                  _wspec(lead_w2, (None, CMP_HIDDEN, HEAD_DIM), lambda b, c, g: (c, 0, 0))],
        out_specs=pl.BlockSpec((None, None, None, ncp, HEAD_DIM), lambda b, c, g: (b, c, g, 0, 0)),
        out_shape=jax.ShapeDtypeStruct((bsz, 2, G_B, ncp, HEAD_DIM), BF),
        scratch_shapes=[pltpu.VMEM((seq + CMP_STRIDE, HEAD_DIM), F32)],
        compiler_params=_cparams(("parallel", "parallel", "parallel")),
        name="nsa_compress",
    )(z, cmp_pe, cmp_w1, cmp_w2)


def _nsa_kernel(brow_ref, q_ref, gb_ref, kc_ref, vct_ref, ksl_ref, vstf_ref, vstn_ref, kw_ref, vwt_ref, et_ref, ovt_ref,
                nearts_ref, neartw_ref, o_ref, qstk_ref, imp_ref, mselt_ref, m_ref, l_ref, acc_ref, ot_ref, *, seq, npick):
    i = pl.program_id(1)
    tq = TQ
    hpg = H_B // G_B
    gl = hpg * tq
    ncp = seq // CMP_STRIDE
    nblk = SEL_BLOCK
    far_end = jnp.maximum(i - 1, 0) * tq
    nfar = (far_end + TK - 1) // TK
    row0 = pl.multiple_of(i * tq, tq)
    tile4 = lambda x: jnp.concatenate([x] * hpg, axis=1)
    tlane = i * tq + lax.broadcasted_iota(jnp.int32, (1, tq), 1)

    for h in range(H_B):
        qstk_ref[h * tq:(h + 1) * tq, :] = (q_ref[:, h * HEAD_DIM:(h + 1) * HEAD_DIM] * (HEAD_DIM ** -0.5)).astype(BF)
    sig_t = jax.nn.sigmoid(gb_ref[...]).T

    def gate_row(g, c):
        return jnp.concatenate([sig_t[H_IDX + 3 * (g * hpg + j) + c:H_IDX + 3 * (g * hpg + j) + c + 1, :]
                                for j in range(hpg)], axis=1)

    qgrp = lambda g: qstk_ref[g * gl:(g + 1) * gl, :]
    grows = lambda g: slice(g * HEAD_DIM, (g + 1) * HEAD_DIM)

    nrow = lax.broadcasted_iota(jnp.int32, (ncp, tq), 0)
    cmp_mask = tile4(nrow * CMP_STRIDE + (CMP_BLOCK - 1) <= tlane)
    anyc = tile4(jnp.where(tlane >= CMP_BLOCK - 1, 1.0, 0.0))
    brow = lax.broadcasted_iota(jnp.int32, (nblk, tq), 0)
    cur = tlane // SEL_BLOCK
    forced = (brow == 0) | (brow == cur) | (brow == cur - 1)
    sub = lax.broadcasted_iota(jnp.int32, (8, tq), 0)
    for g in range(G_B):
        s = jnp.where(cmp_mask, _dot_t(kc_ref[g], qgrp(g)), NEG)
        e = jnp.exp(s - jnp.max(s, axis=0, keepdims=True))
        p = e * (anyc / jnp.sum(e, axis=0, keepdims=True))
        ot_ref[g] = gate_row(g, 0) * _dot(vct_ref[g], p.astype(BF))
        psum = p[:, :tq]
        for j in range(1, hpg):
            psum = psum + p[:, j * tq:(j + 1) * tq]
        hi = psum.astype(BF)
        lo = (psum - hi.astype(F32)).astype(BF)
        imp = (_dot(ovt_ref[...], hi) + _dot(ovt_ref[...], lo))[:nblk]
        imp = jnp.where(brow > cur, NEG, jnp.where(forced, FORCE_SCORE, imp))
        imp_ref[...] = imp
        vals = [imp[8 * r:8 * r + 8] for r in range(nblk // 8)]
        rank = [jnp.zeros((8, tq), F32) for _ in vals]
        for mp in range(nblk):
            other = jnp.broadcast_to(imp_ref[mp:mp + 1, :], (8, tq))
            for r, v in enumerate(vals):
                if 8 * r > mp:
                    beats = other >= v
                elif 8 * r + 7 < mp:
                    beats = other > v
                else:
                    beats = (other > v) | ((other == v) & (sub + 8 * r > mp))
                rank[r] = rank[r] + jnp.where(beats, 1.0, 0.0)
        sel = jnp.concatenate([jnp.where(rk < npick, 1.0, 0.0) for rk in rank], axis=0)
        mselt_ref[g] = jnp.concatenate([sel, jnp.zeros((LANES - nblk, tq), F32)], axis=0).astype(BF)

    rrow = lax.broadcasted_iota(jnp.int32, (2 * tq, tq), 0)
    ccol = lax.broadcasted_iota(jnp.int32, (2 * tq, tq), 1)
    causal_n = tq + ccol - rrow >= 0
    etn = et_ref[pl.ds(row0, 2 * tq), :]
    groups = range(G_B)
    for g in groups:
        m_ref[g] = jnp.full((1, gl), NEG, F32)
        l_ref[g] = jnp.zeros((1, gl), F32)
        acc_ref[g] = jnp.zeros((HEAD_DIM, gl), F32)
    _online_update_t(
        m_ref, l_ref, acc_ref, groups,
        [_dot_t(ksl_ref[pl.ds(row0, 2 * tq), grows(g)], qgrp(g)) + nearts_ref[g] for g in groups],
        [tile4(causal_n & (_dot(etn, mselt_ref[g]) > 0.5)) for g in groups],
        [jnp.concatenate([vstn_ref[i, grows(g), :], vstn_ref[i + 1, grows(g), :]], axis=1) for g in groups])

    def far_attn(kt, c):
        r0 = pl.multiple_of(tq + kt * TK, tq)
        ett = et_ref[pl.ds(r0, TK), :]
        infar = kt * TK + lax.broadcasted_iota(jnp.int32, (TK, tq), 0) < far_end
        _online_update_t(
            m_ref, l_ref, acc_ref, groups,
            [_dot_t(ksl_ref[pl.ds(r0, TK), grows(g)], qgrp(g)) + brow_ref[g] for g in groups],
            [tile4(infar & (_dot(ett, mselt_ref[g]) > 0.5)) for g in groups],
            [vstf_ref[kt, grows(g), :] for g in groups])
        return c

    lax.fori_loop(0, nfar, far_attn, 0)
    for g in groups:
        l = l_ref[g]
        ot_ref[g] += gate_row(g, 1) * (acc_ref[g] * jnp.where(l > 0.0, 1.0 / l, 0.0))

    span = WIN_B + tq
    rw = lax.broadcasted_iota(jnp.int32, (span, tq), 0)
    cw = lax.broadcasted_iota(jnp.int32, (span, tq), 1)
    distw = WIN_B + cw - rw
    mask_w = tile4((distw >= 0) & (distw < WIN_B) & (i * tq - WIN_B + rw >= 0))
    for g in groups:
        s = jnp.where(mask_w, _dot_t(kw_ref[pl.ds(row0, span), grows(g)], qgrp(g)) + neartw_ref[g], NEG)
        e = jnp.exp(s - jnp.max(s, axis=0, keepdims=True))
        p = e / jnp.sum(e, axis=0, keepdims=True)
        vw_t = jnp.concatenate([vwt_ref[i + k, grows(g), :] for k in range(span // tq)], axis=1)
        ot_ref[g] += gate_row(g, 2) * _dot(vw_t, p.astype(BF))

    for h in range(H_B):
        g, j = divmod(h, hpg)
        o_ref[:, h * HEAD_DIM:(h + 1) * HEAD_DIM] = ot_ref[g][:, j * tq:(j + 1) * tq].T.astype(o_ref.dtype)


def _tiles_t(a, rows):
    bsz, n, w = a.shape
    return jnp.transpose(a.reshape(bsz, n // rows, rows, w), (0, 1, 3, 2))


def _lanes_by_group(near, groups):
    h, tq, w = near.shape
    t = jnp.transpose(near, (0, 2, 1)).reshape(groups, h // groups, w, tq)
    return jnp.transpose(t, (0, 2, 1, 3)).reshape(groups, w, (h // groups) * tq)


def _nsa(z, kcv, k_slc, v_slc, k_win, v_win, et_pad, overlap_t, near_s, near_w, far_s, *, seq):
    bsz = z.shape[0]
    ncp = seq // CMP_STRIDE
    npick = min(N_SEL, seq // SEL_BLOCK)
    assert seq // SEL_BLOCK <= SEL_BLOCK
    gw = G_B * HEAD_DIM
    hpg = H_B // G_B
    gl = hpg * TQ
    kc = kcv[:, 0]
    vc_t = jnp.transpose(kcv[:, 1], (0, 1, 3, 2))
    brow = jnp.repeat(far_s.reshape(G_B, hpg), TQ, axis=1).reshape(G_B, 1, gl)
    whole = lambda shape: pl.BlockSpec(shape, lambda b, i: (0,) * len(shape))
    perb = lambda shape: pl.BlockSpec((None,) + shape, lambda b, i: (b,) + (0,) * len(shape))
    return pl.pallas_call(
        functools.partial(_nsa_kernel, seq=seq, npick=npick),
        grid=(bsz, seq // TQ),
        in_specs=[whole((G_B, 1, gl)),
                  pl.BlockSpec((None, TQ, H_B * HEAD_DIM), lambda b, i: (b, i, OFF_QB // (H_B * HEAD_DIM))),
                  pl.BlockSpec((None, TQ, LANES), lambda b, i: (b, i, OFF_WIDX // LANES)),
                  perb((G_B, ncp, HEAD_DIM)), perb((G_B, HEAD_DIM, ncp)),
                  perb((seq + TQ, gw)), perb((seq // TK, gw, TK)), perb((seq // TQ + 1, gw, TQ)),
                  perb((seq + WIN_B, gw)), perb(((seq + WIN_B) // TQ, gw, TQ)),
                  whole((seq + TQ, LANES)), whole((LANES, ncp)),
                  whole((G_B, 2 * TQ, gl)), whole((G_B, WIN_B + TQ, gl))],
        out_specs=pl.BlockSpec((None, TQ, H_B * HEAD_DIM), lambda b, i: (b, i, 0)),
        out_shape=jax.ShapeDtypeStruct((bsz, seq, H_B * HEAD_DIM), BF),
        scratch_shapes=[pltpu.VMEM((H_B * TQ, HEAD_DIM), BF),
                        pltpu.VMEM((SEL_BLOCK, TQ), F32),
                        pltpu.VMEM((G_B, LANES, TQ), BF),
                        pltpu.VMEM((G_B, 1, gl), F32),
                        pltpu.VMEM((G_B, 1, gl), F32),
                        pltpu.VMEM((G_B, HEAD_DIM, gl), F32),
                        pltpu.VMEM((G_B, HEAD_DIM, gl), F32)],
        compiler_params=_cparams(("parallel", "parallel")),
        name="nsa_mixer",
    )(brow, z, z, kc, vc_t, _pad_front(k_slc, TQ), _tiles_t(v_slc, TK), _tiles_t(_pad_front(v_slc, TQ), TQ),
      _pad_front(k_win, WIN_B), _tiles_t(_pad_front(v_win, WIN_B), TQ), et_pad, overlap_t,
      _lanes_by_group(near_s, G_B), _lanes_by_group(near_w, G_B))


def _swa_kernel(sink_ref, q_ref, k_ref, v_ref, near_ref, o_ref):
    i = pl.program_id(1)
    tq = TQ
    hpg = H_C // KV_C
    row0 = pl.multiple_of(i * tq, tq)
    qrow = lax.broadcasted_iota(jnp.int32, (tq, 2 * tq), 0)
    jcol = lax.broadcasted_iota(jnp.int32, (tq, 2 * tq), 1)
    dist = WIN_C + qrow - jcol
    mask = (dist >= 0) & (dist < WIN_C) & (i * tq - WIN_C + jcol >= 0)
    kk = k_ref[pl.ds(row0, 2 * tq), :]
    vv = v_ref[pl.ds(row0, 2 * tq), :]
    for h in range(H_C):
        g = h // hpg
        kg = kk[:, g * HEAD_DIM_C:(g + 1) * HEAD_DIM_C]
        vg = vv[:, g * HEAD_DIM_C:(g + 1) * HEAD_DIM_C]
        qh = (q_ref[:, h * HEAD_DIM_C:(h + 1) * HEAD_DIM_C] * (HEAD_DIM_C ** -0.5)).astype(BF)
        s = jnp.where(mask, _dot_t(qh, kg) + near_ref[h], NEG)
        sink = sink_ref[h]
        m = jnp.maximum(jnp.max(s, axis=-1, keepdims=True), sink)
        e = jnp.exp(s - m)
        p = e / (jnp.sum(e, axis=-1, keepdims=True) + jnp.exp(sink - m))
        o_ref[:, h * HEAD_DIM_C:(h + 1) * HEAD_DIM_C] = _dot(p.astype(BF), vg).astype(o_ref.dtype)


def _swa(z, kc_pad, vc_pad, near_c, sinks, *, seq):
    bsz = z.shape[0]
    kvw = KV_C * HEAD_DIM_C
    return pl.pallas_call(
        _swa_kernel,
        grid=(bsz, seq // TQ),
        in_specs=[pl.BlockSpec(memory_space=pltpu.SMEM),
                  pl.BlockSpec((None, TQ, BRANCH_W), lambda b, i: (b, i, OFF_QC // BRANCH_W)),
                  pl.BlockSpec((None, seq + WIN_C, kvw), lambda b, i: (b, 0, 0)),
                  pl.BlockSpec((None, seq + WIN_C, kvw), lambda b, i: (b, 0, 0)),
                  pl.BlockSpec((H_C, TQ, 2 * TQ), lambda b, i: (0, 0, 0))],
        out_specs=pl.BlockSpec((None, TQ, BRANCH_W), lambda b, i: (b, i, 0)),
        out_shape=jax.ShapeDtypeStruct((bsz, seq, BRANCH_W), BF),
        compiler_params=_cparams(("parallel", "parallel")),
        name="swa_mixer",
    )(sinks, z, kc_pad, vc_pad, near_c)


def _merge_kernel(x_ref, wg_ref, o_ref, wbr_ref, out_ref, acc_ref):
    br = pl.program_id(2)
    gate = jax.nn.sigmoid(_dot(x_ref[...], wg_ref[...]))
    val = _dot(o_ref[...], wbr_ref[...].astype(BF))

    @pl.when(br == 0)
    def _():
        acc_ref[...] = gate * val

    @pl.when(br > 0)
    def _():
        acc_ref[...] += gate * val

    @pl.when(br == N_BRANCH - 1)
    def _():
        out_ref[...] = acc_ref[...].astype(out_ref.dtype)


def _merge(xb, wg, o_all, w_br, *, tm, tn):
    m, d = xb.shape
    nj = d // tn
    w_br, lead = _layered(w_br)
    return pl.pallas_call(
        _merge_kernel,
        grid=(m // tm, nj, N_BRANCH),
        in_specs=[pl.BlockSpec((tm, d), lambda i, j, r: (i, 0)),
                  pl.BlockSpec((d, tn), lambda i, j, r: (0, r * nj + j)),
                  pl.BlockSpec((None, tm, BRANCH_W), lambda i, j, r: (r, i, 0)),
                  _wspec(lead, (None, BRANCH_W, tn), lambda i, j, r: (r, 0, j))],
        out_specs=pl.BlockSpec((tm, tn), lambda i, j, r: (i, j)),
        out_shape=jax.ShapeDtypeStruct((m, d), BF),
        scratch_shapes=[pltpu.VMEM((tm, tn), F32)],
        compiler_params=_cparams(("parallel", "parallel", "arbitrary")),
        name="merge_gates",
    )(xb, wg, o_all, w_br)


def _resln_kernel(*refs, nj, tn, alpha, gated):
    if gated:
        (a_ref, w_ref, res_ref, g_ref, b_ref, bias_ref, p_ref, wp_ref, add_ref,
         of_ref, ob_ref, y_ref, mu_ref, rs_ref) = refs
    else:
        a_ref, w_ref, res_ref, g_ref, b_ref, of_ref, ob_ref, y_ref, mu_ref, rs_ref = refs
    j = pl.program_id(1)

    @pl.when(j < nj)
    def _():
        y = _dot(a_ref[...], w_ref[...].astype(BF))
        if gated:
            y = jax.nn.sigmoid(y + bias_ref[...]) * _dot(p_ref[...], wp_ref[...].astype(BF)) + add_ref[...]
        y_ref[j] = alpha * res_ref[...] + y

    @pl.when(j == nj - 1)
    def _():
        tm = y_ref.shape[1]
        d = nj * tn
        tot = jnp.zeros((tm, 1), F32)
        for c in range(nj):
            tot = tot + jnp.sum(y_ref[c], axis=-1, keepdims=True)
        mu = tot / d
        sq = jnp.zeros((tm, 1), F32)
        for c in range(nj):
            dc = y_ref[c] - mu
            sq = sq + jnp.sum(dc * dc, axis=-1, keepdims=True)
        mu_ref[...] = mu
        rs_ref[...] = lax.rsqrt(sq / d + LN_EPS)

    @pl.when(j >= nj)
    def _():
        o = (y_ref[j - nj] - mu_ref[...]) * rs_ref[...] * g_ref[...] + b_ref[...]
        of_ref[...] = o
        ob_ref[...] = o.astype(BF)


def _resln(a, w, res, g, b, *, alpha, tm, tn, gated=None, name="res_ln"):
    m, kdim = a.shape
    w, lead = _layered(w)
    d = w.shape[-1]
    nj = d // tn
    mm_col = lambda i, j: (0, jnp.minimum(j, nj - 1))
    mm_blk = lambda i, j: (i, jnp.minimum(j, nj - 1))
    out_col = lambda i, j: (0, jnp.maximum(j - nj, 0))
    out_blk = lambda i, j: (i, jnp.maximum(j - nj, 0))
    in_specs = [pl.BlockSpec((tm, kdim), lambda i, j: (i, 0)),
                _wspec(lead, (kdim, tn), mm_col),
                pl.BlockSpec((tm, tn), mm_blk),
                pl.BlockSpec((1, tn), out_col),
                pl.BlockSpec((1, tn), out_col)]
    args = [a, w, res, g.reshape(1, d), b.reshape(1, d)]
    if gated is not None:
        bias, p, wp, add = gated
        wp, lead_p = _layered(wp)
        dp = p.shape[1]
        in_specs += [pl.BlockSpec((1, tn), mm_col),
                     pl.BlockSpec((tm, dp), lambda i, j: (i, 0)),
                     _wspec(lead_p, (dp, tn), mm_col),
                     pl.BlockSpec((tm, tn), mm_blk)]
        args += [bias.reshape(1, d), p, wp, add]
    return pl.pallas_call(
        functools.partial(_resln_kernel, nj=nj, tn=tn, alpha=alpha, gated=gated is not None),
        grid=(m // tm, 2 * nj),
        in_specs=in_specs,
        out_specs=[pl.BlockSpec((tm, tn), out_blk), pl.BlockSpec((tm, tn), out_blk)],
        out_shape=[jax.ShapeDtypeStruct((m, d), F32), jax.ShapeDtypeStruct((m, d), BF)],
        scratch_shapes=[pltpu.VMEM((nj, tm, tn), F32), pltpu.VMEM((tm, 1), F32), pltpu.VMEM((tm, 1), F32)],
        compiler_params=_cparams(("parallel", "arbitrary")),
        name=name,
    )(*args)


HALO = 8


def _ffn_up_kernel(x_ref, wg_ref, wv_ref, cwg_ref, cwv_ref, cbg_ref, cbv_ref, a_ref, hg_ref, hv_ref, *, tm, ch, tiles_per_seq):
    i = pl.program_id(1)
    first = (i % tiles_per_seq) == 0

    @pl.when(first)
    def _():
        hg_ref[0:HALO, :] = jnp.zeros((HALO, hg_ref.shape[1]), F32)
        hv_ref[0:HALO, :] = jnp.zeros((HALO, hv_ref.shape[1]), F32)

    @pl.when(jnp.logical_not(first))
    def _():
        hg_ref[0:HALO, :] = hg_ref[tm:tm + HALO, :]
        hv_ref[0:HALO, :] = hv_ref[tm:tm + HALO, :]

    wg = wg_ref[...].astype(BF)
    wv = wv_ref[...].astype(BF)

    def conv(h_ref, c, cw_ref, cb_ref):
        out = cb_ref[...]
        for k in range(CONV_W):
            off = HALO + c * ch - (CONV_W - 1) + k
            out = out + h_ref[off:off + ch, :] * cw_ref[k:k + 1, :]
        return out

    def up(c):
        xc = x_ref[c * ch:(c + 1) * ch, :]
        hg_ref[HALO + c * ch:HALO + (c + 1) * ch, :] = _dot(xc, wg)
        hv_ref[HALO + c * ch:HALO + (c + 1) * ch, :] = _dot(xc, wv)

    up(0)
    for c in range(tm // ch):
        if c + 1 < tm // ch:
            up(c + 1)
        a_ref[c * ch:(c + 1) * ch, :] = (jax.nn.gelu(conv(hg_ref, c, cwg_ref, cbg_ref))
                                         * conv(hv_ref, c, cwv_ref, cbv_ref)).astype(a_ref.dtype)


def _ffn_up(xb, w_up, conv_w, conv_b, *, seq, tm, tn, ch=256):
    m, d = xb.shape
    w_up, lead = _layered(w_up)
    conv_w, lead_c = _layered(conv_w)
    conv_b, lead_b = _layered(conv_b)
    dff = w_up.shape[-1] // 2
    nj = dff // tn
    assert dff % tn == 0 and seq % tm == 0 and tm % ch == 0
    cb = conv_b.reshape(conv_b.shape[:-1] + (1, 2 * dff))
    return pl.pallas_call(
        functools.partial(_ffn_up_kernel, tm=tm, ch=ch, tiles_per_seq=seq // tm),
        grid=(nj, m // tm),
        in_specs=[pl.BlockSpec((tm, d), lambda j, i: (i, 0)),
                  _wspec(lead, (d, tn), lambda j, i: (0, j)),
                  _wspec(lead, (d, tn), lambda j, i: (0, j + nj)),
                  _wspec(lead_c, (CONV_W, tn), lambda j, i: (0, j)),
                  _wspec(lead_c, (CONV_W, tn), lambda j, i: (0, j + nj)),
                  _wspec(lead_b, (1, tn), lambda j, i: (0, j)),
                  _wspec(lead_b, (1, tn), lambda j, i: (0, j + nj))],
        out_specs=pl.BlockSpec((tm, tn), lambda j, i: (i, j)),
        out_shape=jax.ShapeDtypeStruct((m, dff), BF),
        scratch_shapes=[pltpu.VMEM((tm + HALO, tn), F32), pltpu.VMEM((tm + HALO, tn), F32)],
        compiler_params=_cparams(("parallel", "arbitrary")),
        name="ffn_up_conv_geglu",
    )(xb, w_up, w_up, conv_w, conv_w, cb, cb)


def _t5_bucket(dist):
    dist = jnp.maximum(dist, 0)
    max_exact = N_BUCKETS // 2
    d = jnp.maximum(dist, 1).astype(F32)
    large = max_exact + (jnp.log(d / max_exact) / math.log(MAX_DISTANCE / max_exact)
                         * (N_BUCKETS - max_exact)).astype(jnp.int32)
    large = jnp.minimum(large, N_BUCKETS - 1)
    return jnp.where(dist < max_exact, dist, large)


def _near_bias(tab, offset, width):
    period = TQ + width
    idx = np.arange(period)
    k = np.where(idx < width, idx, idx - period)
    dist = np.clip(offset - k, 0, MAX_DISTANCE)
    u = tab[_t5_bucket(jnp.asarray(dist, jnp.int32))].T
    flat = jnp.tile(u, (1, TQ))[:, :TQ * (period - 1)]
    return flat.reshape(-1, TQ, period - 1)[:, :, :width]


def _pad_front(a, n):
    return jnp.pad(a, ((0, 0), (n, 0), (0, 0)))


def _layer(li, x, xb, p_b, w_in_b, g_cq, g_ckv, g_kidx, b_kidx, w_qidx, w_uq, w_uk, w_uv,
           cmp_pe, cmp_w1, cmp_w2, sinks, w_br, w_o_b, ln1_g, ln1_b,
           w_up, conv_w, conv_b, w_down_b, w_pg_b, b_pg, w_pp, ln2_g, ln2_b, consts, *, bsz, seq, alpha):
    m, d = x.shape
    s_cq, s_ckv, s_kidx, s_widx = 0, Q_LORA, Q_LORA + KV_LORA, Q_LORA + KV_LORA + D_IDX
    s_qb = s_widx + H_IDX
    s_kvb = s_qb + H_B * HEAD_DIM
    s_gb = s_kvb + 6 * G_B * HEAD_DIM
    s_qc = s_gb + 3 * H_B
    s_kc = s_qc + H_C * HEAD_DIM_C
    s_vc = s_kc + KV_C * HEAD_DIM_C
    s_gm = s_vc + KV_C * HEAD_DIM_C
    cols = lambda a, b: w_in_b[li, :, a:b]
    w_small = jnp.concatenate(
        [cols(s_qb, s_kvb), cols(s_qc, s_kc), cols(s_cq, s_kidx), cols(s_kidx, s_widx), cols(s_kvb, s_gb),
         cols(s_kc, s_vc), cols(s_vc, s_gm), cols(s_widx, s_qb), cols(s_gb, s_qc),
         jnp.zeros((d, N_SMALL - s_gm), BF)], axis=1)
    w_gate = w_in_b[li, :, s_gm:]

    z = _mm(xb, w_small, tm=1024, tn=512, name="in_proj")
    cq_n, ckv_n, kidx_n = _prep(z, g_cq[li], g_ckv[li], g_kidx[li], b_kidx[li])
    w_qq = jnp.concatenate([w_qidx[li], w_uq[li]], axis=1).astype(BF)
    qq = _mm(cq_n, w_qq, tm=1024, tn=512, out_dtype=BF, name="q_proj")

    z3 = z.reshape(bsz, seq, N_SMALL)
    b3 = lambda a: a.reshape(bsz, seq, a.shape[-1])
    seg = lambda off, w: z3[:, :, off:off + w].astype(BF)

    o_a = _dsa(b3(qq), z3, b3(kidx_n), b3(ckv_n), w_uk[li], w_uv[li], consts["near_a"], consts["far_a"], seq=seq)

    gw = G_B * HEAD_DIM
    kcv = _compress(z3, (cmp_pe, li), (cmp_w1, li), (cmp_w2, li), seq=seq)
    o_b = _nsa(z3, kcv, seg(OFF_KVB + 2 * gw, gw), seg(OFF_KVB + 3 * gw, gw), seg(OFF_KVB + 4 * gw, gw),
               seg(OFF_KVB + 5 * gw, gw), consts["et_pad"], consts["overlap_t"],
               consts["near_s"], consts["near_w"], consts["far_s"], seq=seq)

    kvw = KV_C * HEAD_DIM_C
    o_c = _swa(z3, _pad_front(seg(OFF_KC, kvw), WIN_C), _pad_front(seg(OFF_VC, kvw), WIN_C),
               consts["near_c"], sinks[li], seq=seq)

    o_all = jnp.stack([o_a.reshape(m, BRANCH_W), o_b.reshape(m, BRANCH_W), o_c.reshape(m, BRANCH_W)])
    merged = _merge(xb, w_gate, o_all, (w_br, li), tm=1024, tn=512)
    x1, x1b = _resln(merged, (w_o_b, li), x, ln1_g[li], ln1_b[li], alpha=alpha, tm=512, tn=512, name="attn_out_ln1")

    a = _ffn_up(x1b, (w_up, li), (conv_w, li), (conv_b, li), seq=seq, tm=min(1024, seq), tn=256)
    dff = w_down_b.shape[1]
    ffn = _mm(a, (w_down_b, li), tm=1024, tn=512, tk=dff // 2 if (dff // 2) % LANES == 0 else dff, name="ffn_down")
    return _resln(x1b, (w_pg_b, li), x1, ln2_g[li], ln2_b[li], alpha=alpha, tm=512, tn=512,
                  gated=(b_pg[li], p_b, (w_pp, li), ffn), name="ple_ln2")


def kernel(x, p, w_in, g_cq, g_ckv, g_kidx, b_kidx, w_qidx, w_uq, w_uk, w_uv, cmp_pe, cmp_w1, cmp_w2, sinks, w_br, w_o, ln1_g, ln1_b, w_up, conv_w, conv_b, w_down, w_pg, b_pg, w_pp, ln2_g, ln2_b, rel_bias):
    bsz, seq, d = x.shape
    depth = p.shape[0]
    alpha = (2 * depth) ** 0.25
    m = bsz * seq
    assert seq % TK == 0 and d % 512 == 0

    key = np.arange(seq)[:, None]
    blk = np.arange(LANES)[None, :]
    et = (key // SEL_BLOCK == blk).astype(np.float32)
    et_pad = jnp.asarray(np.concatenate([np.zeros((TQ, LANES), np.float32), et]), BF)
    ncp = seq // CMP_STRIDE
    cs = np.arange(ncp)[:, None] * CMP_STRIDE
    bs = blk * SEL_BLOCK
    n_cmp = (seq - CMP_BLOCK) // CMP_STRIDE + 1
    ov = ((cs < bs + SEL_BLOCK) & (cs + CMP_BLOCK > bs) & (blk < seq // SEL_BLOCK)
          & (np.arange(ncp)[:, None] < n_cmp)).astype(np.float32)
    tab_a, tab_b, tab_c = rel_bias[:, :H_A], rel_bias[:, H_A:H_A + H_B], rel_bias[:, H_A + H_B:]
    consts = {"et_pad": et_pad, "overlap_t": jnp.asarray(ov.T, BF),
              "near_a": _near_bias(tab_a, TQ, 2 * TQ), "far_a": tab_a[N_BUCKETS - 1],
              "near_s": _near_bias(tab_b, TQ, 2 * TQ), "near_w": _near_bias(tab_b, WIN_B, WIN_B + TQ),
              "far_s": tab_b[N_BUCKETS - 1], "near_c": _near_bias(tab_c, WIN_C, 2 * TQ)}

    xf = x.reshape(m, d)
    xb = xf.astype(BF)
    w_in_b, w_o_b, w_pg_b, w_down_b = (w.astype(BF) for w in (w_in, w_o, w_pg, w_down))
    for li in range(depth):
        xf, xb = _layer(li, xf, xb, p[li].reshape(m, -1).astype(BF), w_in_b, g_cq, g_ckv, g_kidx, b_kidx,
                        w_qidx, w_uq, w_uk, w_uv, cmp_pe, cmp_w1, cmp_w2, sinks,
                        w_br, w_o_b, ln1_g, ln1_b, w_up, conv_w, conv_b, w_down_b,
                        w_pg_b, b_pg, w_pp, ln2_g, ln2_b, consts, bsz=bsz, seq=seq, alpha=alpha)
    return xf.reshape(bsz, seq, d)
```

```python
import functools
import math

import jax
import jax.numpy as jnp
import numpy as np
from jax import lax
from jax.experimental import pallas as pl
from jax.experimental.pallas import tpu as pltpu

HEAD_DIM = 128
NEG = -1e30
FORCE_SCORE = 1e4
LN_EPS = 1e-5
N_BUCKETS = 32
MAX_DISTANCE = 128
H_A = 8
Q_LORA = 768
KV_LORA = 256
H_IDX = 32
D_IDX = 128
TOPK_MAX = 256
H_B = 8
G_B = 2
CMP_BLOCK = 32
CMP_STRIDE = 16
CMP_HIDDEN = 128
SEL_BLOCK = 64
N_SEL = 16
WIN_B = 512
H_C = 16
KV_C = 2
HEAD_DIM_C = 64
WIN_C = 128
N_BRANCH = 3
BRANCH_W = 1024
CONV_W = 3

LANES = 128
TQ = 128
TK = 512
VMEM_LIMIT = 56 * 1024 * 1024
INT_MIN = -2 ** 31

OFF_QB = 0
OFF_QC = 1024
OFF_CQ = 2048
OFF_CKV = 2816
OFF_KIDX = 3072
OFF_KVB = 3200
OFF_KC = 4736
OFF_VC = 4864
OFF_WIDX = 4992
N_SMALL = 5120

BF = jnp.bfloat16
F32 = jnp.float32


def _cparams(sem, vmem=VMEM_LIMIT):
    return pltpu.CompilerParams(dimension_semantics=sem, vmem_limit_bytes=vmem)


def _dot(a, b):
    return jnp.dot(a, b, preferred_element_type=F32)


def _layered(w):
    return (w[0], (w[1],)) if isinstance(w, tuple) else (w, ())


def _wspec(lead, block, imap):
    return pl.BlockSpec((None,) * len(lead) + block, lambda *g: lead + imap(*g))


def _dot_t(a, b):
    return lax.dot_general(a, b, (((1,), (1,)), ((), ())), preferred_element_type=F32)


def _mm_kernel(a_ref, b_ref, o_ref, acc_ref, *, nk):
    part = _dot(a_ref[...], b_ref[...].astype(BF))
    if nk == 1:
        o_ref[...] = part.astype(o_ref.dtype)
    else:
        k = pl.program_id(2)

        @pl.when(k == 0)
        def _():
            acc_ref[...] = part

        @pl.when(k > 0)
        def _():
            acc_ref[...] += part

        @pl.when(k == nk - 1)
        def _():
            o_ref[...] = acc_ref[...].astype(o_ref.dtype)


def _mm(a, b, *, tm, tn, tk=None, out_dtype=F32, name="mm"):
    m, kdim = a.shape
    b, lead = _layered(b)
    n = b.shape[-1]
    tk = kdim if tk is None else tk
    nk = kdim // tk
    assert m % tm == 0 and kdim % tk == 0
    acc_shape = (tm, tn) if nk > 1 else (8, LANES)
    return pl.pallas_call(
        functools.partial(_mm_kernel, nk=nk),
        grid=(m // tm, pl.cdiv(n, tn), nk),
        in_specs=[pl.BlockSpec((tm, tk), lambda i, j, k: (i, k)),
                  _wspec(lead, (tk, tn), lambda i, j, k: (k, j))],
        out_specs=pl.BlockSpec((tm, tn), lambda i, j, k: (i, j)),
        out_shape=jax.ShapeDtypeStruct((m, n), out_dtype),
        scratch_shapes=[pltpu.VMEM(acc_shape, F32)],
        compiler_params=_cparams(("parallel", "parallel", "arbitrary")),
        name=name,
    )(a, b)


def _prep_kernel(c_ref, k_ref, gq_ref, gkv_ref, gk_ref, bk_ref, cq_o, ckv_o, kidx_o):
    c = c_ref[...]
    cq = c[:, :Q_LORA]
    ckv = c[:, Q_LORA:]
    cq = cq * lax.rsqrt(jnp.mean(cq * cq, axis=-1, keepdims=True) + LN_EPS) * gq_ref[...]
    ckv = ckv * lax.rsqrt(jnp.mean(ckv * ckv, axis=-1, keepdims=True) + LN_EPS) * gkv_ref[...]
    k = k_ref[...]
    mu = jnp.mean(k, axis=-1, keepdims=True)
    kc = k - mu
    var = jnp.mean(kc * kc, axis=-1, keepdims=True)
    kn = kc * lax.rsqrt(var + LN_EPS) * gk_ref[...] + bk_ref[...]
    cq_o[...] = cq.astype(BF)
    ckv_o[...] = ckv.astype(BF)
    kidx_o[...] = kn.astype(BF)


def _prep(z, g_cq, g_ckv, g_kidx, b_kidx, *, tm=512):
    m = z.shape[0]
    wc = Q_LORA + KV_LORA
    row = lambda a: a.reshape(1, -1)
    full = lambda n: pl.BlockSpec((1, n), lambda i: (0, 0))
    return pl.pallas_call(
        _prep_kernel,
        grid=(m // tm,),
        in_specs=[pl.BlockSpec((tm, wc), lambda i: (i, OFF_CQ // wc)),
                  pl.BlockSpec((tm, D_IDX), lambda i: (i, OFF_KIDX // D_IDX)),
                  full(Q_LORA), full(KV_LORA), full(D_IDX), full(D_IDX)],
        out_specs=[pl.BlockSpec((tm, Q_LORA), lambda i: (i, 0)),
                   pl.BlockSpec((tm, KV_LORA), lambda i: (i, 0)),
                   pl.BlockSpec((tm, D_IDX), lambda i: (i, 0))],
        out_shape=[jax.ShapeDtypeStruct((m, Q_LORA), BF),
                   jax.ShapeDtypeStruct((m, KV_LORA), BF),
                   jax.ShapeDtypeStruct((m, D_IDX), BF)],
        compiler_params=_cparams(("parallel",)),
        name="prep_norms",
    )(z, z, row(g_cq), row(g_ckv), row(g_kidx), row(b_kidx))


def _to_key(x):
    b = lax.bitcast_convert_type(x, jnp.int32)
    return b ^ ((b >> 31) & jnp.int32(0x7FFFFFFF))


def _kth_largest_key(count_ge, k, shape):
    kf = jnp.float32(k)
    zero = jnp.zeros(shape, jnp.int32)
    t0 = jnp.where(count_ge(zero) >= kf, zero, jnp.full(shape, INT_MIN, jnp.int32))

    def body(it, t):
        cand = t + jnp.left_shift(jnp.int32(1), jnp.int32(30) - it)
        return jnp.where(count_ge(cand) >= kf, cand, t)

    t = lax.fori_loop(0, 31, body, t0)
    return jnp.maximum(t, jnp.int32(INT_MIN + 1))


FOLD_ROWS = 64


def _fold_rows(x):
    n, w = x.shape
    return x.reshape(n // FOLD_ROWS, FOLD_ROWS, w).sum(axis=0)


MASKED = 2.0 * NEG


def _online_update_t(m_ref, l_ref, acc_ref, slots, logits, masks, v_ts):
    scaled = []
    for sl, s, mask in zip(slots, logits, masks):
        s = jnp.where(mask, s, MASKED)
        m_old = m_ref[sl]
        m_new = jnp.maximum(m_old, jnp.max(s, axis=0, keepdims=True))
        alpha = jnp.exp(m_old - m_new)
        p = jnp.exp(s - m_new)
        l_ref[sl] = alpha * l_ref[sl] + jnp.sum(p, axis=0, keepdims=True)
        m_ref[sl] = m_new
        scaled.append((alpha, p.astype(BF)))
    for sl, (alpha, p), v_t in zip(slots, scaled, v_ts):
        acc_ref[sl] = alpha * acc_ref[sl] + _dot(v_t, p)


def _dsa_kernel(brow_ref, qi_ref, q_ref, w_ref, kidx_ref, ckv_ref, kvtf_ref, kvtn_ref, wuk_ref, wuvt_ref, neart_ref,
                o_ref, keyf_ref, keyn_ref, qstk_ref, qlstk_ref, m_ref, l_ref, acc_ref, *, topk):
    i = pl.program_id(1)
    tq = TQ
    npair = H_A // 2
    far_end = jnp.maximum(i - 1, 0) * tq
    nfar = (far_end + TK - 1) // TK
    row0 = pl.multiple_of(i * tq, tq)

    for h in range(H_IDX):
        qstk_ref[h * tq:(h + 1) * tq, :] = qi_ref[:, h * D_IDX:(h + 1) * D_IDX]
    w_t = (w_ref[...] * (D_IDX ** -0.5 * H_IDX ** -0.5)).T

    def scores_t(krows):
        acc = jnp.zeros((krows.shape[0], tq), F32)
        for hp in range(H_IDX // 2):
            s = _dot_t(krows, qstk_ref[2 * hp * tq:(2 * hp + 2) * tq, :])
            acc = (acc + jnp.maximum(s[:, :tq], 0.0) * w_t[2 * hp:2 * hp + 1, :]
                   + jnp.maximum(s[:, tq:], 0.0) * w_t[2 * hp + 1:2 * hp + 2, :])
        return acc

    rrow = lax.broadcasted_iota(jnp.int32, (2 * tq, tq), 0)
    ccol = lax.broadcasted_iota(jnp.int32, (2 * tq, tq), 1)
    valid_n = (tq + ccol - rrow >= 0) & ((i - 1) * tq + rrow >= 0)
    keyn_ref[...] = jnp.where(valid_n, _to_key(scores_t(kidx_ref[pl.ds(row0, 2 * tq), :])), jnp.int32(INT_MIN))

    def far_scores(kt, c):
        r0 = pl.multiple_of(tq + kt * TK, tq)
        s = scores_t(kidx_ref[pl.ds(r0, TK), :])
        pos = kt * TK + lax.broadcasted_iota(jnp.int32, (TK, tq), 0)
        keyf_ref[kt] = jnp.where(pos < far_end, _to_key(s), jnp.int32(INT_MIN))
        return c

    lax.fori_loop(0, nfar, far_scores, 0)

    def count_ge(cand):
        part = _fold_rows(jnp.where(keyn_ref[...] >= cand, 1.0, 0.0))

        def body(kt, a):
            return a + _fold_rows(jnp.where(keyf_ref[kt] >= cand, 1.0, 0.0))

        part = lax.fori_loop(0, nfar, body, part)
        return jnp.sum(part, axis=0, keepdims=True)

    thr = _kth_largest_key(count_ge, topk, (1, tq))

    for h in range(H_A):
        ql = _dot(q_ref[:, h * HEAD_DIM:(h + 1) * HEAD_DIM], wuk_ref[h]) * (HEAD_DIM ** -0.5)
        qlstk_ref[h * tq:(h + 1) * tq, :] = ql.astype(BF)

    for hp in range(npair):
        m_ref[hp] = jnp.full((1, 2 * tq), NEG, F32)
        l_ref[hp] = jnp.zeros((1, 2 * tq), F32)
        acc_ref[hp] = jnp.zeros((KV_LORA, 2 * tq), F32)

    pairs = range(npair)

    def update(kv, kv_t, bias, sel):
        sel2 = jnp.concatenate([sel, sel], axis=1)
        logits = [_dot_t(kv, qlstk_ref[2 * hp * tq:(2 * hp + 2) * tq, :]) + bias(hp) for hp in pairs]
        _online_update_t(m_ref, l_ref, acc_ref, pairs, logits, [sel2] * npair, [kv_t] * npair)

    update(ckv_ref[pl.ds(row0, 2 * tq), :], jnp.concatenate([kvtn_ref[i], kvtn_ref[i + 1]], axis=1),
           lambda hp: neart_ref[hp], keyn_ref[...] >= thr)

    def far_attn(kt, c):
        r0 = pl.multiple_of(tq + kt * TK, tq)
        update(ckv_ref[pl.ds(r0, TK), :], kvtf_ref[kt], lambda hp: brow_ref[hp], keyf_ref[kt] >= thr)
        return c

    lax.fori_loop(0, nfar, far_attn, 0)

    for hp in range(npair):
        l = l_ref[hp]
        o_lat_t = (acc_ref[hp] * jnp.where(l > 0.0, 1.0 / l, 0.0)).astype(BF)
        for hh in range(2):
            h = 2 * hp + hh
            o_t = _dot(wuvt_ref[h], o_lat_t[:, hh * tq:(hh + 1) * tq])
            o_ref[:, h * HEAD_DIM:(h + 1) * HEAD_DIM] = o_t.T.astype(o_ref.dtype)


def _dsa(qq, z, kidx_n, ckv_n, w_uk, w_uv, near_a, far_a, *, seq):
    bsz = qq.shape[0]
    topk = min(TOPK_MAX, seq // 4)
    nqi = H_IDX * D_IDX
    npair = H_A // 2
    kidx_pad = _pad_front(kidx_n, TQ)
    ckv_pad = _pad_front(ckv_n, TQ)
    kvt_far = jnp.transpose(ckv_n.reshape(bsz, seq // TK, TK, KV_LORA), (0, 1, 3, 2))
    kvt_near = jnp.transpose(ckv_pad.reshape(bsz, seq // TQ + 1, TQ, KV_LORA), (0, 1, 3, 2))
    wuk_t = jnp.transpose(w_uk, (1, 2, 0)).astype(BF)
    wuv_t = jnp.transpose(w_uv, (1, 2, 0)).astype(BF)
    brow = jnp.repeat(far_a.reshape(npair, 2), TQ, axis=1).reshape(npair, 1, 2 * TQ)
    near_t = jnp.transpose(jnp.transpose(near_a, (0, 2, 1)).reshape(npair, 2, 2 * TQ, TQ),
                           (0, 2, 1, 3)).reshape(npair, 2 * TQ, 2 * TQ)
    whole = lambda shape: pl.BlockSpec(shape, lambda b, i: (0,) * len(shape))
    perb = lambda shape: pl.BlockSpec((None,) + shape, lambda b, i: (b,) + (0,) * len(shape))
    return pl.pallas_call(
        functools.partial(_dsa_kernel, topk=topk),
        grid=(bsz, seq // TQ),
        in_specs=[whole((npair, 1, 2 * TQ)),
                  pl.BlockSpec((None, TQ, nqi), lambda b, i: (b, i, 0)),
                  pl.BlockSpec((None, TQ, H_A * HEAD_DIM), lambda b, i: (b, i, nqi // (H_A * HEAD_DIM))),
                  pl.BlockSpec((None, TQ, LANES), lambda b, i: (b, i, OFF_WIDX // LANES)),
                  perb((seq + TQ, D_IDX)), perb((seq + TQ, KV_LORA)),
                  perb((seq // TK, KV_LORA, TK)), perb((seq // TQ + 1, KV_LORA, TQ)),
                  whole((H_A, HEAD_DIM, KV_LORA)), whole((H_A, HEAD_DIM, KV_LORA)),
                  whole((npair, 2 * TQ, 2 * TQ))],
        out_specs=pl.BlockSpec((None, TQ, H_A * HEAD_DIM), lambda b, i: (b, i, 0)),
        out_shape=jax.ShapeDtypeStruct((bsz, seq, H_A * HEAD_DIM), BF),
        scratch_shapes=[pltpu.VMEM((seq // TK, TK, TQ), jnp.int32),
                        pltpu.VMEM((2 * TQ, TQ), jnp.int32),
                        pltpu.VMEM((H_IDX * TQ, D_IDX), BF),
                        pltpu.VMEM((H_A * TQ, KV_LORA), BF),
                        pltpu.VMEM((npair, 1, 2 * TQ), F32),
                        pltpu.VMEM((npair, 1, 2 * TQ), F32),
                        pltpu.VMEM((npair, KV_LORA, 2 * TQ), F32)],
        compiler_params=_cparams(("parallel", "parallel")),
        name="dsa_mixer",
    )(brow, qq, qq, z, kidx_pad, ckv_pad, kvt_far, kvt_near, wuk_t, wuv_t, near_t)


def _cmp_kernel(x_ref, pe_ref, w1_ref, w2_ref, o_ref, xp_ref, *, seq, ncp):
    xp_ref[0:seq, :] = x_ref[...]
    xp_ref[seq:seq + CMP_STRIDE, :] = jnp.zeros((CMP_STRIDE, HEAD_DIM), F32)
    acc = jnp.zeros((ncp, CMP_HIDDEN), F32)
    for l in range(CMP_BLOCK):
        rows = xp_ref[pl.ds(l, ncp, stride=CMP_STRIDE), :]
        blk = rows + pe_ref[l:l + 1, :]
        acc = acc + _dot(blk.astype(BF), w1_ref[l].astype(BF))
    hdn = jax.nn.gelu(acc)
    o_ref[...] = _dot(hdn.astype(BF), w2_ref[...].astype(BF)).astype(o_ref.dtype)


def _compress(z, cmp_pe, cmp_w1, cmp_w2, *, seq):
    bsz = z.shape[0]
    ncp = seq // CMP_STRIDE
    cmp_pe, lead_pe = _layered(cmp_pe)
    cmp_w1, lead_w1 = _layered(cmp_w1)
    cmp_w2, lead_w2 = _layered(cmp_w2)
    return pl.pallas_call(
        functools.partial(_cmp_kernel, seq=seq, ncp=ncp),
        grid=(bsz, 2, G_B),
        in_specs=[pl.BlockSpec((None, seq, HEAD_DIM), lambda b, c, g: (b, 0, OFF_KVB // HEAD_DIM + c * G_B + g)),
                  _wspec(lead_pe, (None, CMP_BLOCK, HEAD_DIM), lambda b, c, g: (c, 0, 0)),
                  _wspec(lead_w1, (None, CMP_BLOCK, HEAD_DIM, CMP_HIDDEN), lambda b, c, g: (c, 0, 0, 0)),
                  _wspec(lead_w2, (None, CMP_HIDDEN, HEAD_DIM), lambda b, c, g: (c, 0, 0))],
        out_specs=pl.BlockSpec((None, None, None, ncp, HEAD_DIM), lambda b, c, g: (b, c, g, 0, 0)),
        out_shape=jax.ShapeDtypeStruct((bsz, 2, G_B, ncp, HEAD_DIM), BF),
        scratch_shapes=[pltpu.VMEM((seq + CMP_STRIDE, HEAD_DIM), F32)],
        compiler_params=_cparams(("parallel", "parallel", "parallel")),
        name="nsa_compress",
    )(z, cmp_pe, cmp_w1, cmp_w2)


def _nsa_kernel(brow_ref, q_ref, gb_ref, kc_ref, vct_ref, ksl_ref, vstf_ref, vstn_ref, kw_ref, vwt_ref, et_ref, ovt_ref,
                nearts_ref, neartw_ref, o_ref, qstk_ref, imp_ref, mselt_ref, m_ref, l_ref, acc_ref, ot_ref, *, seq, npick):
    i = pl.program_id(1)
    tq = TQ
    hpg = H_B // G_B
    gl = hpg * tq
    ncp = seq // CMP_STRIDE
    nblk = SEL_BLOCK
    far_end = jnp.maximum(i - 1, 0) * tq
    nfar = (far_end + TK - 1) // TK
    row0 = pl.multiple_of(i * tq, tq)
    tile4 = lambda x: jnp.concatenate([x] * hpg, axis=1)
    tlane = i * tq + lax.broadcasted_iota(jnp.int32, (1, tq), 1)

    for h in range(H_B):
        qstk_ref[h * tq:(h + 1) * tq, :] = (q_ref[:, h * HEAD_DIM:(h + 1) * HEAD_DIM] * (HEAD_DIM ** -0.5)).astype(BF)
    sig_t = jax.nn.sigmoid(gb_ref[...]).T

    def gate_row(g, c):
        return jnp.concatenate([sig_t[H_IDX + 3 * (g * hpg + j) + c:H_IDX + 3 * (g * hpg + j) + c + 1, :]
                                for j in range(hpg)], axis=1)

    qgrp = lambda g: qstk_ref[g * gl:(g + 1) * gl, :]
    grows = lambda g: slice(g * HEAD_DIM, (g + 1) * HEAD_DIM)

    nrow = lax.broadcasted_iota(jnp.int32, (ncp, tq), 0)
    cmp_mask = tile4(nrow * CMP_STRIDE + (CMP_BLOCK - 1) <= tlane)
    anyc = tile4(jnp.where(tlane >= CMP_BLOCK - 1, 1.0, 0.0))
    brow = lax.broadcasted_iota(jnp.int32, (nblk, tq), 0)
    cur = tlane // SEL_BLOCK
    forced = (brow == 0) | (brow == cur) | (brow == cur - 1)
    sub = lax.broadcasted_iota(jnp.int32, (8, tq), 0)
    for g in range(G_B):
        s = jnp.where(cmp_mask, _dot_t(kc_ref[g], qgrp(g)), NEG)
        e = jnp.exp(s - jnp.max(s, axis=0, keepdims=True))
        p = e * (anyc / jnp.sum(e, axis=0, keepdims=True))
        ot_ref[g] = gate_row(g, 0) * _dot(vct_ref[g], p.astype(BF))
        psum = p[:, :tq]
        for j in range(1, hpg):
            psum = psum + p[:, j * tq:(j + 1) * tq]
        hi = psum.astype(BF)
        lo = (psum - hi.astype(F32)).astype(BF)
        imp = (_dot(ovt_ref[...], hi) + _dot(ovt_ref[...], lo))[:nblk]
        imp = jnp.where(brow > cur, NEG, jnp.where(forced, FORCE_SCORE, imp))
        imp_ref[...] = imp
        vals = [imp[8 * r:8 * r + 8] for r in range(nblk // 8)]
        rank = [jnp.zeros((8, tq), F32) for _ in vals]
        for mp in range(nblk):
            other = jnp.broadcast_to(imp_ref[mp:mp + 1, :], (8, tq))
            for r, v in enumerate(vals):
                if 8 * r > mp:
                    beats = other >= v
                elif 8 * r + 7 < mp:
                    beats = other > v
                else:
                    beats = (other > v) | ((other == v) & (sub + 8 * r > mp))
                rank[r] = rank[r] + jnp.where(beats, 1.0, 0.0)
        sel = jnp.concatenate([jnp.where(rk < npick, 1.0, 0.0) for rk in rank], axis=0)
        mselt_ref[g] = jnp.concatenate([sel, jnp.zeros((LANES - nblk, tq), F32)], axis=0).astype(BF)

    rrow = lax.broadcasted_iota(jnp.int32, (2 * tq, tq), 0)
    ccol = lax.broadcasted_iota(jnp.int32, (2 * tq, tq), 1)
    causal_n = tq + ccol - rrow >= 0
    etn = et_ref[pl.ds(row0, 2 * tq), :]
    groups = range(G_B)
    for g in groups:
        m_ref[g] = jnp.full((1, gl), NEG, F32)
        l_ref[g] = jnp.zeros((1, gl), F32)
        acc_ref[g] = jnp.zeros((HEAD_DIM, gl), F32)
    _online_update_t(
        m_ref, l_ref, acc_ref, groups,
        [_dot_t(ksl_ref[pl.ds(row0, 2 * tq), grows(g)], qgrp(g)) + nearts_ref[g] for g in groups],
        [tile4(causal_n & (_dot(etn, mselt_ref[g]) > 0.5)) for g in groups],
        [jnp.concatenate([vstn_ref[i, grows(g), :], vstn_ref[i + 1, grows(g), :]], axis=1) for g in groups])

    def far_attn(kt, c):
        r0 = pl.multiple_of(tq + kt * TK, tq)
        ett = et_ref[pl.ds(r0, TK), :]
        infar = kt * TK + lax.broadcasted_iota(jnp.int32, (TK, tq), 0) < far_end
        _online_update_t(
            m_ref, l_ref, acc_ref, groups,
            [_dot_t(ksl_ref[pl.ds(r0, TK), grows(g)], qgrp(g)) + brow_ref[g] for g in groups],
            [tile4(infar & (_dot(ett, mselt_ref[g]) > 0.5)) for g in groups],
            [vstf_ref[kt, grows(g), :] for g in groups])
        return c

    lax.fori_loop(0, nfar, far_attn, 0)
    for g in groups:
        l = l_ref[g]
        ot_ref[g] += gate_row(g, 1) * (acc_ref[g] * jnp.where(l > 0.0, 1.0 / l, 0.0))

    span = WIN_B + tq
    rw = lax.broadcasted_iota(jnp.int32, (span, tq), 0)
    cw = lax.broadcasted_iota(jnp.int32, (span, tq), 1)
    distw = WIN_B + cw - rw
    mask_w = tile4((distw >= 0) & (distw < WIN_B) & (i * tq - WIN_B + rw >= 0))
    for g in groups:
        s = jnp.where(mask_w, _dot_t(kw_ref[pl.ds(row0, span), grows(g)], qgrp(g)) + neartw_ref[g], NEG)
        e = jnp.exp(s - jnp.max(s, axis=0, keepdims=True))
        p = e / jnp.sum(e, axis=0, keepdims=True)
        vw_t = jnp.concatenate([vwt_ref[i + k, grows(g), :] for k in range(span // tq)], axis=1)
        ot_ref[g] += gate_row(g, 2) * _dot(vw_t, p.astype(BF))

    for h in range(H_B):
        g, j = divmod(h, hpg)
        o_ref[:, h * HEAD_DIM:(h + 1) * HEAD_DIM] = ot_ref[g][:, j * tq:(j + 1) * tq].T.astype(o_ref.dtype)


def _tiles_t(a, rows):
    bsz, n, w = a.shape
    return jnp.transpose(a.reshape(bsz, n // rows, rows, w), (0, 1, 3, 2))


def _lanes_by_group(near, groups):
    h, tq, w = near.shape
    t = jnp.transpose(near, (0, 2, 1)).reshape(groups, h // groups, w, tq)
    return jnp.transpose(t, (0, 2, 1, 3)).reshape(groups, w, (h // groups) * tq)


def _nsa(z, kcv, k_slc, v_slc, k_win, v_win, et_pad, overlap_t, near_s, near_w, far_s, *, seq):
    bsz = z.shape[0]
    ncp = seq // CMP_STRIDE
    npick = min(N_SEL, seq // SEL_BLOCK)
    assert seq // SEL_BLOCK <= SEL_BLOCK
    gw = G_B * HEAD_DIM
    hpg = H_B // G_B
    gl = hpg * TQ
    kc = kcv[:, 0]
    vc_t = jnp.transpose(kcv[:, 1], (0, 1, 3, 2))
    brow = jnp.repeat(far_s.reshape(G_B, hpg), TQ, axis=1).reshape(G_B, 1, gl)
    whole = lambda shape: pl.BlockSpec(shape, lambda b, i: (0,) * len(shape))
    perb = lambda shape: pl.BlockSpec((None,) + shape, lambda b, i: (b,) + (0,) * len(shape))
    return pl.pallas_call(
        functools.partial(_nsa_kernel, seq=seq, npick=npick),
        grid=(bsz, seq // TQ),
        in_specs=[whole((G_B, 1, gl)),
                  pl.BlockSpec((None, TQ, H_B * HEAD_DIM), lambda b, i: (b, i, OFF_QB // (H_B * HEAD_DIM))),
                  pl.BlockSpec((None, TQ, LANES), lambda b, i: (b, i, OFF_WIDX // LANES)),
                  perb((G_B, ncp, HEAD_DIM)), perb((G_B, HEAD_DIM, ncp)),
                  perb((seq + TQ, gw)), perb((seq // TK, gw, TK)), perb((seq // TQ + 1, gw, TQ)),
                  perb((seq + WIN_B, gw)), perb(((seq + WIN_B) // TQ, gw, TQ)),
                  whole((seq + TQ, LANES)), whole((LANES, ncp)),
                  whole((G_B, 2 * TQ, gl)), whole((G_B, WIN_B + TQ, gl))],
        out_specs=pl.BlockSpec((None, TQ, H_B * HEAD_DIM), lambda b, i: (b, i, 0)),
        out_shape=jax.ShapeDtypeStruct((bsz, seq, H_B * HEAD_DIM), BF),
        scratch_shapes=[pltpu.VMEM((H_B * TQ, HEAD_DIM), BF),
                        pltpu.VMEM((SEL_BLOCK, TQ), F32),
                        pltpu.VMEM((G_B, LANES, TQ), BF),
                        pltpu.VMEM((G_B, 1, gl), F32),
                        pltpu.VMEM((G_B, 1, gl), F32),
                        pltpu.VMEM((G_B, HEAD_DIM, gl), F32),
                        pltpu.VMEM((G_B, HEAD_DIM, gl), F32)],
        compiler_params=_cparams(("parallel", "parallel")),
        name="nsa_mixer",
    )(brow, z, z, kc, vc_t, _pad_front(k_slc, TQ), _tiles_t(v_slc, TK), _tiles_t(_pad_front(v_slc, TQ), TQ),
      _pad_front(k_win, WIN_B), _tiles_t(_pad_front(v_win, WIN_B), TQ), et_pad, overlap_t,
      _lanes_by_group(near_s, G_B), _lanes_by_group(near_w, G_B))


def _swa_kernel(sink_ref, q_ref, k_ref, v_ref, near_ref, o_ref):
    i = pl.program_id(1)
    tq = TQ
    hpg = H_C // KV_C
    row0 = pl.multiple_of(i * tq, tq)
    qrow = lax.broadcasted_iota(jnp.int32, (tq, 2 * tq), 0)
    jcol = lax.broadcasted_iota(jnp.int32, (tq, 2 * tq), 1)
    dist = WIN_C + qrow - jcol
    mask = (dist >= 0) & (dist < WIN_C) & (i * tq - WIN_C + jcol >= 0)
    kk = k_ref[pl.ds(row0, 2 * tq), :]
    vv = v_ref[pl.ds(row0, 2 * tq), :]
    for h in range(H_C):
        g = h // hpg
        kg = kk[:, g * HEAD_DIM_C:(g + 1) * HEAD_DIM_C]
        vg = vv[:, g * HEAD_DIM_C:(g + 1) * HEAD_DIM_C]
        qh = (q_ref[:, h * HEAD_DIM_C:(h + 1) * HEAD_DIM_C] * (HEAD_DIM_C ** -0.5)).astype(BF)
        s = jnp.where(mask, _dot_t(qh, kg) + near_ref[h], NEG)
        sink = sink_ref[h]
        m = jnp.maximum(jnp.max(s, axis=-1, keepdims=True), sink)
        e = jnp.exp(s - m)
        p = e / (jnp.sum(e, axis=-1, keepdims=True) + jnp.exp(sink - m))
        o_ref[:, h * HEAD_DIM_C:(h + 1) * HEAD_DIM_C] = _dot(p.astype(BF), vg).astype(o_ref.dtype)


def _swa(z, kc_pad, vc_pad, near_c, sinks, *, seq):
    bsz = z.shape[0]
    kvw = KV_C * HEAD_DIM_C
    return pl.pallas_call(
        _swa_kernel,
        grid=(bsz, seq // TQ),
        in_specs=[pl.BlockSpec(memory_space=pltpu.SMEM),
                  pl.BlockSpec((None, TQ, BRANCH_W), lambda b, i: (b, i, OFF_QC // BRANCH_W)),
                  pl.BlockSpec((None, seq + WIN_C, kvw), lambda b, i: (b, 0, 0)),
                  pl.BlockSpec((None, seq + WIN_C, kvw), lambda b, i: (b, 0, 0)),
                  pl.BlockSpec((H_C, TQ, 2 * TQ), lambda b, i: (0, 0, 0))],
        out_specs=pl.BlockSpec((None, TQ, BRANCH_W), lambda b, i: (b, i, 0)),
        out_shape=jax.ShapeDtypeStruct((bsz, seq, BRANCH_W), BF),
        compiler_params=_cparams(("parallel", "parallel")),
        name="swa_mixer",
    )(sinks, z, kc_pad, vc_pad, near_c)


def _merge_kernel(x_ref, wg_ref, o_ref, wbr_ref, out_ref, acc_ref):
    br = pl.program_id(2)
    gate = jax.nn.sigmoid(_dot(x_ref[...], wg_ref[...]))
    val = _dot(o_ref[...], wbr_ref[...].astype(BF))

    @pl.when(br == 0)
    def _():
        acc_ref[...] = gate * val

    @pl.when(br > 0)
    def _():
        acc_ref[...] += gate * val

    @pl.when(br == N_BRANCH - 1)
    def _():
        out_ref[...] = acc_ref[...].astype(out_ref.dtype)


def _merge(xb, wg, o_all, w_br, *, tm, tn):
    m, d = xb.shape
    nj = d // tn
    w_br, lead = _layered(w_br)
    wg, lead_g = _layered(wg)
    return pl.pallas_call(
        _merge_kernel,
        grid=(m // tm, nj, N_BRANCH),
        in_specs=[pl.BlockSpec((tm, d), lambda i, j, r: (i, 0)),
                  _wspec(lead_g, (d, tn), lambda i, j, r: (0, r * nj + j)),
                  pl.BlockSpec((None, tm, BRANCH_W), lambda i, j, r: (r, i, 0)),
                  _wspec(lead, (None, BRANCH_W, tn), lambda i, j, r: (r, 0, j))],
        out_specs=pl.BlockSpec((tm, tn), lambda i, j, r: (i, j)),
        out_shape=jax.ShapeDtypeStruct((m, d), BF),
        scratch_shapes=[pltpu.VMEM((tm, tn), F32)],
        compiler_params=_cparams(("parallel", "parallel", "arbitrary")),
        name="merge_gates",
    )(xb, wg, o_all, w_br)


def _resln_kernel(*refs, nj, tn, alpha, gated):
    if gated:
        (a_ref, w_ref, res_ref, g_ref, b_ref, bias_ref, p_ref, wp_ref, add_ref,
         of_ref, ob_ref, y_ref, mu_ref, rs_ref) = refs
    else:
        a_ref, w_ref, res_ref, g_ref, b_ref, of_ref, ob_ref, y_ref, mu_ref, rs_ref = refs
    j = pl.program_id(1)

    @pl.when(j < nj)
    def _():
        y = _dot(a_ref[...], w_ref[...].astype(BF))
        if gated:
            y = jax.nn.sigmoid(y + bias_ref[...]) * _dot(p_ref[...], wp_ref[...].astype(BF)) + add_ref[...]
        y_ref[j] = alpha * res_ref[...] + y

    @pl.when(j == nj - 1)
    def _():
        tm = y_ref.shape[1]
        d = nj * tn
        tot = jnp.zeros((tm, 1), F32)
        for c in range(nj):
            tot = tot + jnp.sum(y_ref[c], axis=-1, keepdims=True)
        mu = tot / d
        sq = jnp.zeros((tm, 1), F32)
        for c in range(nj):
            dc = y_ref[c] - mu
            sq = sq + jnp.sum(dc * dc, axis=-1, keepdims=True)
        mu_ref[...] = mu
        rs_ref[...] = lax.rsqrt(sq / d + LN_EPS)

    @pl.when(j >= nj)
    def _():
        o = (y_ref[j - nj] - mu_ref[...]) * rs_ref[...] * g_ref[...] + b_ref[...]
        of_ref[...] = o
        ob_ref[...] = o.astype(BF)


def _resln(a, w, res, g, b, *, alpha, tm, tn, gated=None, name="res_ln"):
    m, kdim = a.shape
    w, lead = _layered(w)
    d = w.shape[-1]
    nj = d // tn
    mm_col = lambda i, j: (0, jnp.minimum(j, nj - 1))
    mm_blk = lambda i, j: (i, jnp.minimum(j, nj - 1))
    out_col = lambda i, j: (0, jnp.maximum(j - nj, 0))
    out_blk = lambda i, j: (i, jnp.maximum(j - nj, 0))
    in_specs = [pl.BlockSpec((tm, kdim), lambda i, j: (i, 0)),
                _wspec(lead, (kdim, tn), mm_col),
                pl.BlockSpec((tm, tn), mm_blk),
                pl.BlockSpec((1, tn), out_col),
                pl.BlockSpec((1, tn), out_col)]
    args = [a, w, res, g.reshape(1, d), b.reshape(1, d)]
    if gated is not None:
        bias, p, wp, add = gated
        wp, lead_p = _layered(wp)
        dp = p.shape[1]
        in_specs += [pl.BlockSpec((1, tn), mm_col),
                     pl.BlockSpec((tm, dp), lambda i, j: (i, 0)),
                     _wspec(lead_p, (dp, tn), mm_col),
                     pl.BlockSpec((tm, tn), mm_blk)]
        args += [bias.reshape(1, d), p, wp, add]
    return pl.pallas_call(
        functools.partial(_resln_kernel, nj=nj, tn=tn, alpha=alpha, gated=gated is not None),
        grid=(m // tm, 2 * nj),
        in_specs=in_specs,
        out_specs=[pl.BlockSpec((tm, tn), out_blk), pl.BlockSpec((tm, tn), out_blk)],
        out_shape=[jax.ShapeDtypeStruct((m, d), F32), jax.ShapeDtypeStruct((m, d), BF)],
        scratch_shapes=[pltpu.VMEM((nj, tm, tn), F32), pltpu.VMEM((tm, 1), F32), pltpu.VMEM((tm, 1), F32)],
        compiler_params=_cparams(("parallel", "arbitrary")),
        name=name,
    )(*args)


HALO = 8


def _ffn_up_kernel(x_ref, wg_ref, wv_ref, cwg_ref, cwv_ref, cbg_ref, cbv_ref, a_ref, hg_ref, hv_ref, *, tm, ch, tiles_per_seq):
    i = pl.program_id(1)
    first = (i % tiles_per_seq) == 0

    @pl.when(first)
    def _():
        hg_ref[0:HALO, :] = jnp.zeros((HALO, hg_ref.shape[1]), F32)
        hv_ref[0:HALO, :] = jnp.zeros((HALO, hv_ref.shape[1]), F32)

    @pl.when(jnp.logical_not(first))
    def _():
        hg_ref[0:HALO, :] = hg_ref[tm:tm + HALO, :]
        hv_ref[0:HALO, :] = hv_ref[tm:tm + HALO, :]

    wg = wg_ref[...].astype(BF)
    wv = wv_ref[...].astype(BF)

    def conv(h_ref, c, cw_ref, cb_ref):
        out = cb_ref[...]
        for k in range(CONV_W):
            off = HALO + c * ch - (CONV_W - 1) + k
            out = out + h_ref[off:off + ch, :] * cw_ref[k:k + 1, :]
        return out

    def up(c):
        xc = x_ref[c * ch:(c + 1) * ch, :]
        hg_ref[HALO + c * ch:HALO + (c + 1) * ch, :] = _dot(xc, wg)
        hv_ref[HALO + c * ch:HALO + (c + 1) * ch, :] = _dot(xc, wv)

    up(0)
    for c in range(tm // ch):
        if c + 1 < tm // ch:
            up(c + 1)
        a_ref[c * ch:(c + 1) * ch, :] = (jax.nn.gelu(conv(hg_ref, c, cwg_ref, cbg_ref))
                                         * conv(hv_ref, c, cwv_ref, cbv_ref)).astype(a_ref.dtype)


def _ffn_up(xb, w_up, conv_w, conv_b, *, seq, tm, tn, ch=256):
    m, d = xb.shape
    w_up, lead = _layered(w_up)
    conv_w, lead_c = _layered(conv_w)
    conv_b, lead_b = _layered(conv_b)
    dff = w_up.shape[-1] // 2
    nj = dff // tn
    assert dff % tn == 0 and seq % tm == 0 and tm % ch == 0
    cb = conv_b.reshape(conv_b.shape[:-1] + (1, 2 * dff))
    return pl.pallas_call(
        functools.partial(_ffn_up_kernel, tm=tm, ch=ch, tiles_per_seq=seq // tm),
        grid=(nj, m // tm),
        in_specs=[pl.BlockSpec((tm, d), lambda j, i: (i, 0)),
                  _wspec(lead, (d, tn), lambda j, i: (0, j)),
                  _wspec(lead, (d, tn), lambda j, i: (0, j + nj)),
                  _wspec(lead_c, (CONV_W, tn), lambda j, i: (0, j)),
                  _wspec(lead_c, (CONV_W, tn), lambda j, i: (0, j + nj)),
                  _wspec(lead_b, (1, tn), lambda j, i: (0, j)),
                  _wspec(lead_b, (1, tn), lambda j, i: (0, j + nj))],
        out_specs=pl.BlockSpec((tm, tn), lambda j, i: (i, j)),
        out_shape=jax.ShapeDtypeStruct((m, dff), BF),
        scratch_shapes=[pltpu.VMEM((tm + HALO, tn), F32), pltpu.VMEM((tm + HALO, tn), F32)],
        compiler_params=_cparams(("parallel", "arbitrary")),
        name="ffn_up_conv_geglu",
    )(xb, w_up, w_up, conv_w, conv_w, cb, cb)


def _t5_bucket(dist):
    dist = jnp.maximum(dist, 0)
    max_exact = N_BUCKETS // 2
    d = jnp.maximum(dist, 1).astype(F32)
    large = max_exact + (jnp.log(d / max_exact) / math.log(MAX_DISTANCE / max_exact)
                         * (N_BUCKETS - max_exact)).astype(jnp.int32)
    large = jnp.minimum(large, N_BUCKETS - 1)
    return jnp.where(dist < max_exact, dist, large)


def _near_bias(tab, offset, width):
    period = TQ + width
    idx = np.arange(period)
    k = np.where(idx < width, idx, idx - period)
    dist = np.clip(offset - k, 0, MAX_DISTANCE)
    u = tab[_t5_bucket(jnp.asarray(dist, jnp.int32))].T
    flat = jnp.tile(u, (1, TQ))[:, :TQ * (period - 1)]
    return flat.reshape(-1, TQ, period - 1)[:, :, :width]


def _pad_front(a, n):
    return jnp.pad(a, ((0, 0), (n, 0), (0, 0)))


def _in_splits():
    s_kidx = Q_LORA + KV_LORA
    s_widx = s_kidx + D_IDX
    s_qb = s_widx + H_IDX
    s_kvb = s_qb + H_B * HEAD_DIM
    s_gb = s_kvb + 6 * G_B * HEAD_DIM
    s_qc = s_gb + 3 * H_B
    s_kc = s_qc + H_C * HEAD_DIM_C
    s_vc = s_kc + KV_C * HEAD_DIM_C
    s_gm = s_vc + KV_C * HEAD_DIM_C
    segs = [(s_qb, s_kvb), (s_qc, s_kc), (0, s_kidx), (s_kidx, s_widx), (s_kvb, s_gb),
            (s_kc, s_vc), (s_vc, s_gm), (s_widx, s_qb), (s_gb, s_qc)]
    return segs, s_gm


def _win_kernel(w_ref, small_ref, gate_ref, *, segs, s_gm):
    parts = [w_ref[:, a:b] for a, b in segs]
    used = sum(b - a for a, b in segs)
    parts.append(jnp.zeros((w_ref.shape[0], N_SMALL - used), F32))
    small_ref[...] = jnp.concatenate(parts, axis=1).astype(BF)
    gate_ref[...] = w_ref[:, s_gm:].astype(BF)


def _prep_w_in(w_in, *, tr=128):
    depth, d, n_in = w_in.shape
    segs, s_gm = _in_splits()
    return pl.pallas_call(
        functools.partial(_win_kernel, segs=segs, s_gm=s_gm),
        grid=(depth, d // tr),
        in_specs=[pl.BlockSpec((None, tr, n_in), lambda l, i: (l, i, 0))],
        out_specs=[pl.BlockSpec((None, tr, N_SMALL), lambda l, i: (l, i, 0)),
                   pl.BlockSpec((None, tr, n_in - s_gm), lambda l, i: (l, i, 0))],
        out_shape=[jax.ShapeDtypeStruct((depth, d, N_SMALL), BF),
                   jax.ShapeDtypeStruct((depth, d, n_in - s_gm), BF)],
        compiler_params=_cparams(("parallel", "parallel")),
        name="prep_w_in",
    )(w_in)


def _layer(li, x, xb, p_b, w_small_b, w_gate_b, g_cq, g_ckv, g_kidx, b_kidx, w_qidx, w_uq, w_uk, w_uv,
           cmp_pe, cmp_w1, cmp_w2, sinks, w_br, w_o_b, ln1_g, ln1_b,
           w_up, conv_w, conv_b, w_down_b, w_pg_b, b_pg, w_pp, ln2_g, ln2_b, consts, *, bsz, seq, alpha):
    m, d = x.shape
    z = _mm(xb, (w_small_b, li), tm=1024, tn=512, name="in_proj")
    cq_n, ckv_n, kidx_n = _prep(z, g_cq[li], g_ckv[li], g_kidx[li], b_kidx[li])
    w_qq = jnp.concatenate([w_qidx[li], w_uq[li]], axis=1).astype(BF)
    qq = _mm(cq_n, w_qq, tm=1024, tn=512, out_dtype=BF, name="q_proj")

    z3 = z.reshape(bsz, seq, N_SMALL)
    b3 = lambda a: a.reshape(bsz, seq, a.shape[-1])
    seg = lambda off, w: z3[:, :, off:off + w].astype(BF)

    o_a = _dsa(b3(qq), z3, b3(kidx_n), b3(ckv_n), w_uk[li], w_uv[li], consts["near_a"], consts["far_a"], seq=seq)

    gw = G_B * HEAD_DIM
    kcv = _compress(z3, (cmp_pe, li), (cmp_w1, li), (cmp_w2, li), seq=seq)
    o_b = _nsa(z3, kcv, seg(OFF_KVB + 2 * gw, gw), seg(OFF_KVB + 3 * gw, gw), seg(OFF_KVB + 4 * gw, gw),
               seg(OFF_KVB + 5 * gw, gw), consts["et_pad"], consts["overlap_t"],
               consts["near_s"], consts["near_w"], consts["far_s"], seq=seq)

    kvw = KV_C * HEAD_DIM_C
    o_c = _swa(z3, _pad_front(seg(OFF_KC, kvw), WIN_C), _pad_front(seg(OFF_VC, kvw), WIN_C),
               consts["near_c"], sinks[li], seq=seq)

    o_all = jnp.stack([o_a.reshape(m, BRANCH_W), o_b.reshape(m, BRANCH_W), o_c.reshape(m, BRANCH_W)])
    merged = _merge(xb, (w_gate_b, li), o_all, (w_br, li), tm=1024, tn=512)
    x1, x1b = _resln(merged, (w_o_b, li), x, ln1_g[li], ln1_b[li], alpha=alpha, tm=1024, tn=256, name="attn_out_ln1")

    a = _ffn_up(x1b, (w_up, li), (conv_w, li), (conv_b, li), seq=seq, tm=min(1024, seq), tn=256)
    dff = w_down_b.shape[1]
    ffn = _mm(a, (w_down_b, li), tm=1024, tn=512, tk=dff // 2 if (dff // 2) % LANES == 0 else dff, name="ffn_down")
    return _resln(x1b, (w_pg_b, li), x1, ln2_g[li], ln2_b[li], alpha=alpha, tm=1024, tn=256,
                  gated=(b_pg[li], p_b, (w_pp, li), ffn), name="ple_ln2")


def kernel(x, p, w_in, g_cq, g_ckv, g_kidx, b_kidx, w_qidx, w_uq, w_uk, w_uv, cmp_pe, cmp_w1, cmp_w2, sinks, w_br, w_o, ln1_g, ln1_b, w_up, conv_w, conv_b, w_down, w_pg, b_pg, w_pp, ln2_g, ln2_b, rel_bias):
    bsz, seq, d = x.shape
    depth = p.shape[0]
    alpha = (2 * depth) ** 0.25
    m = bsz * seq
    assert seq % TK == 0 and d % 512 == 0

    key = np.arange(seq)[:, None]
    blk = np.arange(LANES)[None, :]
    et = (key // SEL_BLOCK == blk).astype(np.float32)
    et_pad = jnp.asarray(np.concatenate([np.zeros((TQ, LANES), np.float32), et]), BF)
    ncp = seq // CMP_STRIDE
    cs = np.arange(ncp)[:, None] * CMP_STRIDE
    bs = blk * SEL_BLOCK
    n_cmp = (seq - CMP_BLOCK) // CMP_STRIDE + 1
    ov = ((cs < bs + SEL_BLOCK) & (cs + CMP_BLOCK > bs) & (blk < seq // SEL_BLOCK)
          & (np.arange(ncp)[:, None] < n_cmp)).astype(np.float32)
    tab_a, tab_b, tab_c = rel_bias[:, :H_A], rel_bias[:, H_A:H_A + H_B], rel_bias[:, H_A + H_B:]
    consts = {"et_pad": et_pad, "overlap_t": jnp.asarray(ov.T, BF),
              "near_a": _near_bias(tab_a, TQ, 2 * TQ), "far_a": tab_a[N_BUCKETS - 1],
              "near_s": _near_bias(tab_b, TQ, 2 * TQ), "near_w": _near_bias(tab_b, WIN_B, WIN_B + TQ),
              "far_s": tab_b[N_BUCKETS - 1], "near_c": _near_bias(tab_c, WIN_C, 2 * TQ)}

    xf = x.reshape(m, d)
    xb = xf.astype(BF)
    w_o_b, w_pg_b, w_down_b = (w.astype(BF) for w in (w_o, w_pg, w_down))
    w_small_b, w_gate_b = _prep_w_in(w_in)
    for li in range(depth):
        xf, xb = _layer(li, xf, xb, p[li].reshape(m, -1).astype(BF), w_small_b, w_gate_b, g_cq, g_ckv, g_kidx, b_kidx,
                        w_qidx, w_uq, w_uk, w_uv, cmp_pe, cmp_w1, cmp_w2, sinks,
                        w_br, w_o_b, ln1_g, ln1_b, w_up, conv_w, conv_b, w_down_b,
                        w_pg_b, b_pg, w_pp, ln2_g, ln2_b, consts, bsz=bsz, seq=seq, alpha=alpha)
    return xf.reshape(bsz, seq, d)
```

```python
import functools
import math

import jax
import jax.numpy as jnp
import numpy as np
from jax import lax
from jax.experimental import pallas as pl
from jax.experimental.pallas import tpu as pltpu

HEAD_DIM = 128
NEG = -1e30
FORCE_SCORE = 1e4
LN_EPS = 1e-5
N_BUCKETS = 32
MAX_DISTANCE = 128
H_A = 8
Q_LORA = 768
KV_LORA = 256
H_IDX = 32
D_IDX = 128
TOPK_MAX = 256
H_B = 8
G_B = 2
CMP_BLOCK = 32
CMP_STRIDE = 16
CMP_HIDDEN = 128
SEL_BLOCK = 64
N_SEL = 16
WIN_B = 512
H_C = 16
KV_C = 2
HEAD_DIM_C = 64
WIN_C = 128
N_BRANCH = 3
BRANCH_W = 1024
CONV_W = 3

LANES = 128
TQ = 128
TK = 512
VMEM_LIMIT = 56 * 1024 * 1024
INT_MIN = -2 ** 31

OFF_QB = 0
OFF_QC = 1024
OFF_CQ = 2048
OFF_CKV = 2816
OFF_KIDX = 3072
OFF_KVB = 3200
OFF_KC = 4736
OFF_VC = 4864
OFF_WIDX = 4992
N_SMALL = 5120

BF = jnp.bfloat16
F32 = jnp.float32


def _cparams(sem, vmem=VMEM_LIMIT):
    return pltpu.CompilerParams(dimension_semantics=sem, vmem_limit_bytes=vmem)


def _dot(a, b):
    return jnp.dot(a, b, preferred_element_type=F32)


def _layered(w):
    return (w[0], (w[1],)) if isinstance(w, tuple) else (w, ())


def _wspec(lead, block, imap):
    return pl.BlockSpec((None,) * len(lead) + block, lambda *g: lead + imap(*g))


def _dot_t(a, b):
    return lax.dot_general(a, b, (((1,), (1,)), ((), ())), preferred_element_type=F32)


def _mm_kernel(a_ref, b_ref, o_ref, acc_ref, *, nk, trans_b):
    b = b_ref[...].astype(BF)
    part = _dot_t(a_ref[...], b) if trans_b else _dot(a_ref[...], b)
    if nk == 1:
        o_ref[...] = part.astype(o_ref.dtype)
    else:
        k = pl.program_id(2)

        @pl.when(k == 0)
        def _():
            acc_ref[...] = part

        @pl.when(k > 0)
        def _():
            acc_ref[...] += part

        @pl.when(k == nk - 1)
        def _():
            o_ref[...] = acc_ref[...].astype(o_ref.dtype)


def _mm(a, b, *, tm, tn, tk=None, out_dtype=F32, trans_b=False, name="mm"):
    m, kdim = a.shape
    b, lead = _layered(b)
    n = b.shape[-2] if trans_b else b.shape[-1]
    tk = kdim if tk is None else tk
    nk = kdim // tk
    assert m % tm == 0 and kdim % tk == 0
    acc_shape = (tm, tn) if nk > 1 else (8, LANES)
    b_spec = (_wspec(lead, (tn, tk), lambda i, j, k: (j, k)) if trans_b
              else _wspec(lead, (tk, tn), lambda i, j, k: (k, j)))
    return pl.pallas_call(
        functools.partial(_mm_kernel, nk=nk, trans_b=trans_b),
        grid=(m // tm, pl.cdiv(n, tn), nk),
        in_specs=[pl.BlockSpec((tm, tk), lambda i, j, k: (i, k)), b_spec],
        out_specs=pl.BlockSpec((tm, tn), lambda i, j, k: (i, j)),
        out_shape=jax.ShapeDtypeStruct((m, n), out_dtype),
        scratch_shapes=[pltpu.VMEM(acc_shape, F32)],
        compiler_params=_cparams(("parallel", "parallel", "arbitrary")),
        name=name,
    )(a, b)


def _prep_kernel(c_ref, k_ref, gq_ref, gkv_ref, gk_ref, bk_ref, cq_o, ckv_o, kidx_o):
    c = c_ref[...]
    cq = c[:, :Q_LORA]
    ckv = c[:, Q_LORA:]
    cq = cq * lax.rsqrt(jnp.mean(cq * cq, axis=-1, keepdims=True) + LN_EPS) * gq_ref[...]
    ckv = ckv * lax.rsqrt(jnp.mean(ckv * ckv, axis=-1, keepdims=True) + LN_EPS) * gkv_ref[...]
    k = k_ref[...]
    mu = jnp.mean(k, axis=-1, keepdims=True)
    kc = k - mu
    var = jnp.mean(kc * kc, axis=-1, keepdims=True)
    kn = kc * lax.rsqrt(var + LN_EPS) * gk_ref[...] + bk_ref[...]
    cq_o[...] = cq.astype(BF)
    ckv_o[...] = ckv.astype(BF)
    kidx_o[...] = kn.astype(BF)


def _prep(z, g_cq, g_ckv, g_kidx, b_kidx, *, tm=512):
    m = z.shape[0]
    wc = Q_LORA + KV_LORA
    row = lambda a: a.reshape(1, -1)
    full = lambda n: pl.BlockSpec((1, n), lambda i: (0, 0))
    return pl.pallas_call(
        _prep_kernel,
        grid=(m // tm,),
        in_specs=[pl.BlockSpec((tm, wc), lambda i: (i, OFF_CQ // wc)),
                  pl.BlockSpec((tm, D_IDX), lambda i: (i, OFF_KIDX // D_IDX)),
                  full(Q_LORA), full(KV_LORA), full(D_IDX), full(D_IDX)],
        out_specs=[pl.BlockSpec((tm, Q_LORA), lambda i: (i, 0)),
                   pl.BlockSpec((tm, KV_LORA), lambda i: (i, 0)),
                   pl.BlockSpec((tm, D_IDX), lambda i: (i, 0))],
        out_shape=[jax.ShapeDtypeStruct((m, Q_LORA), BF),
                   jax.ShapeDtypeStruct((m, KV_LORA), BF),
                   jax.ShapeDtypeStruct((m, D_IDX), BF)],
        compiler_params=_cparams(("parallel",)),
        name="prep_norms",
    )(z, z, row(g_cq), row(g_ckv), row(g_kidx), row(b_kidx))


def _to_key(x):
    b = lax.bitcast_convert_type(x, jnp.int32)
    return b ^ ((b >> 31) & jnp.int32(0x7FFFFFFF))


def _kth_largest_key(count_ge, k, shape):
    kf = jnp.float32(k)
    zero = jnp.zeros(shape, jnp.int32)
    t0 = jnp.where(count_ge(zero) >= kf, zero, jnp.full(shape, INT_MIN, jnp.int32))

    def body(it, t):
        cand = t + jnp.left_shift(jnp.int32(1), jnp.int32(30) - it)
        return jnp.where(count_ge(cand) >= kf, cand, t)

    t = lax.fori_loop(0, 31, body, t0)
    return jnp.maximum(t, jnp.int32(INT_MIN + 1))


FOLD_ROWS = 64


def _fold_rows(x):
    n, w = x.shape
    return x.reshape(n // FOLD_ROWS, FOLD_ROWS, w).sum(axis=0)


MASKED = 2.0 * NEG


def _online_update_t(m_ref, l_ref, acc_ref, slots, logits, masks, v_ts):
    scaled = []
    for sl, s, mask in zip(slots, logits, masks):
        s = jnp.where(mask, s, MASKED)
        m_old = m_ref[sl]
        m_new = jnp.maximum(m_old, jnp.max(s, axis=0, keepdims=True))
        alpha = jnp.exp(m_old - m_new)
        p = jnp.exp(s - m_new)
        l_ref[sl] = alpha * l_ref[sl] + jnp.sum(p, axis=0, keepdims=True)
        m_ref[sl] = m_new
        scaled.append((alpha, p.astype(BF)))
    for sl, (alpha, p), v_t in zip(slots, scaled, v_ts):
        acc_ref[sl] = alpha * acc_ref[sl] + _dot(v_t, p)


def _dsa_kernel(brow_ref, qi_ref, q_ref, w_ref, kidx_ref, ckv_ref, kvtf_ref, kvtn_ref, wuk_ref, wuvt_ref, neart_ref,
                o_ref, keyf_ref, keyn_ref, qstk_ref, qlstk_ref, m_ref, l_ref, acc_ref, *, topk):
    i = pl.program_id(1)
    tq = TQ
    npair = H_A // 2
    far_end = jnp.maximum(i - 1, 0) * tq
    nfar = (far_end + TK - 1) // TK
    row0 = pl.multiple_of(i * tq, tq)

    for h in range(H_IDX):
        qstk_ref[h * tq:(h + 1) * tq, :] = qi_ref[:, h * D_IDX:(h + 1) * D_IDX]
    w_t = (w_ref[...] * (D_IDX ** -0.5 * H_IDX ** -0.5)).T

    def scores_t(krows):
        acc = jnp.zeros((krows.shape[0], tq), F32)
        for hp in range(H_IDX // 2):
            s = _dot_t(krows, qstk_ref[2 * hp * tq:(2 * hp + 2) * tq, :])
            acc = (acc + jnp.maximum(s[:, :tq], 0.0) * w_t[2 * hp:2 * hp + 1, :]
                   + jnp.maximum(s[:, tq:], 0.0) * w_t[2 * hp + 1:2 * hp + 2, :])
        return acc

    rrow = lax.broadcasted_iota(jnp.int32, (2 * tq, tq), 0)
    ccol = lax.broadcasted_iota(jnp.int32, (2 * tq, tq), 1)
    valid_n = (tq + ccol - rrow >= 0) & ((i - 1) * tq + rrow >= 0)
    keyn_ref[...] = jnp.where(valid_n, _to_key(scores_t(kidx_ref[pl.ds(row0, 2 * tq), :])), jnp.int32(INT_MIN))

    def far_scores(kt, c):
        r0 = pl.multiple_of(tq + kt * TK, tq)
        s = scores_t(kidx_ref[pl.ds(r0, TK), :])
        pos = kt * TK + lax.broadcasted_iota(jnp.int32, (TK, tq), 0)
        keyf_ref[kt] = jnp.where(pos < far_end, _to_key(s), jnp.int32(INT_MIN))
        return c

    lax.fori_loop(0, nfar, far_scores, 0)

    def count_ge(cand):
        part = _fold_rows(jnp.where(keyn_ref[...] >= cand, 1.0, 0.0))

        def body(kt, a):
            return a + _fold_rows(jnp.where(keyf_ref[kt] >= cand, 1.0, 0.0))

        part = lax.fori_loop(0, nfar, body, part)
        return jnp.sum(part, axis=0, keepdims=True)

    thr = _kth_largest_key(count_ge, topk, (1, tq))

    for h in range(H_A):
        ql = _dot(q_ref[:, h * HEAD_DIM:(h + 1) * HEAD_DIM], wuk_ref[h]) * (HEAD_DIM ** -0.5)
        qlstk_ref[h * tq:(h + 1) * tq, :] = ql.astype(BF)

    for hp in range(npair):
        m_ref[hp] = jnp.full((1, 2 * tq), NEG, F32)
        l_ref[hp] = jnp.zeros((1, 2 * tq), F32)
        acc_ref[hp] = jnp.zeros((KV_LORA, 2 * tq), F32)

    pairs = range(npair)

    def update(kv, kv_t, bias, sel):
        sel2 = jnp.concatenate([sel, sel], axis=1)
        logits = [_dot_t(kv, qlstk_ref[2 * hp * tq:(2 * hp + 2) * tq, :]) + bias(hp) for hp in pairs]
        _online_update_t(m_ref, l_ref, acc_ref, pairs, logits, [sel2] * npair, [kv_t] * npair)

    update(ckv_ref[pl.ds(row0, 2 * tq), :], jnp.concatenate([kvtn_ref[i], kvtn_ref[i + 1]], axis=1),
           lambda hp: neart_ref[hp], keyn_ref[...] >= thr)

    def far_attn(kt, c):
        r0 = pl.multiple_of(tq + kt * TK, tq)
        update(ckv_ref[pl.ds(r0, TK), :], kvtf_ref[kt], lambda hp: brow_ref[hp], keyf_ref[kt] >= thr)
        return c

    lax.fori_loop(0, nfar, far_attn, 0)

    for hp in range(npair):
        l = l_ref[hp]
        o_lat_t = (acc_ref[hp] * jnp.where(l > 0.0, 1.0 / l, 0.0)).astype(BF)
        for hh in range(2):
            h = 2 * hp + hh
            o_t = _dot(wuvt_ref[h], o_lat_t[:, hh * tq:(hh + 1) * tq])
            o_ref[:, h * HEAD_DIM:(h + 1) * HEAD_DIM] = o_t.T.astype(o_ref.dtype)


def _dsa(qq, z, kidx_n, ckv_n, w_uk, w_uv, near_a, far_a, *, seq):
    bsz = qq.shape[0]
    topk = min(TOPK_MAX, seq // 4)
    nqi = H_IDX * D_IDX
    npair = H_A // 2
    kidx_pad = _pad_front(kidx_n, TQ)
    ckv_pad = _pad_front(ckv_n, TQ)
    kvt_far = jnp.transpose(ckv_n.reshape(bsz, seq // TK, TK, KV_LORA), (0, 1, 3, 2))
    kvt_near = jnp.transpose(ckv_pad.reshape(bsz, seq // TQ + 1, TQ, KV_LORA), (0, 1, 3, 2))
    wuk_t = jnp.transpose(w_uk, (1, 2, 0)).astype(BF)
    wuv_t = jnp.transpose(w_uv, (1, 2, 0)).astype(BF)
    brow = jnp.repeat(far_a.reshape(npair, 2), TQ, axis=1).reshape(npair, 1, 2 * TQ)
    near_t = jnp.transpose(jnp.transpose(near_a, (0, 2, 1)).reshape(npair, 2, 2 * TQ, TQ),
                           (0, 2, 1, 3)).reshape(npair, 2 * TQ, 2 * TQ)
    whole = lambda shape: pl.BlockSpec(shape, lambda b, i: (0,) * len(shape))
    perb = lambda shape: pl.BlockSpec((None,) + shape, lambda b, i: (b,) + (0,) * len(shape))
    return pl.pallas_call(
        functools.partial(_dsa_kernel, topk=topk),
        grid=(bsz, seq // TQ),
        in_specs=[whole((npair, 1, 2 * TQ)),
                  pl.BlockSpec((None, TQ, nqi), lambda b, i: (b, i, 0)),
                  pl.BlockSpec((None, TQ, H_A * HEAD_DIM), lambda b, i: (b, i, nqi // (H_A * HEAD_DIM))),
                  pl.BlockSpec((None, TQ, LANES), lambda b, i: (b, i, OFF_WIDX // LANES)),
                  perb((seq + TQ, D_IDX)), perb((seq + TQ, KV_LORA)),
                  perb((seq // TK, KV_LORA, TK)), perb((seq // TQ + 1, KV_LORA, TQ)),
                  whole((H_A, HEAD_DIM, KV_LORA)), whole((H_A, HEAD_DIM, KV_LORA)),
                  whole((npair, 2 * TQ, 2 * TQ))],
        out_specs=pl.BlockSpec((None, TQ, H_A * HEAD_DIM), lambda b, i: (b, i, 0)),
        out_shape=jax.ShapeDtypeStruct((bsz, seq, H_A * HEAD_DIM), BF),
        scratch_shapes=[pltpu.VMEM((seq // TK, TK, TQ), jnp.int32),
                        pltpu.VMEM((2 * TQ, TQ), jnp.int32),
                        pltpu.VMEM((H_IDX * TQ, D_IDX), BF),
                        pltpu.VMEM((H_A * TQ, KV_LORA), BF),
                        pltpu.VMEM((npair, 1, 2 * TQ), F32),
                        pltpu.VMEM((npair, 1, 2 * TQ), F32),
                        pltpu.VMEM((npair, KV_LORA, 2 * TQ), F32)],
        compiler_params=_cparams(("parallel", "parallel")),
        name="dsa_mixer",
    )(brow, qq, qq, z, kidx_pad, ckv_pad, kvt_far, kvt_near, wuk_t, wuv_t, near_t)


def _cmp_kernel(x_ref, pe_ref, w1_ref, w2_ref, o_ref, xp_ref, *, seq, ncp):
    xp_ref[0:seq, :] = x_ref[...]
    xp_ref[seq:seq + CMP_STRIDE, :] = jnp.zeros((CMP_STRIDE, HEAD_DIM), F32)
    acc = jnp.zeros((ncp, CMP_HIDDEN), F32)
    for l in range(CMP_BLOCK):
        rows = xp_ref[pl.ds(l, ncp, stride=CMP_STRIDE), :]
        blk = rows + pe_ref[l:l + 1, :]
        acc = acc + _dot(blk.astype(BF), w1_ref[l].astype(BF))
    hdn = jax.nn.gelu(acc)
    o_ref[...] = _dot(hdn.astype(BF), w2_ref[...].astype(BF)).astype(o_ref.dtype)


def _compress(z, cmp_pe, cmp_w1, cmp_w2, *, seq):
    bsz = z.shape[0]
    ncp = seq // CMP_STRIDE
    cmp_pe, lead_pe = _layered(cmp_pe)
    cmp_w1, lead_w1 = _layered(cmp_w1)
    cmp_w2, lead_w2 = _layered(cmp_w2)
    return pl.pallas_call(
        functools.partial(_cmp_kernel, seq=seq, ncp=ncp),
        grid=(bsz, 2, G_B),
        in_specs=[pl.BlockSpec((None, seq, HEAD_DIM), lambda b, c, g: (b, 0, OFF_KVB // HEAD_DIM + c * G_B + g)),
                  _wspec(lead_pe, (None, CMP_BLOCK, HEAD_DIM), lambda b, c, g: (c, 0, 0)),
                  _wspec(lead_w1, (None, CMP_BLOCK, HEAD_DIM, CMP_HIDDEN), lambda b, c, g: (c, 0, 0, 0)),
                  _wspec(lead_w2, (None, CMP_HIDDEN, HEAD_DIM), lambda b, c, g: (c, 0, 0))],
        out_specs=pl.BlockSpec((None, None, None, ncp, HEAD_DIM), lambda b, c, g: (b, c, g, 0, 0)),
        out_shape=jax.ShapeDtypeStruct((bsz, 2, G_B, ncp, HEAD_DIM), BF),
        scratch_shapes=[pltpu.VMEM((seq + CMP_STRIDE, HEAD_DIM), F32)],
        compiler_params=_cparams(("parallel", "parallel", "parallel")),
        name="nsa_compress",
    )(z, cmp_pe, cmp_w1, cmp_w2)


def _nsa_kernel(brow_ref, q_ref, gb_ref, kc_ref, vct_ref, ksl_ref, vstf_ref, vstn_ref, kw_ref, vwt_ref, et_ref, ovt_ref,
                nearts_ref, neartw_ref, o_ref, qstk_ref, imp_ref, mselt_ref, m_ref, l_ref, acc_ref, ot_ref, *, seq, npick):
    i = pl.program_id(1)
    tq = TQ
    hpg = H_B // G_B
    gl = hpg * tq
    ncp = seq // CMP_STRIDE
    nblk = SEL_BLOCK
    far_end = jnp.maximum(i - 1, 0) * tq
    nfar = (far_end + TK - 1) // TK
    row0 = pl.multiple_of(i * tq, tq)
    tile4 = lambda x: jnp.concatenate([x] * hpg, axis=1)
    tlane = i * tq + lax.broadcasted_iota(jnp.int32, (1, tq), 1)

    for h in range(H_B):
        qstk_ref[h * tq:(h + 1) * tq, :] = (q_ref[:, h * HEAD_DIM:(h + 1) * HEAD_DIM] * (HEAD_DIM ** -0.5)).astype(BF)
    sig_t = jax.nn.sigmoid(gb_ref[...]).T

    def gate_row(g, c):
        return jnp.concatenate([sig_t[H_IDX + 3 * (g * hpg + j) + c:H_IDX + 3 * (g * hpg + j) + c + 1, :]
                                for j in range(hpg)], axis=1)

    qgrp = lambda g: qstk_ref[g * gl:(g + 1) * gl, :]
    grows = lambda g: slice(g * HEAD_DIM, (g + 1) * HEAD_DIM)

    nrow = lax.broadcasted_iota(jnp.int32, (ncp, tq), 0)
    cmp_mask = tile4(nrow * CMP_STRIDE + (CMP_BLOCK - 1) <= tlane)
    anyc = tile4(jnp.where(tlane >= CMP_BLOCK - 1, 1.0, 0.0))
    brow = lax.broadcasted_iota(jnp.int32, (nblk, tq), 0)
    cur = tlane // SEL_BLOCK
    forced = (brow == 0) | (brow == cur) | (brow == cur - 1)
    sub = lax.broadcasted_iota(jnp.int32, (8, tq), 0)
    for g in range(G_B):
        s = jnp.where(cmp_mask, _dot_t(kc_ref[g], qgrp(g)), NEG)
        e = jnp.exp(s - jnp.max(s, axis=0, keepdims=True))
        p = e * (anyc / jnp.sum(e, axis=0, keepdims=True))
        ot_ref[g] = gate_row(g, 0) * _dot(vct_ref[g], p.astype(BF))
        psum = p[:, :tq]
        for j in range(1, hpg):
            psum = psum + p[:, j * tq:(j + 1) * tq]
        hi = psum.astype(BF)
        lo = (psum - hi.astype(F32)).astype(BF)
        imp = (_dot(ovt_ref[...], hi) + _dot(ovt_ref[...], lo))[:nblk]
        imp = jnp.where(brow > cur, NEG, jnp.where(forced, FORCE_SCORE, imp))
        imp_ref[...] = imp
        vals = [imp[8 * r:8 * r + 8] for r in range(nblk // 8)]
        rank = [jnp.zeros((8, tq), F32) for _ in vals]
        for mp in range(nblk):
            other = jnp.broadcast_to(imp_ref[mp:mp + 1, :], (8, tq))
            for r, v in enumerate(vals):
                if 8 * r > mp:
                    beats = other >= v
                elif 8 * r + 7 < mp:
                    beats = other > v
                else:
                    beats = (other > v) | ((other == v) & (sub + 8 * r > mp))
                rank[r] = rank[r] + jnp.where(beats, 1.0, 0.0)
        sel = jnp.concatenate([jnp.where(rk < npick, 1.0, 0.0) for rk in rank], axis=0)
        mselt_ref[g] = jnp.concatenate([sel, jnp.zeros((LANES - nblk, tq), F32)], axis=0).astype(BF)

    rrow = lax.broadcasted_iota(jnp.int32, (2 * tq, tq), 0)
    ccol = lax.broadcasted_iota(jnp.int32, (2 * tq, tq), 1)
    causal_n = tq + ccol - rrow >= 0
    etn = et_ref[pl.ds(row0, 2 * tq), :]
    groups = range(G_B)
    for g in groups:
        m_ref[g] = jnp.full((1, gl), NEG, F32)
        l_ref[g] = jnp.zeros((1, gl), F32)
        acc_ref[g] = jnp.zeros((HEAD_DIM, gl), F32)
    _online_update_t(
        m_ref, l_ref, acc_ref, groups,
        [_dot_t(ksl_ref[pl.ds(row0, 2 * tq), grows(g)], qgrp(g)) + nearts_ref[g] for g in groups],
        [tile4(causal_n & (_dot(etn, mselt_ref[g]) > 0.5)) for g in groups],
        [jnp.concatenate([vstn_ref[i, grows(g), :], vstn_ref[i + 1, grows(g), :]], axis=1) for g in groups])

    def far_attn(kt, c):
        r0 = pl.multiple_of(tq + kt * TK, tq)
        ett = et_ref[pl.ds(r0, TK), :]
        infar = kt * TK + lax.broadcasted_iota(jnp.int32, (TK, tq), 0) < far_end
        _online_update_t(
            m_ref, l_ref, acc_ref, groups,
            [_dot_t(ksl_ref[pl.ds(r0, TK), grows(g)], qgrp(g)) + brow_ref[g] for g in groups],
            [tile4(infar & (_dot(ett, mselt_ref[g]) > 0.5)) for g in groups],
            [vstf_ref[kt, grows(g), :] for g in groups])
        return c

    lax.fori_loop(0, nfar, far_attn, 0)
    for g in groups:
        l = l_ref[g]
        ot_ref[g] += gate_row(g, 1) * (acc_ref[g] * jnp.where(l > 0.0, 1.0 / l, 0.0))

    span = WIN_B + tq
    rw = lax.broadcasted_iota(jnp.int32, (span, tq), 0)
    cw = lax.broadcasted_iota(jnp.int32, (span, tq), 1)
    distw = WIN_B + cw - rw
    mask_w = tile4((distw >= 0) & (distw < WIN_B) & (i * tq - WIN_B + rw >= 0))
    for g in groups:
        s = jnp.where(mask_w, _dot_t(kw_ref[pl.ds(row0, span), grows(g)], qgrp(g)) + neartw_ref[g], NEG)
        e = jnp.exp(s - jnp.max(s, axis=0, keepdims=True))
        p = e / jnp.sum(e, axis=0, keepdims=True)
        vw_t = jnp.concatenate([vwt_ref[i + k, grows(g), :] for k in range(span // tq)], axis=1)
        ot_ref[g] += gate_row(g, 2) * _dot(vw_t, p.astype(BF))

    for h in range(H_B):
        g, j = divmod(h, hpg)
        o_ref[:, h * HEAD_DIM:(h + 1) * HEAD_DIM] = ot_ref[g][:, j * tq:(j + 1) * tq].T.astype(o_ref.dtype)


def _tiles_t(a, rows):
    bsz, n, w = a.shape
    return jnp.transpose(a.reshape(bsz, n // rows, rows, w), (0, 1, 3, 2))


def _lanes_by_group(near, groups):
    h, tq, w = near.shape
    t = jnp.transpose(near, (0, 2, 1)).reshape(groups, h // groups, w, tq)
    return jnp.transpose(t, (0, 2, 1, 3)).reshape(groups, w, (h // groups) * tq)


def _nsa(z, kcv, k_slc, v_slc, k_win, v_win, et_pad, overlap_t, near_s, near_w, far_s, *, seq):
    bsz = z.shape[0]
    ncp = seq // CMP_STRIDE
    npick = min(N_SEL, seq // SEL_BLOCK)
    assert seq // SEL_BLOCK <= SEL_BLOCK
    gw = G_B * HEAD_DIM
    hpg = H_B // G_B
    gl = hpg * TQ
    kc = kcv[:, 0]
    vc_t = jnp.transpose(kcv[:, 1], (0, 1, 3, 2))
    brow = jnp.repeat(far_s.reshape(G_B, hpg), TQ, axis=1).reshape(G_B, 1, gl)
    whole = lambda shape: pl.BlockSpec(shape, lambda b, i: (0,) * len(shape))
    perb = lambda shape: pl.BlockSpec((None,) + shape, lambda b, i: (b,) + (0,) * len(shape))
    return pl.pallas_call(
        functools.partial(_nsa_kernel, seq=seq, npick=npick),
        grid=(bsz, seq // TQ),
        in_specs=[whole((G_B, 1, gl)),
                  pl.BlockSpec((None, TQ, H_B * HEAD_DIM), lambda b, i: (b, i, OFF_QB // (H_B * HEAD_DIM))),
                  pl.BlockSpec((None, TQ, LANES), lambda b, i: (b, i, OFF_WIDX // LANES)),
                  perb((G_B, ncp, HEAD_DIM)), perb((G_B, HEAD_DIM, ncp)),
                  perb((seq + TQ, gw)), perb((seq // TK, gw, TK)), perb((seq // TQ + 1, gw, TQ)),
                  perb((seq + WIN_B, gw)), perb(((seq + WIN_B) // TQ, gw, TQ)),
                  whole((seq + TQ, LANES)), whole((LANES, ncp)),
                  whole((G_B, 2 * TQ, gl)), whole((G_B, WIN_B + TQ, gl))],
        out_specs=pl.BlockSpec((None, TQ, H_B * HEAD_DIM), lambda b, i: (b, i, 0)),
        out_shape=jax.ShapeDtypeStruct((bsz, seq, H_B * HEAD_DIM), BF),
        scratch_shapes=[pltpu.VMEM((H_B * TQ, HEAD_DIM), BF),
                        pltpu.VMEM((SEL_BLOCK, TQ), F32),
                        pltpu.VMEM((G_B, LANES, TQ), BF),
                        pltpu.VMEM((G_B, 1, gl), F32),
                        pltpu.VMEM((G_B, 1, gl), F32),
                        pltpu.VMEM((G_B, HEAD_DIM, gl), F32),
                        pltpu.VMEM((G_B, HEAD_DIM, gl), F32)],
        compiler_params=_cparams(("parallel", "parallel")),
        name="nsa_mixer",
    )(brow, z, z, kc, vc_t, _pad_front(k_slc, TQ), _tiles_t(v_slc, TK), _tiles_t(_pad_front(v_slc, TQ), TQ),
      _pad_front(k_win, WIN_B), _tiles_t(_pad_front(v_win, WIN_B), TQ), et_pad, overlap_t,
      _lanes_by_group(near_s, G_B), _lanes_by_group(near_w, G_B))


def _swa_kernel(sink_ref, q_ref, k_ref, v_ref, near_ref, o_ref):
    i = pl.program_id(1)
    tq = TQ
    hpg = H_C // KV_C
    row0 = pl.multiple_of(i * tq, tq)
    qrow = lax.broadcasted_iota(jnp.int32, (tq, 2 * tq), 0)
    jcol = lax.broadcasted_iota(jnp.int32, (tq, 2 * tq), 1)
    dist = WIN_C + qrow - jcol
    mask = (dist >= 0) & (dist < WIN_C) & (i * tq - WIN_C + jcol >= 0)
    kk = k_ref[pl.ds(row0, 2 * tq), :]
    vv = v_ref[pl.ds(row0, 2 * tq), :]
    for h in range(H_C):
        g = h // hpg
        kg = kk[:, g * HEAD_DIM_C:(g + 1) * HEAD_DIM_C]
        vg = vv[:, g * HEAD_DIM_C:(g + 1) * HEAD_DIM_C]
        qh = (q_ref[:, h * HEAD_DIM_C:(h + 1) * HEAD_DIM_C] * (HEAD_DIM_C ** -0.5)).astype(BF)
        s = jnp.where(mask, _dot_t(qh, kg) + near_ref[h], NEG)
        sink = sink_ref[h]
        m = jnp.maximum(jnp.max(s, axis=-1, keepdims=True), sink)
        e = jnp.exp(s - m)
        p = e / (jnp.sum(e, axis=-1, keepdims=True) + jnp.exp(sink - m))
        o_ref[:, h * HEAD_DIM_C:(h + 1) * HEAD_DIM_C] = _dot(p.astype(BF), vg).astype(o_ref.dtype)


def _swa(z, kc_pad, vc_pad, near_c, sinks, *, seq):
    bsz = z.shape[0]
    kvw = KV_C * HEAD_DIM_C
    return pl.pallas_call(
        _swa_kernel,
        grid=(bsz, seq // TQ),
        in_specs=[pl.BlockSpec(memory_space=pltpu.SMEM),
                  pl.BlockSpec((None, TQ, BRANCH_W), lambda b, i: (b, i, OFF_QC // BRANCH_W)),
                  pl.BlockSpec((None, seq + WIN_C, kvw), lambda b, i: (b, 0, 0)),
                  pl.BlockSpec((None, seq + WIN_C, kvw), lambda b, i: (b, 0, 0)),
                  pl.BlockSpec((H_C, TQ, 2 * TQ), lambda b, i: (0, 0, 0))],
        out_specs=pl.BlockSpec((None, TQ, BRANCH_W), lambda b, i: (b, i, 0)),
        out_shape=jax.ShapeDtypeStruct((bsz, seq, BRANCH_W), BF),
        compiler_params=_cparams(("parallel", "parallel")),
        name="swa_mixer",
    )(sinks, z, kc_pad, vc_pad, near_c)


def _merge_kernel(x_ref, wg_ref, o_ref, wbr_ref, out_ref, acc_ref):
    br = pl.program_id(2)
    gate = jax.nn.sigmoid(_dot_t(x_ref[...], wg_ref[...].astype(BF)))
    val = _dot(o_ref[...], wbr_ref[...].astype(BF))

    @pl.when(br == 0)
    def _():
        acc_ref[...] = gate * val

    @pl.when(br > 0)
    def _():
        acc_ref[...] += gate * val

    @pl.when(br == N_BRANCH - 1)
    def _():
        out_ref[...] = acc_ref[...].astype(out_ref.dtype)


def _merge(xb, w_in_t, li, gate_row0, o_all, w_br, *, tm, tn):
    m, d = xb.shape
    nj = d // tn
    w_br, lead = _layered(w_br)
    depth, n_in, _ = w_in_t.shape
    row0 = li * n_in + gate_row0
    return pl.pallas_call(
        _merge_kernel,
        grid=(m // tm, nj, N_BRANCH),
        in_specs=[pl.BlockSpec((tm, d), lambda i, j, r: (i, 0)),
                  pl.BlockSpec((pl.Element(tn), pl.Element(d)),
                               lambda i, j, r: (pl.multiple_of(row0 + r * d + j * tn, 8), 0)),
                  pl.BlockSpec((None, tm, BRANCH_W), lambda i, j, r: (r, i, 0)),
                  _wspec(lead, (None, BRANCH_W, tn), lambda i, j, r: (r, 0, j))],
        out_specs=pl.BlockSpec((tm, tn), lambda i, j, r: (i, j)),
        out_shape=jax.ShapeDtypeStruct((m, d), BF),
        scratch_shapes=[pltpu.VMEM((tm, tn), F32)],
        compiler_params=_cparams(("parallel", "parallel", "arbitrary")),
        name="merge_gates",
    )(xb, w_in_t.reshape(depth * n_in, d), o_all, w_br)


def _resln_kernel(*refs, nj, tn, alpha, gated):
    if gated:
        (a_ref, w_ref, res_ref, g_ref, b_ref, bias_ref, p_ref, wp_ref, add_ref,
         of_ref, ob_ref, y_ref, mu_ref, rs_ref) = refs
    else:
        a_ref, w_ref, res_ref, g_ref, b_ref, of_ref, ob_ref, y_ref, mu_ref, rs_ref = refs
    j = pl.program_id(1)

    @pl.when(j < nj)
    def _():
        y = _dot(a_ref[...], w_ref[...].astype(BF))
        if gated:
            y = jax.nn.sigmoid(y + bias_ref[...]) * _dot(p_ref[...], wp_ref[...].astype(BF)) + add_ref[...]
        y_ref[j] = alpha * res_ref[...] + y

    @pl.when(j == nj - 1)
    def _():
        tm = y_ref.shape[1]
        d = nj * tn
        tot = jnp.zeros((tm, 1), F32)
        for c in range(nj):
            tot = tot + jnp.sum(y_ref[c], axis=-1, keepdims=True)
        mu = tot / d
        sq = jnp.zeros((tm, 1), F32)
        for c in range(nj):
            dc = y_ref[c] - mu
            sq = sq + jnp.sum(dc * dc, axis=-1, keepdims=True)
        mu_ref[...] = mu
        rs_ref[...] = lax.rsqrt(sq / d + LN_EPS)

    @pl.when(j >= nj)
    def _():
        o = (y_ref[j - nj] - mu_ref[...]) * rs_ref[...] * g_ref[...] + b_ref[...]
        of_ref[...] = o
        ob_ref[...] = o.astype(BF)


def _resln(a, w, res, g, b, *, alpha, tm, tn, gated=None, name="res_ln"):
    m, kdim = a.shape
    w, lead = _layered(w)
    d = w.shape[-1]
    nj = d // tn
    mm_col = lambda i, j: (0, jnp.minimum(j, nj - 1))
    mm_blk = lambda i, j: (i, jnp.minimum(j, nj - 1))
    out_col = lambda i, j: (0, jnp.maximum(j - nj, 0))
    out_blk = lambda i, j: (i, jnp.maximum(j - nj, 0))
    in_specs = [pl.BlockSpec((tm, kdim), lambda i, j: (i, 0)),
                _wspec(lead, (kdim, tn), mm_col),
                pl.BlockSpec((tm, tn), mm_blk),
                pl.BlockSpec((1, tn), out_col),
                pl.BlockSpec((1, tn), out_col)]
    args = [a, w, res, g.reshape(1, d), b.reshape(1, d)]
    if gated is not None:
        bias, p, wp, add = gated
        wp, lead_p = _layered(wp)
        dp = p.shape[1]
        in_specs += [pl.BlockSpec((1, tn), mm_col),
                     pl.BlockSpec((tm, dp), lambda i, j: (i, 0)),
                     _wspec(lead_p, (dp, tn), mm_col),
                     pl.BlockSpec((tm, tn), mm_blk)]
        args += [bias.reshape(1, d), p, wp, add]
    return pl.pallas_call(
        functools.partial(_resln_kernel, nj=nj, tn=tn, alpha=alpha, gated=gated is not None),
        grid=(m // tm, 2 * nj),
        in_specs=in_specs,
        out_specs=[pl.BlockSpec((tm, tn), out_blk), pl.BlockSpec((tm, tn), out_blk)],
        out_shape=[jax.ShapeDtypeStruct((m, d), F32), jax.ShapeDtypeStruct((m, d), BF)],
        scratch_shapes=[pltpu.VMEM((nj, tm, tn), F32), pltpu.VMEM((tm, 1), F32), pltpu.VMEM((tm, 1), F32)],
        compiler_params=_cparams(("parallel", "arbitrary")),
        name=name,
    )(*args)


HALO = 8


def _ffn_up_kernel(x_ref, wg_ref, wv_ref, cwg_ref, cwv_ref, cbg_ref, cbv_ref, a_ref, hg_ref, hv_ref, *, tm, ch, tiles_per_seq):
    i = pl.program_id(1)
    first = (i % tiles_per_seq) == 0

    @pl.when(first)
    def _():
        hg_ref[0:HALO, :] = jnp.zeros((HALO, hg_ref.shape[1]), F32)
        hv_ref[0:HALO, :] = jnp.zeros((HALO, hv_ref.shape[1]), F32)

    @pl.when(jnp.logical_not(first))
    def _():
        hg_ref[0:HALO, :] = hg_ref[tm:tm + HALO, :]
        hv_ref[0:HALO, :] = hv_ref[tm:tm + HALO, :]

    wg = wg_ref[...].astype(BF)
    wv = wv_ref[...].astype(BF)

    def conv(h_ref, c, cw_ref, cb_ref):
        out = cb_ref[...]
        for k in range(CONV_W):
            off = HALO + c * ch - (CONV_W - 1) + k
            out = out + h_ref[off:off + ch, :] * cw_ref[k:k + 1, :]
        return out

    def up(c):
        xc = x_ref[c * ch:(c + 1) * ch, :]
        hg_ref[HALO + c * ch:HALO + (c + 1) * ch, :] = _dot(xc, wg)
        hv_ref[HALO + c * ch:HALO + (c + 1) * ch, :] = _dot(xc, wv)

    up(0)
    for c in range(tm // ch):
        if c + 1 < tm // ch:
            up(c + 1)
        a_ref[c * ch:(c + 1) * ch, :] = (jax.nn.gelu(conv(hg_ref, c, cwg_ref, cbg_ref))
                                         * conv(hv_ref, c, cwv_ref, cbv_ref)).astype(a_ref.dtype)


def _ffn_up(xb, w_up, conv_w, conv_b, *, seq, tm, tn, ch=256):
    m, d = xb.shape
    w_up, lead = _layered(w_up)
    conv_w, lead_c = _layered(conv_w)
    conv_b, lead_b = _layered(conv_b)
    dff = w_up.shape[-1] // 2
    nj = dff // tn
    assert dff % tn == 0 and seq % tm == 0 and tm % ch == 0
    cb = conv_b.reshape(conv_b.shape[:-1] + (1, 2 * dff))
    return pl.pallas_call(
        functools.partial(_ffn_up_kernel, tm=tm, ch=ch, tiles_per_seq=seq // tm),
        grid=(nj, m // tm),
        in_specs=[pl.BlockSpec((tm, d), lambda j, i: (i, 0)),
                  _wspec(lead, (d, tn), lambda j, i: (0, j)),
                  _wspec(lead, (d, tn), lambda j, i: (0, j + nj)),
                  _wspec(lead_c, (CONV_W, tn), lambda j, i: (0, j)),
                  _wspec(lead_c, (CONV_W, tn), lambda j, i: (0, j + nj)),
                  _wspec(lead_b, (1, tn), lambda j, i: (0, j)),
                  _wspec(lead_b, (1, tn), lambda j, i: (0, j + nj))],
        out_specs=pl.BlockSpec((tm, tn), lambda j, i: (i, j)),
        out_shape=jax.ShapeDtypeStruct((m, dff), BF),
        scratch_shapes=[pltpu.VMEM((tm + HALO, tn), F32), pltpu.VMEM((tm + HALO, tn), F32)],
        compiler_params=_cparams(("parallel", "arbitrary")),
        name="ffn_up_conv_geglu",
    )(xb, w_up, w_up, conv_w, conv_w, cb, cb)


def _t5_bucket(dist):
    dist = jnp.maximum(dist, 0)
    max_exact = N_BUCKETS // 2
    d = jnp.maximum(dist, 1).astype(F32)
    large = max_exact + (jnp.log(d / max_exact) / math.log(MAX_DISTANCE / max_exact)
                         * (N_BUCKETS - max_exact)).astype(jnp.int32)
    large = jnp.minimum(large, N_BUCKETS - 1)
    return jnp.where(dist < max_exact, dist, large)


def _near_bias(tab, offset, width):
    period = TQ + width
    idx = np.arange(period)
    k = np.where(idx < width, idx, idx - period)
    dist = np.clip(offset - k, 0, MAX_DISTANCE)
    u = tab[_t5_bucket(jnp.asarray(dist, jnp.int32))].T
    flat = jnp.tile(u, (1, TQ))[:, :TQ * (period - 1)]
    return flat.reshape(-1, TQ, period - 1)[:, :, :width]


def _pad_front(a, n):
    return jnp.pad(a, ((0, 0), (n, 0), (0, 0)))


def _in_splits():
    s_kidx = Q_LORA + KV_LORA
    s_widx = s_kidx + D_IDX
    s_qb = s_widx + H_IDX
    s_kvb = s_qb + H_B * HEAD_DIM
    s_gb = s_kvb + 6 * G_B * HEAD_DIM
    s_qc = s_gb + 3 * H_B
    s_kc = s_qc + H_C * HEAD_DIM_C
    s_vc = s_kc + KV_C * HEAD_DIM_C
    s_gm = s_vc + KV_C * HEAD_DIM_C
    segs = [(s_qb, s_kvb), (s_qc, s_kc), (0, s_kidx), (s_kidx, s_widx), (s_kvb, s_gb),
            (s_kc, s_vc), (s_vc, s_gm), (s_widx, s_qb), (s_gb, s_qc)]
    return segs, s_gm


def _small_weights_t(w_in_t, li):
    segs, s_gm = _in_splits()
    d = w_in_t.shape[-1]
    parts = [w_in_t[li, a:b] for a, b in segs] + [jnp.zeros((N_SMALL - s_gm, d), w_in_t.dtype)]
    return jnp.concatenate(parts, axis=0)


def _layer(li, x, xb, p_b, w_in_t, g_cq, g_ckv, g_kidx, b_kidx, w_qidx, w_uq, w_uk, w_uv,
           cmp_pe, cmp_w1, cmp_w2, sinks, w_br, w_o, ln1_g, ln1_b,
           w_up, conv_w, conv_b, w_down_b, w_pg, b_pg, w_pp, ln2_g, ln2_b, consts, *, bsz, seq, alpha):
    m, d = x.shape
    z = _mm(xb, _small_weights_t(w_in_t, li), tm=1024, tn=512, trans_b=True, name="in_proj")
    cq_n, ckv_n, kidx_n = _prep(z, g_cq[li], g_ckv[li], g_kidx[li], b_kidx[li])
    w_qq = jnp.concatenate([w_qidx[li], w_uq[li]], axis=1).astype(BF)
    qq = _mm(cq_n, w_qq, tm=1024, tn=512, out_dtype=BF, name="q_proj")

    z3 = z.reshape(bsz, seq, N_SMALL)
    b3 = lambda a: a.reshape(bsz, seq, a.shape[-1])
    seg = lambda off, w: z3[:, :, off:off + w].astype(BF)

    o_a = _dsa(b3(qq), z3, b3(kidx_n), b3(ckv_n), w_uk[li], w_uv[li], consts["near_a"], consts["far_a"], seq=seq)

    gw = G_B * HEAD_DIM
    kcv = _compress(z3, (cmp_pe, li), (cmp_w1, li), (cmp_w2, li), seq=seq)
    o_b = _nsa(z3, kcv, seg(OFF_KVB + 2 * gw, gw), seg(OFF_KVB + 3 * gw, gw), seg(OFF_KVB + 4 * gw, gw),
               seg(OFF_KVB + 5 * gw, gw), consts["et_pad"], consts["overlap_t"],
               consts["near_s"], consts["near_w"], consts["far_s"], seq=seq)

    kvw = KV_C * HEAD_DIM_C
    o_c = _swa(z3, _pad_front(seg(OFF_KC, kvw), WIN_C), _pad_front(seg(OFF_VC, kvw), WIN_C),
               consts["near_c"], sinks[li], seq=seq)

    o_all = jnp.stack([o_a.reshape(m, BRANCH_W), o_b.reshape(m, BRANCH_W), o_c.reshape(m, BRANCH_W)])
    merged = _merge(xb, w_in_t, li, _in_splits()[1], o_all, (w_br, li), tm=1024, tn=512)
    x1, x1b = _resln(merged, (w_o, li), x, ln1_g[li], ln1_b[li], alpha=alpha, tm=1024, tn=256, name="attn_out_ln1")

    a = _ffn_up(x1b, (w_up, li), (conv_w, li), (conv_b, li), seq=seq, tm=min(1024, seq), tn=256)
    dff = w_down_b.shape[1]
    ffn = _mm(a, (w_down_b, li), tm=1024, tn=512, tk=dff // 2 if (dff // 2) % LANES == 0 else dff, name="ffn_down")
    return _resln(x1b, (w_pg, li), x1, ln2_g[li], ln2_b[li], alpha=alpha, tm=1024, tn=256,
                  gated=(b_pg[li], p_b, (w_pp, li), ffn), name="ple_ln2")


def kernel(x, p, w_in, g_cq, g_ckv, g_kidx, b_kidx, w_qidx, w_uq, w_uk, w_uv, cmp_pe, cmp_w1, cmp_w2, sinks, w_br, w_o, ln1_g, ln1_b, w_up, conv_w, conv_b, w_down, w_pg, b_pg, w_pp, ln2_g, ln2_b, rel_bias):
    bsz, seq, d = x.shape
    depth = p.shape[0]
    alpha = (2 * depth) ** 0.25
    m = bsz * seq
    assert seq % TK == 0 and d % 512 == 0

    key = np.arange(seq)[:, None]
    blk = np.arange(LANES)[None, :]
    et = (key // SEL_BLOCK == blk).astype(np.float32)
    et_pad = jnp.asarray(np.concatenate([np.zeros((TQ, LANES), np.float32), et]), BF)
    ncp = seq // CMP_STRIDE
    cs = np.arange(ncp)[:, None] * CMP_STRIDE
    bs = blk * SEL_BLOCK
    n_cmp = (seq - CMP_BLOCK) // CMP_STRIDE + 1
    ov = ((cs < bs + SEL_BLOCK) & (cs + CMP_BLOCK > bs) & (blk < seq // SEL_BLOCK)
          & (np.arange(ncp)[:, None] < n_cmp)).astype(np.float32)
    tab_a, tab_b, tab_c = rel_bias[:, :H_A], rel_bias[:, H_A:H_A + H_B], rel_bias[:, H_A + H_B:]
    consts = {"et_pad": et_pad, "overlap_t": jnp.asarray(ov.T, BF),
              "near_a": _near_bias(tab_a, TQ, 2 * TQ), "far_a": tab_a[N_BUCKETS - 1],
              "near_s": _near_bias(tab_b, TQ, 2 * TQ), "near_w": _near_bias(tab_b, WIN_B, WIN_B + TQ),
              "far_s": tab_b[N_BUCKETS - 1], "near_c": _near_bias(tab_c, WIN_C, 2 * TQ)}

    xf = x.reshape(m, d)
    xb = xf.astype(BF)
    w_down_b = w_down.astype(BF)
    w_in_t = jnp.swapaxes(w_in, 1, 2)
    for li in range(depth):
        xf, xb = _layer(li, xf, xb, p[li].reshape(m, -1).astype(BF), w_in_t, g_cq, g_ckv, g_kidx, b_kidx,
                        w_qidx, w_uq, w_uk, w_uv, cmp_pe, cmp_w1, cmp_w2, sinks,
                        w_br, w_o, ln1_g, ln1_b, w_up, conv_w, conv_b, w_down_b,
                        w_pg, b_pg, w_pp, ln2_g, ln2_b, consts, bsz=bsz, seq=seq, alpha=alpha)
    return xf.reshape(bsz, seq, d)
```

```python
import functools
import math

import jax
import jax.numpy as jnp
import numpy as np
from jax import lax
from jax.experimental import pallas as pl
from jax.experimental.pallas import tpu as pltpu

HEAD_DIM = 128
NEG = -1e30
FORCE_SCORE = 1e4
LN_EPS = 1e-5
N_BUCKETS = 32
MAX_DISTANCE = 128
H_A = 8
Q_LORA = 768
KV_LORA = 256
H_IDX = 32
D_IDX = 128
TOPK_MAX = 256
H_B = 8
G_B = 2
CMP_BLOCK = 32
CMP_STRIDE = 16
CMP_HIDDEN = 128
SEL_BLOCK = 64
N_SEL = 16
WIN_B = 512
H_C = 16
KV_C = 2
HEAD_DIM_C = 64
WIN_C = 128
N_BRANCH = 3
BRANCH_W = 1024
CONV_W = 3

LANES = 128
TQ = 128
TK = 512
VMEM_LIMIT = 56 * 1024 * 1024
INT_MIN = -2 ** 31

OFF_QB = 0
OFF_QC = 1024
OFF_CQ = 2048
OFF_CKV = 2816
OFF_KIDX = 3072
OFF_KVB = 3200
OFF_KC = 4736
OFF_VC = 4864
OFF_WIDX = 4992
N_SMALL = 5120

BF = jnp.bfloat16
F32 = jnp.float32


def _cparams(sem, vmem=VMEM_LIMIT):
    return pltpu.CompilerParams(dimension_semantics=sem, vmem_limit_bytes=vmem)


def _dot(a, b):
    return jnp.dot(a, b, preferred_element_type=F32)


def _layered(w):
    return (w[0], (w[1],)) if isinstance(w, tuple) else (w, ())


def _wspec(lead, block, imap):
    return pl.BlockSpec((None,) * len(lead) + block, lambda *g: lead + imap(*g))


def _dot_t(a, b):
    return lax.dot_general(a, b, (((1,), (1,)), ((), ())), preferred_element_type=F32)


def _mm_kernel(a_ref, b_ref, *refs, nk, trans_b):
    *o_refs, acc_ref = refs
    b = b_ref[...].astype(BF)
    part = _dot_t(a_ref[...], b) if trans_b else _dot(a_ref[...], b)

    def emit(val):
        for o_ref in o_refs:
            o_ref[...] = val.astype(o_ref.dtype)

    if nk == 1:
        emit(part)
    else:
        k = pl.program_id(2)

        @pl.when(k == 0)
        def _():
            acc_ref[...] = part

        @pl.when(k > 0)
        def _():
            acc_ref[...] += part

        @pl.when(k == nk - 1)
        def _():
            emit(acc_ref[...])


def _mm(a, b, *, tm, tn, tk=None, out_dtype=F32, also=None, trans_b=False, name="mm"):
    m, kdim = a.shape
    b, lead = _layered(b)
    n = b.shape[-2] if trans_b else b.shape[-1]
    tk = kdim if tk is None else tk
    nk = kdim // tk
    assert m % tm == 0 and kdim % tk == 0
    acc_shape = (tm, tn) if nk > 1 else (8, LANES)
    b_spec = (_wspec(lead, (tn, tk), lambda i, j, k: (j, k)) if trans_b
              else _wspec(lead, (tk, tn), lambda i, j, k: (k, j)))
    dtypes = [out_dtype] + ([also] if also is not None else [])
    out = pl.pallas_call(
        functools.partial(_mm_kernel, nk=nk, trans_b=trans_b),
        grid=(m // tm, pl.cdiv(n, tn), nk),
        in_specs=[pl.BlockSpec((tm, tk), lambda i, j, k: (i, k)), b_spec],
        out_specs=[pl.BlockSpec((tm, tn), lambda i, j, k: (i, j)) for _ in dtypes],
        out_shape=[jax.ShapeDtypeStruct((m, n), dt) for dt in dtypes],
        scratch_shapes=[pltpu.VMEM(acc_shape, F32)],
        compiler_params=_cparams(("parallel", "parallel", "arbitrary")),
        name=name,
    )(a, b)
    return out if also is not None else out[0]


def _prep_kernel(c_ref, k_ref, gq_ref, gkv_ref, gk_ref, bk_ref, cq_o, ckv_o, kidx_o):
    c = c_ref[...]
    cq = c[:, :Q_LORA]
    ckv = c[:, Q_LORA:]
    cq = cq * lax.rsqrt(jnp.mean(cq * cq, axis=-1, keepdims=True) + LN_EPS) * gq_ref[...]
    ckv = ckv * lax.rsqrt(jnp.mean(ckv * ckv, axis=-1, keepdims=True) + LN_EPS) * gkv_ref[...]
    k = k_ref[...]
    mu = jnp.mean(k, axis=-1, keepdims=True)
    kc = k - mu
    var = jnp.mean(kc * kc, axis=-1, keepdims=True)
    kn = kc * lax.rsqrt(var + LN_EPS) * gk_ref[...] + bk_ref[...]
    cq_o[...] = cq.astype(BF)
    ckv_o[...] = ckv.astype(BF)
    kidx_o[...] = kn.astype(BF)


def _prep(z, g_cq, g_ckv, g_kidx, b_kidx, *, tm=512):
    m = z.shape[0]
    wc = Q_LORA + KV_LORA
    row = lambda a: a.reshape(1, -1)
    full = lambda n: pl.BlockSpec((1, n), lambda i: (0, 0))
    return pl.pallas_call(
        _prep_kernel,
        grid=(m // tm,),
        in_specs=[pl.BlockSpec((tm, wc), lambda i: (i, OFF_CQ // wc)),
                  pl.BlockSpec((tm, D_IDX), lambda i: (i, OFF_KIDX // D_IDX)),
                  full(Q_LORA), full(KV_LORA), full(D_IDX), full(D_IDX)],
        out_specs=[pl.BlockSpec((tm, Q_LORA), lambda i: (i, 0)),
                   pl.BlockSpec((tm, KV_LORA), lambda i: (i, 0)),
                   pl.BlockSpec((tm, D_IDX), lambda i: (i, 0))],
        out_shape=[jax.ShapeDtypeStruct((m, Q_LORA), BF),
                   jax.ShapeDtypeStruct((m, KV_LORA), BF),
                   jax.ShapeDtypeStruct((m, D_IDX), BF)],
        compiler_params=_cparams(("parallel",)),
        name="prep_norms",
    )(z, z, row(g_cq), row(g_ckv), row(g_kidx), row(b_kidx))


def _to_key(x):
    b = lax.bitcast_convert_type(x, jnp.int32)
    return b ^ ((b >> 31) & jnp.int32(0x7FFFFFFF))


def _kth_largest_key(count_ge, k, shape):
    kf = jnp.float32(k)
    zero = jnp.zeros(shape, jnp.int32)
    t0 = jnp.where(count_ge(zero) >= kf, zero, jnp.full(shape, INT_MIN, jnp.int32))

    def body(it, t):
        cand = t + jnp.left_shift(jnp.int32(1), jnp.int32(30) - it)
        return jnp.where(count_ge(cand) >= kf, cand, t)

    t = lax.fori_loop(0, 31, body, t0)
    return jnp.maximum(t, jnp.int32(INT_MIN + 1))


TIE_POS_BITS = 14
FOLD_ROWS = 64


def _fold_rows(x):
    n, w = x.shape
    return x.reshape(n // FOLD_ROWS, FOLD_ROWS, w).sum(axis=0)


MASKED = 2.0 * NEG


def _online_update_t(m_ref, l_ref, acc_ref, slots, logits, masks, v_ts):
    scaled = []
    for sl, s, mask in zip(slots, logits, masks):
        s = jnp.where(mask, s, MASKED)
        m_old = m_ref[sl]
        m_new = jnp.maximum(m_old, jnp.max(s, axis=0, keepdims=True))
        alpha = jnp.exp(m_old - m_new)
        p = jnp.exp(s - m_new)
        l_ref[sl] = alpha * l_ref[sl] + jnp.sum(p, axis=0, keepdims=True)
        m_ref[sl] = m_new
        scaled.append((alpha, p.astype(BF)))
    for sl, (alpha, p), v_t in zip(slots, scaled, v_ts):
        acc_ref[sl] = alpha * acc_ref[sl] + _dot(v_t, p)


def _dsa_kernel(brow_ref, qi_ref, q_ref, w_ref, kidx_ref, ckv_ref, kvtf_ref, kvtn_ref, wuk_ref, wuvt_ref, neart_ref,
                o_ref, keyf_ref, keyn_ref, qstk_ref, qlstk_ref, m_ref, l_ref, acc_ref, *, topk):
    i = pl.program_id(1)
    tq = TQ
    npair = H_A // 2
    far_end = jnp.maximum(i - 1, 0) * tq
    nfar = (far_end + TK - 1) // TK
    row0 = pl.multiple_of(i * tq, tq)

    for h in range(H_IDX):
        qstk_ref[h * tq:(h + 1) * tq, :] = qi_ref[:, h * D_IDX:(h + 1) * D_IDX]
    w_t = (w_ref[...] * (D_IDX ** -0.5 * H_IDX ** -0.5)).T

    def scores_t(krows):
        acc = jnp.zeros((krows.shape[0], tq), F32)
        for hp in range(H_IDX // 2):
            s = _dot_t(krows, qstk_ref[2 * hp * tq:(2 * hp + 2) * tq, :])
            acc = (acc + jnp.maximum(s[:, :tq], 0.0) * w_t[2 * hp:2 * hp + 1, :]
                   + jnp.maximum(s[:, tq:], 0.0) * w_t[2 * hp + 1:2 * hp + 2, :])
        return acc

    rrow = lax.broadcasted_iota(jnp.int32, (2 * tq, tq), 0)
    ccol = lax.broadcasted_iota(jnp.int32, (2 * tq, tq), 1)
    valid_n = (tq + ccol - rrow >= 0) & ((i - 1) * tq + rrow >= 0)
    keyn_ref[...] = jnp.where(valid_n, _to_key(scores_t(kidx_ref[pl.ds(row0, 2 * tq), :])), jnp.int32(INT_MIN))

    def far_scores(kt, c):
        r0 = pl.multiple_of(tq + kt * TK, tq)
        s = scores_t(kidx_ref[pl.ds(r0, TK), :])
        pos = kt * TK + lax.broadcasted_iota(jnp.int32, (TK, tq), 0)
        keyf_ref[kt] = jnp.where(pos < far_end, _to_key(s), jnp.int32(INT_MIN))
        return c

    lax.fori_loop(0, nfar, far_scores, 0)

    def count_ge(cand):
        part = _fold_rows(jnp.where(keyn_ref[...] >= cand, 1.0, 0.0))

        def body(kt, a):
            return a + _fold_rows(jnp.where(keyf_ref[kt] >= cand, 1.0, 0.0))

        part = lax.fori_loop(0, nfar, body, part)
        return jnp.sum(part, axis=0, keepdims=True)

    thr = _kth_largest_key(count_ge, topk, (1, tq))

    def count2(pred):
        upos_n = lax.broadcasted_iota(jnp.int32, (2 * tq, tq), 0) + (i - 1) * tq + tq
        part = _fold_rows(jnp.where(pred(keyn_ref[...], upos_n), 1.0, 0.0))

        def body(kt, a):
            upos = kt * TK + tq + lax.broadcasted_iota(jnp.int32, (TK, tq), 0)
            return a + _fold_rows(jnp.where(pred(keyf_ref[kt], upos), 1.0, 0.0))

        return jnp.sum(lax.fori_loop(0, nfar, body, part), axis=0, keepdims=True)

    n_gt = count2(lambda kk, up: kk > thr)
    n_eq = count2(lambda kk, up: kk == thr)
    need = jnp.float32(topk) - n_gt
    surplus = n_eq > need

    @pl.when(jnp.max(jnp.where(surplus, 1.0, 0.0)) > 0.0)
    def _():
        def body(it, r):
            cand = r + jnp.left_shift(jnp.int32(1), jnp.int32(TIE_POS_BITS - 1) - it)
            below = count2(lambda kk, up: (kk == thr) & (up < cand))
            return jnp.where(below < need, cand, r)

        last = lax.fori_loop(0, TIE_POS_BITS, body, jnp.zeros((1, tq), jnp.int32))
        last = jnp.where(surplus, last, jnp.int32(2 ** TIE_POS_BITS))
        upos_n = lax.broadcasted_iota(jnp.int32, (2 * tq, tq), 0) + (i - 1) * tq + tq
        kn = keyn_ref[...]
        keyn_ref[...] = jnp.where((kn == thr) & (upos_n > last), jnp.int32(INT_MIN), kn)

        def demote(kt, c):
            upos = kt * TK + tq + lax.broadcasted_iota(jnp.int32, (TK, tq), 0)
            kf = keyf_ref[kt]
            keyf_ref[kt] = jnp.where((kf == thr) & (upos > last), jnp.int32(INT_MIN), kf)
            return c

        lax.fori_loop(0, nfar, demote, 0)

    for h in range(H_A):
        ql = _dot(q_ref[:, h * HEAD_DIM:(h + 1) * HEAD_DIM], wuk_ref[h]) * (HEAD_DIM ** -0.5)
        qlstk_ref[h * tq:(h + 1) * tq, :] = ql.astype(BF)

    for hp in range(npair):
        m_ref[hp] = jnp.full((1, 2 * tq), NEG, F32)
        l_ref[hp] = jnp.zeros((1, 2 * tq), F32)
        acc_ref[hp] = jnp.zeros((KV_LORA, 2 * tq), F32)

    pairs = range(npair)

    def update(kv, kv_t, bias, sel):
        sel2 = jnp.concatenate([sel, sel], axis=1)
        logits = [_dot_t(kv, qlstk_ref[2 * hp * tq:(2 * hp + 2) * tq, :]) + bias(hp) for hp in pairs]
        _online_update_t(m_ref, l_ref, acc_ref, pairs, logits, [sel2] * npair, [kv_t] * npair)

    update(ckv_ref[pl.ds(row0, 2 * tq), :], jnp.concatenate([kvtn_ref[i], kvtn_ref[i + 1]], axis=1),
           lambda hp: neart_ref[hp], keyn_ref[...] >= thr)

    def far_attn(kt, c):
        r0 = pl.multiple_of(tq + kt * TK, tq)
        update(ckv_ref[pl.ds(r0, TK), :], kvtf_ref[kt], lambda hp: brow_ref[hp], keyf_ref[kt] >= thr)
        return c

    lax.fori_loop(0, nfar, far_attn, 0)

    for hp in range(npair):
        l = l_ref[hp]
        o_lat_t = (acc_ref[hp] * jnp.where(l > 0.0, 1.0 / l, 0.0)).astype(BF)
        for hh in range(2):
            h = 2 * hp + hh
            o_t = _dot(wuvt_ref[h], o_lat_t[:, hh * tq:(hh + 1) * tq])
            o_ref[:, h * HEAD_DIM:(h + 1) * HEAD_DIM] = o_t.T.astype(o_ref.dtype)


def _dsa(qq, z, kidx_n, ckv_n, w_uk, w_uv, near_a, far_a, *, seq):
    bsz = qq.shape[0]
    topk = min(TOPK_MAX, seq // 4)
    nqi = H_IDX * D_IDX
    npair = H_A // 2
    kidx_pad = _pad_front(kidx_n, TQ)
    ckv_pad = _pad_front(ckv_n, TQ)
    kvt_far = jnp.transpose(ckv_n.reshape(bsz, seq // TK, TK, KV_LORA), (0, 1, 3, 2))
    kvt_near = jnp.transpose(ckv_pad.reshape(bsz, seq // TQ + 1, TQ, KV_LORA), (0, 1, 3, 2))
    wuk_t = jnp.transpose(w_uk, (1, 2, 0)).astype(BF)
    wuv_t = jnp.transpose(w_uv, (1, 2, 0)).astype(BF)
    brow = jnp.repeat(far_a.reshape(npair, 2), TQ, axis=1).reshape(npair, 1, 2 * TQ)
    near_t = jnp.transpose(jnp.transpose(near_a, (0, 2, 1)).reshape(npair, 2, 2 * TQ, TQ),
                           (0, 2, 1, 3)).reshape(npair, 2 * TQ, 2 * TQ)
    whole = lambda shape: pl.BlockSpec(shape, lambda b, i: (0,) * len(shape))
    perb = lambda shape: pl.BlockSpec((None,) + shape, lambda b, i: (b,) + (0,) * len(shape))
    return pl.pallas_call(
        functools.partial(_dsa_kernel, topk=topk),
        grid=(bsz, seq // TQ),
        in_specs=[whole((npair, 1, 2 * TQ)),
                  pl.BlockSpec((None, TQ, nqi), lambda b, i: (b, i, 0)),
                  pl.BlockSpec((None, TQ, H_A * HEAD_DIM), lambda b, i: (b, i, nqi // (H_A * HEAD_DIM))),
                  pl.BlockSpec((None, TQ, LANES), lambda b, i: (b, i, OFF_WIDX // LANES)),
                  perb((seq + TQ, D_IDX)), perb((seq + TQ, KV_LORA)),
                  perb((seq // TK, KV_LORA, TK)), perb((seq // TQ + 1, KV_LORA, TQ)),
                  whole((H_A, HEAD_DIM, KV_LORA)), whole((H_A, HEAD_DIM, KV_LORA)),
                  whole((npair, 2 * TQ, 2 * TQ))],
        out_specs=pl.BlockSpec((None, TQ, H_A * HEAD_DIM), lambda b, i: (b, i, 0)),
        out_shape=jax.ShapeDtypeStruct((bsz, seq, H_A * HEAD_DIM), BF),
        scratch_shapes=[pltpu.VMEM((seq // TK, TK, TQ), jnp.int32),
                        pltpu.VMEM((2 * TQ, TQ), jnp.int32),
                        pltpu.VMEM((H_IDX * TQ, D_IDX), BF),
                        pltpu.VMEM((H_A * TQ, KV_LORA), BF),
                        pltpu.VMEM((npair, 1, 2 * TQ), F32),
                        pltpu.VMEM((npair, 1, 2 * TQ), F32),
                        pltpu.VMEM((npair, KV_LORA, 2 * TQ), F32)],
        compiler_params=_cparams(("parallel", "parallel")),
        name="dsa_mixer",
    )(brow, qq, qq, z, kidx_pad, ckv_pad, kvt_far, kvt_near, wuk_t, wuv_t, near_t)


def _cmp_kernel(x_ref, pe_ref, w1_ref, w2_ref, o_ref, xp_ref, *, seq, ncp):
    xp_ref[0:seq, :] = x_ref[...]
    xp_ref[seq:seq + CMP_STRIDE, :] = jnp.zeros((CMP_STRIDE, HEAD_DIM), F32)
    acc = jnp.zeros((ncp, CMP_HIDDEN), F32)
    for l in range(CMP_BLOCK):
        rows = xp_ref[pl.ds(l, ncp, stride=CMP_STRIDE), :]
        blk = rows + pe_ref[l:l + 1, :]
        acc = acc + _dot(blk.astype(BF), w1_ref[l].astype(BF))
    hdn = jax.nn.gelu(acc)
    o_ref[...] = _dot(hdn.astype(BF), w2_ref[...].astype(BF)).astype(o_ref.dtype)


def _compress(z, cmp_pe, cmp_w1, cmp_w2, *, seq):
    bsz = z.shape[0]
    ncp = seq // CMP_STRIDE
    cmp_pe, lead_pe = _layered(cmp_pe)
    cmp_w1, lead_w1 = _layered(cmp_w1)
    cmp_w2, lead_w2 = _layered(cmp_w2)
    return pl.pallas_call(
        functools.partial(_cmp_kernel, seq=seq, ncp=ncp),
        grid=(bsz, 2, G_B),
        in_specs=[pl.BlockSpec((None, seq, HEAD_DIM), lambda b, c, g: (b, 0, OFF_KVB // HEAD_DIM + c * G_B + g)),
                  _wspec(lead_pe, (None, CMP_BLOCK, HEAD_DIM), lambda b, c, g: (c, 0, 0)),
                  _wspec(lead_w1, (None, CMP_BLOCK, HEAD_DIM, CMP_HIDDEN), lambda b, c, g: (c, 0, 0, 0)),
                  _wspec(lead_w2, (None, CMP_HIDDEN, HEAD_DIM), lambda b, c, g: (c, 0, 0))],
        out_specs=pl.BlockSpec((None, None, None, ncp, HEAD_DIM), lambda b, c, g: (b, c, g, 0, 0)),
        out_shape=jax.ShapeDtypeStruct((bsz, 2, G_B, ncp, HEAD_DIM), BF),
        scratch_shapes=[pltpu.VMEM((seq + CMP_STRIDE, HEAD_DIM), F32)],
        compiler_params=_cparams(("parallel", "parallel", "parallel")),
        name="nsa_compress",
    )(z, cmp_pe, cmp_w1, cmp_w2)


def _nsa_kernel(brow_ref, q_ref, gb_ref, kc_ref, vct_ref, ksl_ref, vstf_ref, vstn_ref, kw_ref, vwt_ref, et_ref, ovt_ref,
                nearts_ref, neartw_ref, o_ref, qstk_ref, imp_ref, mselt_ref, m_ref, l_ref, acc_ref, ot_ref, *, seq, npick):
    i = pl.program_id(1)
    tq = TQ
    hpg = H_B // G_B
    gl = hpg * tq
    ncp = seq // CMP_STRIDE
    nblk = SEL_BLOCK
    far_end = jnp.maximum(i - 1, 0) * tq
    nfar = (far_end + TK - 1) // TK
    row0 = pl.multiple_of(i * tq, tq)
    tile4 = lambda x: jnp.concatenate([x] * hpg, axis=1)
    tlane = i * tq + lax.broadcasted_iota(jnp.int32, (1, tq), 1)

    for h in range(H_B):
        qstk_ref[h * tq:(h + 1) * tq, :] = (q_ref[:, h * HEAD_DIM:(h + 1) * HEAD_DIM] * (HEAD_DIM ** -0.5)).astype(BF)
    sig_t = jax.nn.sigmoid(gb_ref[...]).T

    def gate_row(g, c):
        return jnp.concatenate([sig_t[H_IDX + 3 * (g * hpg + j) + c:H_IDX + 3 * (g * hpg + j) + c + 1, :]
                                for j in range(hpg)], axis=1)

    qgrp = lambda g: qstk_ref[g * gl:(g + 1) * gl, :]
    grows = lambda g: slice(g * HEAD_DIM, (g + 1) * HEAD_DIM)

    nrow = lax.broadcasted_iota(jnp.int32, (ncp, tq), 0)
    cmp_mask = tile4(nrow * CMP_STRIDE + (CMP_BLOCK - 1) <= tlane)
    anyc = tile4(jnp.where(tlane >= CMP_BLOCK - 1, 1.0, 0.0))
    brow = lax.broadcasted_iota(jnp.int32, (nblk, tq), 0)
    cur = tlane // SEL_BLOCK
    forced = (brow == 0) | (brow == cur) | (brow == cur - 1)
    sub = lax.broadcasted_iota(jnp.int32, (8, tq), 0)
    for g in range(G_B):
        s = jnp.where(cmp_mask, _dot_t(kc_ref[g], qgrp(g)), NEG)
        e = jnp.exp(s - jnp.max(s, axis=0, keepdims=True))
        p = e * (anyc / jnp.sum(e, axis=0, keepdims=True))
        ot_ref[g] = gate_row(g, 0) * _dot(vct_ref[g], p.astype(BF))
        psum = p[:, :tq]
        for j in range(1, hpg):
            psum = psum + p[:, j * tq:(j + 1) * tq]
        hi = psum.astype(BF)
        lo = (psum - hi.astype(F32)).astype(BF)
        imp = (_dot(ovt_ref[...], hi) + _dot(ovt_ref[...], lo))[:nblk]
        imp = jnp.where(brow > cur, NEG, jnp.where(forced, FORCE_SCORE, imp))
        imp_ref[...] = imp
        vals = [imp[8 * r:8 * r + 8] for r in range(nblk // 8)]
        rank = [jnp.zeros((8, tq), F32) for _ in vals]
        for mp in range(nblk):
            other = jnp.broadcast_to(imp_ref[mp:mp + 1, :], (8, tq))
            for r, v in enumerate(vals):
                if 8 * r > mp:
                    beats = other >= v
                elif 8 * r + 7 < mp:
                    beats = other > v
                else:
                    beats = (other > v) | ((other == v) & (sub + 8 * r > mp))
                rank[r] = rank[r] + jnp.where(beats, 1.0, 0.0)
        sel = jnp.concatenate([jnp.where(rk < npick, 1.0, 0.0) for rk in rank], axis=0)
        mselt_ref[g] = jnp.concatenate([sel, jnp.zeros((LANES - nblk, tq), F32)], axis=0).astype(BF)

    rrow = lax.broadcasted_iota(jnp.int32, (2 * tq, tq), 0)
    ccol = lax.broadcasted_iota(jnp.int32, (2 * tq, tq), 1)
    causal_n = tq + ccol - rrow >= 0
    etn = et_ref[pl.ds(row0, 2 * tq), :]
    groups = range(G_B)
    for g in groups:
        m_ref[g] = jnp.full((1, gl), NEG, F32)
        l_ref[g] = jnp.zeros((1, gl), F32)
        acc_ref[g] = jnp.zeros((HEAD_DIM, gl), F32)
    _online_update_t(
        m_ref, l_ref, acc_ref, groups,
        [_dot_t(ksl_ref[pl.ds(row0, 2 * tq), grows(g)], qgrp(g)) + nearts_ref[g] for g in groups],
        [tile4(causal_n & (_dot(etn, mselt_ref[g]) > 0.5)) for g in groups],
        [jnp.concatenate([vstn_ref[i, grows(g), :], vstn_ref[i + 1, grows(g), :]], axis=1) for g in groups])

    def far_attn(kt, c):
        r0 = pl.multiple_of(tq + kt * TK, tq)
        ett = et_ref[pl.ds(r0, TK), :]
        infar = kt * TK + lax.broadcasted_iota(jnp.int32, (TK, tq), 0) < far_end
        _online_update_t(
            m_ref, l_ref, acc_ref, groups,
            [_dot_t(ksl_ref[pl.ds(r0, TK), grows(g)], qgrp(g)) + brow_ref[g] for g in groups],
            [tile4(infar & (_dot(ett, mselt_ref[g]) > 0.5)) for g in groups],
            [vstf_ref[kt, grows(g), :] for g in groups])
        return c

    lax.fori_loop(0, nfar, far_attn, 0)
    for g in groups:
        l = l_ref[g]
        ot_ref[g] += gate_row(g, 1) * (acc_ref[g] * jnp.where(l > 0.0, 1.0 / l, 0.0))

    span = WIN_B + tq
    rw = lax.broadcasted_iota(jnp.int32, (span, tq), 0)
    cw = lax.broadcasted_iota(jnp.int32, (span, tq), 1)
    distw = WIN_B + cw - rw
    mask_w = tile4((distw >= 0) & (distw < WIN_B) & (i * tq - WIN_B + rw >= 0))
    for g in groups:
        s = jnp.where(mask_w, _dot_t(kw_ref[pl.ds(row0, span), grows(g)], qgrp(g)) + neartw_ref[g], NEG)
        e = jnp.exp(s - jnp.max(s, axis=0, keepdims=True))
        p = e / jnp.sum(e, axis=0, keepdims=True)
        vw_t = jnp.concatenate([vwt_ref[i + k, grows(g), :] for k in range(span // tq)], axis=1)
        ot_ref[g] += gate_row(g, 2) * _dot(vw_t, p.astype(BF))

    for h in range(H_B):
        g, j = divmod(h, hpg)
        o_ref[:, h * HEAD_DIM:(h + 1) * HEAD_DIM] = ot_ref[g][:, j * tq:(j + 1) * tq].T.astype(o_ref.dtype)


def _tiles_t(a, rows):
    bsz, n, w = a.shape
    return jnp.transpose(a.reshape(bsz, n // rows, rows, w), (0, 1, 3, 2))


def _lanes_by_group(near, groups):
    h, tq, w = near.shape
    t = jnp.transpose(near, (0, 2, 1)).reshape(groups, h // groups, w, tq)
    return jnp.transpose(t, (0, 2, 1, 3)).reshape(groups, w, (h // groups) * tq)


def _nsa(z, kcv, k_slc, v_slc, k_win, v_win, et_pad, overlap_t, near_s, near_w, far_s, *, seq):
    bsz = z.shape[0]
    ncp = seq // CMP_STRIDE
    npick = min(N_SEL, seq // SEL_BLOCK)
    assert seq // SEL_BLOCK <= SEL_BLOCK
    gw = G_B * HEAD_DIM
    hpg = H_B // G_B
    gl = hpg * TQ
    kc = kcv[:, 0]
    vc_t = jnp.transpose(kcv[:, 1], (0, 1, 3, 2))
    brow = jnp.repeat(far_s.reshape(G_B, hpg), TQ, axis=1).reshape(G_B, 1, gl)
    whole = lambda shape: pl.BlockSpec(shape, lambda b, i: (0,) * len(shape))
    perb = lambda shape: pl.BlockSpec((None,) + shape, lambda b, i: (b,) + (0,) * len(shape))
    return pl.pallas_call(
        functools.partial(_nsa_kernel, seq=seq, npick=npick),
        grid=(bsz, seq // TQ),
        in_specs=[whole((G_B, 1, gl)),
                  pl.BlockSpec((None, TQ, H_B * HEAD_DIM), lambda b, i: (b, i, OFF_QB // (H_B * HEAD_DIM))),
                  pl.BlockSpec((None, TQ, LANES), lambda b, i: (b, i, OFF_WIDX // LANES)),
                  perb((G_B, ncp, HEAD_DIM)), perb((G_B, HEAD_DIM, ncp)),
                  perb((seq + TQ, gw)), perb((seq // TK, gw, TK)), perb((seq // TQ + 1, gw, TQ)),
                  perb((seq + WIN_B, gw)), perb(((seq + WIN_B) // TQ, gw, TQ)),
                  whole((seq + TQ, LANES)), whole((LANES, ncp)),
                  whole((G_B, 2 * TQ, gl)), whole((G_B, WIN_B + TQ, gl))],
        out_specs=pl.BlockSpec((None, TQ, H_B * HEAD_DIM), lambda b, i: (b, i, 0)),
        out_shape=jax.ShapeDtypeStruct((bsz, seq, H_B * HEAD_DIM), BF),
        scratch_shapes=[pltpu.VMEM((H_B * TQ, HEAD_DIM), BF),
                        pltpu.VMEM((SEL_BLOCK, TQ), F32),
                        pltpu.VMEM((G_B, LANES, TQ), BF),
                        pltpu.VMEM((G_B, 1, gl), F32),
                        pltpu.VMEM((G_B, 1, gl), F32),
                        pltpu.VMEM((G_B, HEAD_DIM, gl), F32),
                        pltpu.VMEM((G_B, HEAD_DIM, gl), F32)],
        compiler_params=_cparams(("parallel", "parallel")),
        name="nsa_mixer",
    )(brow, z, z, kc, vc_t, _pad_front(k_slc, TQ), _tiles_t(v_slc, TK), _tiles_t(_pad_front(v_slc, TQ), TQ),
      _pad_front(k_win, WIN_B), _tiles_t(_pad_front(v_win, WIN_B), TQ), et_pad, overlap_t,
      _lanes_by_group(near_s, G_B), _lanes_by_group(near_w, G_B))


def _swa_kernel(sink_ref, q_ref, k_ref, v_ref, near_ref, o_ref):
    i = pl.program_id(1)
    tq = TQ
    hpg = H_C // KV_C
    row0 = pl.multiple_of(i * tq, tq)
    qrow = lax.broadcasted_iota(jnp.int32, (tq, 2 * tq), 0)
    jcol = lax.broadcasted_iota(jnp.int32, (tq, 2 * tq), 1)
    dist = WIN_C + qrow - jcol
    mask = (dist >= 0) & (dist < WIN_C) & (i * tq - WIN_C + jcol >= 0)
    kk = k_ref[pl.ds(row0, 2 * tq), :]
    vv = v_ref[pl.ds(row0, 2 * tq), :]
    for h in range(H_C):
        g = h // hpg
        kg = kk[:, g * HEAD_DIM_C:(g + 1) * HEAD_DIM_C]
        vg = vv[:, g * HEAD_DIM_C:(g + 1) * HEAD_DIM_C]
        qh = (q_ref[:, h * HEAD_DIM_C:(h + 1) * HEAD_DIM_C] * (HEAD_DIM_C ** -0.5)).astype(BF)
        s = jnp.where(mask, _dot_t(qh, kg) + near_ref[h], NEG)
        sink = sink_ref[h]
        m = jnp.maximum(jnp.max(s, axis=-1, keepdims=True), sink)
        e = jnp.exp(s - m)
        p = e / (jnp.sum(e, axis=-1, keepdims=True) + jnp.exp(sink - m))
        o_ref[:, h * HEAD_DIM_C:(h + 1) * HEAD_DIM_C] = _dot(p.astype(BF), vg).astype(o_ref.dtype)


def _swa(z, kc_pad, vc_pad, near_c, sinks, *, seq):
    bsz = z.shape[0]
    kvw = KV_C * HEAD_DIM_C
    return pl.pallas_call(
        _swa_kernel,
        grid=(bsz, seq // TQ),
        in_specs=[pl.BlockSpec(memory_space=pltpu.SMEM),
                  pl.BlockSpec((None, TQ, BRANCH_W), lambda b, i: (b, i, OFF_QC // BRANCH_W)),
                  pl.BlockSpec((None, seq + WIN_C, kvw), lambda b, i: (b, 0, 0)),
                  pl.BlockSpec((None, seq + WIN_C, kvw), lambda b, i: (b, 0, 0)),
                  pl.BlockSpec((H_C, TQ, 2 * TQ), lambda b, i: (0, 0, 0))],
        out_specs=pl.BlockSpec((None, TQ, BRANCH_W), lambda b, i: (b, i, 0)),
        out_shape=jax.ShapeDtypeStruct((bsz, seq, BRANCH_W), BF),
        compiler_params=_cparams(("parallel", "parallel")),
        name="swa_mixer",
    )(sinks, z, kc_pad, vc_pad, near_c)


def _merge_kernel(x_ref, wg_ref, o_ref, wbr_ref, out_ref, acc_ref):
    br = pl.program_id(2)
    gate = jax.nn.sigmoid(_dot_t(x_ref[...], wg_ref[...].astype(BF)))
    val = _dot(o_ref[...], wbr_ref[...].astype(BF))

    @pl.when(br == 0)
    def _():
        acc_ref[...] = gate * val

    @pl.when(br > 0)
    def _():
        acc_ref[...] += gate * val

    @pl.when(br == N_BRANCH - 1)
    def _():
        out_ref[...] = acc_ref[...].astype(out_ref.dtype)


def _merge(xb, w_in_t, li, gate_row0, o_all, w_br, *, tm, tn):
    m, d = xb.shape
    nj = d // tn
    w_br, lead = _layered(w_br)
    depth, n_in, _ = w_in_t.shape
    row0 = li * n_in + gate_row0
    return pl.pallas_call(
        _merge_kernel,
        grid=(m // tm, nj, N_BRANCH),
        in_specs=[pl.BlockSpec((tm, d), lambda i, j, r: (i, 0)),
                  pl.BlockSpec((pl.Element(tn), pl.Element(d)),
                               lambda i, j, r: (pl.multiple_of(row0 + r * d + j * tn, 8), 0)),
                  pl.BlockSpec((None, tm, BRANCH_W), lambda i, j, r: (r, i, 0)),
                  _wspec(lead, (None, BRANCH_W, tn), lambda i, j, r: (r, 0, j))],
        out_specs=pl.BlockSpec((tm, tn), lambda i, j, r: (i, j)),
        out_shape=jax.ShapeDtypeStruct((m, d), BF),
        scratch_shapes=[pltpu.VMEM((tm, tn), F32)],
        compiler_params=_cparams(("parallel", "parallel", "arbitrary")),
        name="merge_gates",
    )(xb, w_in_t.reshape(depth * n_in, d), o_all, w_br)


def _resln_kernel(*refs, nj, tn, alpha, gated):
    if gated:
        (a_ref, w_ref, res_ref, g_ref, b_ref, bias_ref, p_ref, wp_ref, add_ref,
         of_ref, ob_ref, y_ref, mu_ref, rs_ref) = refs
    else:
        a_ref, w_ref, res_ref, g_ref, b_ref, of_ref, ob_ref, y_ref, mu_ref, rs_ref = refs
    j = pl.program_id(1)

    @pl.when(j < nj)
    def _():
        y = _dot(a_ref[...], w_ref[...].astype(BF))
        if gated:
            y = jax.nn.sigmoid(y + bias_ref[...]) * _dot(p_ref[...], wp_ref[...].astype(BF)) + add_ref[...]
        y_ref[j] = alpha * res_ref[...] + y

    @pl.when(j == nj - 1)
    def _():
        tm = y_ref.shape[1]
        d = nj * tn
        tot = jnp.zeros((tm, 1), F32)
        for c in range(nj):
            tot = tot + jnp.sum(y_ref[c], axis=-1, keepdims=True)
        mu = tot / d
        sq = jnp.zeros((tm, 1), F32)
        for c in range(nj):
            dc = y_ref[c] - mu
            sq = sq + jnp.sum(dc * dc, axis=-1, keepdims=True)
        mu_ref[...] = mu
        rs_ref[...] = lax.rsqrt(sq / d + LN_EPS)

    @pl.when(j >= nj)
    def _():
        o = (y_ref[j - nj] - mu_ref[...]) * rs_ref[...] * g_ref[...] + b_ref[...]
        of_ref[...] = o
        ob_ref[...] = o.astype(BF)


def _resln(a, w, res, g, b, *, alpha, tm, tn, gated=None, name="res_ln"):
    m, kdim = a.shape
    w, lead = _layered(w)
    d = w.shape[-1]
    nj = d // tn
    mm_col = lambda i, j: (0, jnp.minimum(j, nj - 1))
    mm_blk = lambda i, j: (i, jnp.minimum(j, nj - 1))
    out_col = lambda i, j: (0, jnp.maximum(j - nj, 0))
    out_blk = lambda i, j: (i, jnp.maximum(j - nj, 0))
    in_specs = [pl.BlockSpec((tm, kdim), lambda i, j: (i, 0)),
                _wspec(lead, (kdim, tn), mm_col),
                pl.BlockSpec((tm, tn), mm_blk),
                pl.BlockSpec((1, tn), out_col),
                pl.BlockSpec((1, tn), out_col)]
    args = [a, w, res, g.reshape(1, d), b.reshape(1, d)]
    if gated is not None:
        bias, p, wp, add = gated
        wp, lead_p = _layered(wp)
        dp = p.shape[1]
        in_specs += [pl.BlockSpec((1, tn), mm_col),
                     pl.BlockSpec((tm, dp), lambda i, j: (i, 0)),
                     _wspec(lead_p, (dp, tn), mm_col),
                     pl.BlockSpec((tm, tn), mm_blk)]
        args += [bias.reshape(1, d), p, wp, add]
    return pl.pallas_call(
        functools.partial(_resln_kernel, nj=nj, tn=tn, alpha=alpha, gated=gated is not None),
        grid=(m // tm, 2 * nj),
        in_specs=in_specs,
        out_specs=[pl.BlockSpec((tm, tn), out_blk), pl.BlockSpec((tm, tn), out_blk)],
        out_shape=[jax.ShapeDtypeStruct((m, d), F32), jax.ShapeDtypeStruct((m, d), BF)],
        scratch_shapes=[pltpu.VMEM((nj, tm, tn), F32), pltpu.VMEM((tm, 1), F32), pltpu.VMEM((tm, 1), F32)],
        compiler_params=_cparams(("parallel", "arbitrary")),
        name=name,
    )(*args)


HALO = 8


def _ffn_up_kernel(x_ref, wg_ref, wv_ref, cwg_ref, cwv_ref, cbg_ref, cbv_ref, a_ref, hg_ref, hv_ref, *, tm, ch, tiles_per_seq):
    i = pl.program_id(1)
    first = (i % tiles_per_seq) == 0

    @pl.when(first)
    def _():
        hg_ref[0:HALO, :] = jnp.zeros((HALO, hg_ref.shape[1]), F32)
        hv_ref[0:HALO, :] = jnp.zeros((HALO, hv_ref.shape[1]), F32)

    @pl.when(jnp.logical_not(first))
    def _():
        hg_ref[0:HALO, :] = hg_ref[tm:tm + HALO, :]
        hv_ref[0:HALO, :] = hv_ref[tm:tm + HALO, :]

    wg = wg_ref[...].astype(BF)
    wv = wv_ref[...].astype(BF)

    def conv(h_ref, c, cw_ref, cb_ref):
        out = cb_ref[...]
        for k in range(CONV_W):
            off = HALO + c * ch - (CONV_W - 1) + k
            out = out + h_ref[off:off + ch, :] * cw_ref[k:k + 1, :]
        return out

    def up(c):
        xc = x_ref[c * ch:(c + 1) * ch, :]
        hg_ref[HALO + c * ch:HALO + (c + 1) * ch, :] = _dot(xc, wg)
        hv_ref[HALO + c * ch:HALO + (c + 1) * ch, :] = _dot(xc, wv)

    up(0)
    for c in range(tm // ch):
        if c + 1 < tm // ch:
            up(c + 1)
        a_ref[c * ch:(c + 1) * ch, :] = (jax.nn.gelu(conv(hg_ref, c, cwg_ref, cbg_ref))
                                         * conv(hv_ref, c, cwv_ref, cbv_ref)).astype(a_ref.dtype)


def _ffn_up(xb, w_up, conv_w, conv_b, *, seq, tm, tn, ch=256):
    m, d = xb.shape
    w_up, lead = _layered(w_up)
    conv_w, lead_c = _layered(conv_w)
    conv_b, lead_b = _layered(conv_b)
    dff = w_up.shape[-1] // 2
    nj = dff // tn
    assert dff % tn == 0 and seq % tm == 0 and tm % ch == 0
    cb = conv_b.reshape(conv_b.shape[:-1] + (1, 2 * dff))
    return pl.pallas_call(
        functools.partial(_ffn_up_kernel, tm=tm, ch=ch, tiles_per_seq=seq // tm),
        grid=(nj, m // tm),
        in_specs=[pl.BlockSpec((tm, d), lambda j, i: (i, 0)),
                  _wspec(lead, (d, tn), lambda j, i: (0, j)),
                  _wspec(lead, (d, tn), lambda j, i: (0, j + nj)),
                  _wspec(lead_c, (CONV_W, tn), lambda j, i: (0, j)),
                  _wspec(lead_c, (CONV_W, tn), lambda j, i: (0, j + nj)),
                  _wspec(lead_b, (1, tn), lambda j, i: (0, j)),
                  _wspec(lead_b, (1, tn), lambda j, i: (0, j + nj))],
        out_specs=pl.BlockSpec((tm, tn), lambda j, i: (i, j)),
        out_shape=jax.ShapeDtypeStruct((m, dff), BF),
        scratch_shapes=[pltpu.VMEM((tm + HALO, tn), F32), pltpu.VMEM((tm + HALO, tn), F32)],
        compiler_params=_cparams(("parallel", "arbitrary")),
        name="ffn_up_conv_geglu",
    )(xb, w_up, w_up, conv_w, conv_w, cb, cb)


def _t5_bucket(dist):
    dist = jnp.maximum(dist, 0)
    max_exact = N_BUCKETS // 2
    d = jnp.maximum(dist, 1).astype(F32)
    large = max_exact + (jnp.log(d / max_exact) / math.log(MAX_DISTANCE / max_exact)
                         * (N_BUCKETS - max_exact)).astype(jnp.int32)
    large = jnp.minimum(large, N_BUCKETS - 1)
    return jnp.where(dist < max_exact, dist, large)


def _near_bias(tab, offset, width):
    period = TQ + width
    idx = np.arange(period)
    k = np.where(idx < width, idx, idx - period)
    dist = np.clip(offset - k, 0, MAX_DISTANCE)
    u = tab[_t5_bucket(jnp.asarray(dist, jnp.int32))].T
    flat = jnp.tile(u, (1, TQ))[:, :TQ * (period - 1)]
    return flat.reshape(-1, TQ, period - 1)[:, :, :width]


def _pad_front(a, n):
    return jnp.pad(a, ((0, 0), (n, 0), (0, 0)))


def _in_splits():
    s_kidx = Q_LORA + KV_LORA
    s_widx = s_kidx + D_IDX
    s_qb = s_widx + H_IDX
    s_kvb = s_qb + H_B * HEAD_DIM
    s_gb = s_kvb + 6 * G_B * HEAD_DIM
    s_qc = s_gb + 3 * H_B
    s_kc = s_qc + H_C * HEAD_DIM_C
    s_vc = s_kc + KV_C * HEAD_DIM_C
    s_gm = s_vc + KV_C * HEAD_DIM_C
    segs = [(s_qb, s_kvb), (s_qc, s_kc), (0, s_kidx), (s_kidx, s_widx), (s_kvb, s_gb),
            (s_kc, s_vc), (s_vc, s_gm), (s_widx, s_qb), (s_gb, s_qc)]
    return segs, s_gm


def _small_weights_t(w_in_t, li):
    segs, s_gm = _in_splits()
    d = w_in_t.shape[-1]
    parts = [w_in_t[li, a:b] for a, b in segs] + [jnp.zeros((N_SMALL - s_gm, d), w_in_t.dtype)]
    return jnp.concatenate(parts, axis=0)


def _layer(li, x, xb, p_b, w_in_t, g_cq, g_ckv, g_kidx, b_kidx, w_qidx, w_uq, w_uk, w_uv,
           cmp_pe, cmp_w1, cmp_w2, sinks, w_br, w_o, ln1_g, ln1_b,
           w_up, conv_w, conv_b, w_down_b, w_pg, b_pg, w_pp, ln2_g, ln2_b, consts, *, bsz, seq, alpha):
    m, d = x.shape
    z, z_b = _mm(xb, _small_weights_t(w_in_t, li), tm=1024, tn=512, also=BF, trans_b=True, name="in_proj")
    cq_n, ckv_n, kidx_n = _prep(z, g_cq[li], g_ckv[li], g_kidx[li], b_kidx[li])
    w_qq = jnp.concatenate([w_qidx[li], w_uq[li]], axis=1).astype(BF)
    qq = _mm(cq_n, w_qq, tm=1024, tn=1280, out_dtype=BF, name="q_proj")

    z3 = z.reshape(bsz, seq, N_SMALL)
    b3 = lambda a: a.reshape(bsz, seq, a.shape[-1])
    zb3 = z_b.reshape(bsz, seq, N_SMALL)
    seg = lambda off, w: zb3[:, :, off:off + w]

    o_a = _dsa(b3(qq), z3, b3(kidx_n), b3(ckv_n), w_uk[li], w_uv[li], consts["near_a"], consts["far_a"], seq=seq)

    gw = G_B * HEAD_DIM
    kcv = _compress(z3, (cmp_pe, li), (cmp_w1, li), (cmp_w2, li), seq=seq)
    o_b = _nsa(z3, kcv, seg(OFF_KVB + 2 * gw, gw), seg(OFF_KVB + 3 * gw, gw), seg(OFF_KVB + 4 * gw, gw),
               seg(OFF_KVB + 5 * gw, gw), consts["et_pad"], consts["overlap_t"],
               consts["near_s"], consts["near_w"], consts["far_s"], seq=seq)

    kvw = KV_C * HEAD_DIM_C
    o_c = _swa(z3, _pad_front(seg(OFF_KC, kvw), WIN_C), _pad_front(seg(OFF_VC, kvw), WIN_C),
               consts["near_c"], sinks[li], seq=seq)

    o_all = jnp.stack([o_a.reshape(m, BRANCH_W), o_b.reshape(m, BRANCH_W), o_c.reshape(m, BRANCH_W)])
    merged = _merge(xb, w_in_t, li, _in_splits()[1], o_all, (w_br, li), tm=1024, tn=512)
    x1, x1b = _resln(merged, (w_o, li), x, ln1_g[li], ln1_b[li], alpha=alpha, tm=1024, tn=256, name="attn_out_ln1")

    a = _ffn_up(x1b, (w_up, li), (conv_w, li), (conv_b, li), seq=seq, tm=min(1024, seq), tn=256)
    dff = w_down_b.shape[1]
    ffn = _mm(a, (w_down_b, li), tm=1024, tn=512, tk=dff // 2 if (dff // 2) % LANES == 0 else dff, name="ffn_down")
    return _resln(x1b, (w_pg, li), x1, ln2_g[li], ln2_b[li], alpha=alpha, tm=1024, tn=256,
                  gated=(b_pg[li], p_b, (w_pp, li), ffn), name="ple_ln2")


def kernel(x, p, w_in, g_cq, g_ckv, g_kidx, b_kidx, w_qidx, w_uq, w_uk, w_uv, cmp_pe, cmp_w1, cmp_w2, sinks, w_br, w_o, ln1_g, ln1_b, w_up, conv_w, conv_b, w_down, w_pg, b_pg, w_pp, ln2_g, ln2_b, rel_bias):
    bsz, seq, d = x.shape
    depth = p.shape[0]
    alpha = (2 * depth) ** 0.25
    m = bsz * seq
    assert seq % TK == 0 and d % 512 == 0 and seq + TQ <= 2 ** TIE_POS_BITS

    key = np.arange(seq)[:, None]
    blk = np.arange(LANES)[None, :]
    et = (key // SEL_BLOCK == blk).astype(np.float32)
    et_pad = jnp.asarray(np.concatenate([np.zeros((TQ, LANES), np.float32), et]), BF)
    ncp = seq // CMP_STRIDE
    cs = np.arange(ncp)[:, None] * CMP_STRIDE
    bs = blk * SEL_BLOCK
    n_cmp = (seq - CMP_BLOCK) // CMP_STRIDE + 1
    ov = ((cs < bs + SEL_BLOCK) & (cs + CMP_BLOCK > bs) & (blk < seq // SEL_BLOCK)
          & (np.arange(ncp)[:, None] < n_cmp)).astype(np.float32)
    tab_a, tab_b, tab_c = rel_bias[:, :H_A], rel_bias[:, H_A:H_A + H_B], rel_bias[:, H_A + H_B:]
    consts = {"et_pad": et_pad, "overlap_t": jnp.asarray(ov.T, BF),
              "near_a": _near_bias(tab_a, TQ, 2 * TQ), "far_a": tab_a[N_BUCKETS - 1],
              "near_s": _near_bias(tab_b, TQ, 2 * TQ), "near_w": _near_bias(tab_b, WIN_B, WIN_B + TQ),
              "far_s": tab_b[N_BUCKETS - 1], "near_c": _near_bias(tab_c, WIN_C, 2 * TQ)}

    xf = x.reshape(m, d)
    xb = xf.astype(BF)
    w_down_b = w_down.astype(BF)
    w_in_t = jnp.swapaxes(w_in, 1, 2)
    for li in range(depth):
        xf, xb = _layer(li, xf, xb, p[li].reshape(m, -1).astype(BF), w_in_t, g_cq, g_ckv, g_kidx, b_kidx,
                        w_qidx, w_uq, w_uk, w_uv, cmp_pe, cmp_w1, cmp_w2, sinks,
                        w_br, w_o, ln1_g, ln1_b, w_up, conv_w, conv_b, w_down_b,
                        w_pg, b_pg, w_pp, ln2_g, ln2_b, consts, bsz=bsz, seq=seq, alpha=alpha)
    return xf.reshape(bsz, seq, d)
```

```python
import functools
import math

import jax
import jax.numpy as jnp
import numpy as np
from jax import lax
from jax.experimental import pallas as pl
from jax.experimental.pallas import tpu as pltpu

HEAD_DIM = 128
NEG = -1e30
FORCE_SCORE = 1e4
LN_EPS = 1e-5
N_BUCKETS = 32
MAX_DISTANCE = 128
H_A = 8
Q_LORA = 768
KV_LORA = 256
H_IDX = 32
D_IDX = 128
TOPK_MAX = 256
H_B = 8
G_B = 2
CMP_BLOCK = 32
CMP_STRIDE = 16
CMP_HIDDEN = 128
SEL_BLOCK = 64
N_SEL = 16
WIN_B = 512
H_C = 16
KV_C = 2
HEAD_DIM_C = 64
WIN_C = 128
N_BRANCH = 3
BRANCH_W = 1024
CONV_W = 3

LANES = 128
TQ = 128
TK = 512
VMEM_LIMIT = 56 * 1024 * 1024
INT_MIN = -2 ** 31

OFF_QB = 0
OFF_QC = 1024
OFF_CQ = 2048
OFF_CKV = 2816
OFF_KIDX = 3072
OFF_KVB = 3200
OFF_KC = 4736
OFF_VC = 4864
OFF_WIDX = 4992
N_SMALL = 5120

BF = jnp.bfloat16
F32 = jnp.float32


def _cparams(sem, vmem=VMEM_LIMIT):
    return pltpu.CompilerParams(dimension_semantics=sem, vmem_limit_bytes=vmem)


def _dot(a, b):
    return jnp.dot(a, b, preferred_element_type=F32)


def _layered(w):
    return (w[0], (w[1],)) if isinstance(w, tuple) else (w, ())


def _wspec(lead, block, imap):
    return pl.BlockSpec((None,) * len(lead) + block, lambda *g: lead + imap(*g))


def _dot_t(a, b):
    return lax.dot_general(a, b, (((1,), (1,)), ((), ())), preferred_element_type=F32)


def _mm_kernel(a_ref, b_ref, *refs, nk, trans_b):
    *o_refs, acc_ref = refs
    b = b_ref[...].astype(BF)
    part = _dot_t(a_ref[...], b) if trans_b else _dot(a_ref[...], b)

    def emit(val):
        for o_ref in o_refs:
            o_ref[...] = val.astype(o_ref.dtype)

    if nk == 1:
        emit(part)
    else:
        k = pl.program_id(2)

        @pl.when(k == 0)
        def _():
            acc_ref[...] = part

        @pl.when(k > 0)
        def _():
            acc_ref[...] += part

        @pl.when(k == nk - 1)
        def _():
            emit(acc_ref[...])


def _mm(a, b, *, tm, tn, tk=None, out_dtype=F32, also=None, trans_b=False, name="mm"):
    m, kdim = a.shape
    b, lead = _layered(b)
    n = b.shape[-2] if trans_b else b.shape[-1]
    tk = kdim if tk is None else tk
    nk = kdim // tk
    assert m % tm == 0 and kdim % tk == 0
    acc_shape = (tm, tn) if nk > 1 else (8, LANES)
    b_spec = (_wspec(lead, (tn, tk), lambda i, j, k: (j, k)) if trans_b
              else _wspec(lead, (tk, tn), lambda i, j, k: (k, j)))
    dtypes = [out_dtype] + ([also] if also is not None else [])
    out = pl.pallas_call(
        functools.partial(_mm_kernel, nk=nk, trans_b=trans_b),
        grid=(m // tm, pl.cdiv(n, tn), nk),
        in_specs=[pl.BlockSpec((tm, tk), lambda i, j, k: (i, k)), b_spec],
        out_specs=[pl.BlockSpec((tm, tn), lambda i, j, k: (i, j)) for _ in dtypes],
        out_shape=[jax.ShapeDtypeStruct((m, n), dt) for dt in dtypes],
        scratch_shapes=[pltpu.VMEM(acc_shape, F32)],
        compiler_params=_cparams(("parallel", "parallel", "arbitrary")),
        name=name,
    )(a, b)
    return out if also is not None else out[0]


def _prep_kernel(c_ref, k_ref, gq_ref, gkv_ref, gk_ref, bk_ref, cq_o, ckv_o, kidx_o):
    c = c_ref[...]
    cq = c[:, :Q_LORA]
    ckv = c[:, Q_LORA:]
    cq = cq * lax.rsqrt(jnp.mean(cq * cq, axis=-1, keepdims=True) + LN_EPS) * gq_ref[...]
    ckv = ckv * lax.rsqrt(jnp.mean(ckv * ckv, axis=-1, keepdims=True) + LN_EPS) * gkv_ref[...]
    k = k_ref[...]
    mu = jnp.mean(k, axis=-1, keepdims=True)
    kc = k - mu
    var = jnp.mean(kc * kc, axis=-1, keepdims=True)
    kn = kc * lax.rsqrt(var + LN_EPS) * gk_ref[...] + bk_ref[...]
    cq_o[...] = cq.astype(BF)
    ckv_o[...] = ckv.astype(BF)
    kidx_o[...] = kn.astype(BF)


def _prep(z, g_cq, g_ckv, g_kidx, b_kidx, *, tm=512):
    m = z.shape[0]
    wc = Q_LORA + KV_LORA
    row = lambda a: a.reshape(1, -1)
    full = lambda n: pl.BlockSpec((1, n), lambda i: (0, 0))
    return pl.pallas_call(
        _prep_kernel,
        grid=(m // tm,),
        in_specs=[pl.BlockSpec((tm, wc), lambda i: (i, OFF_CQ // wc)),
                  pl.BlockSpec((tm, D_IDX), lambda i: (i, OFF_KIDX // D_IDX)),
                  full(Q_LORA), full(KV_LORA), full(D_IDX), full(D_IDX)],
        out_specs=[pl.BlockSpec((tm, Q_LORA), lambda i: (i, 0)),
                   pl.BlockSpec((tm, KV_LORA), lambda i: (i, 0)),
                   pl.BlockSpec((tm, D_IDX), lambda i: (i, 0))],
        out_shape=[jax.ShapeDtypeStruct((m, Q_LORA), BF),
                   jax.ShapeDtypeStruct((m, KV_LORA), BF),
                   jax.ShapeDtypeStruct((m, D_IDX), BF)],
        compiler_params=_cparams(("parallel",)),
        name="prep_norms",
    )(z, z, row(g_cq), row(g_ckv), row(g_kidx), row(b_kidx))


def _to_key(x):
    b = lax.bitcast_convert_type(x, jnp.int32)
    return b ^ ((b >> 31) & jnp.int32(0x7FFFFFFF))


def _kth_largest_key(count_ge, k, shape):
    kf = jnp.float32(k)
    zero = jnp.zeros(shape, jnp.int32)
    t0 = jnp.where(count_ge(zero) >= kf, zero, jnp.full(shape, INT_MIN, jnp.int32))

    def body(it, t):
        cand = t + jnp.left_shift(jnp.int32(1), jnp.int32(30) - it)
        return jnp.where(count_ge(cand) >= kf, cand, t)

    t = lax.fori_loop(0, 31, body, t0)
    return jnp.maximum(t, jnp.int32(INT_MIN + 1))


TIE_POS_BITS = 14
FOLD_ROWS = 64


def _fold_rows(x):
    n, w = x.shape
    return x.reshape(n // FOLD_ROWS, FOLD_ROWS, w).sum(axis=0)


MASKED = 2.0 * NEG


def _online_update_t(m_ref, l_ref, acc_ref, slots, logits, masks, v_ts):
    scaled = []
    for sl, s, mask in zip(slots, logits, masks):
        s = jnp.where(mask, s, MASKED)
        m_old = m_ref[sl]
        m_new = jnp.maximum(m_old, jnp.max(s, axis=0, keepdims=True))
        alpha = jnp.exp(m_old - m_new)
        p = jnp.exp(s - m_new)
        l_ref[sl] = alpha * l_ref[sl] + jnp.sum(p, axis=0, keepdims=True)
        m_ref[sl] = m_new
        scaled.append((alpha, p.astype(BF)))
    for sl, (alpha, p), v_t in zip(slots, scaled, v_ts):
        acc_ref[sl] = alpha * acc_ref[sl] + _dot(v_t, p)


def _dsa_kernel(brow_ref, qi_ref, q_ref, w_ref, kidx_ref, ckv_ref, kvtf_ref, kvtn_ref, wuk_ref, wuvt_ref, neart_ref,
                o_ref, keyf_ref, keyn_ref, qstk_ref, qlstk_ref, m_ref, l_ref, acc_ref, *, topk):
    i = pl.program_id(1)
    tq = TQ
    npair = H_A // 2
    far_end = jnp.maximum(i - 1, 0) * tq
    nfar = (far_end + TK - 1) // TK
    row0 = pl.multiple_of(i * tq, tq)

    for h in range(H_IDX):
        qstk_ref[h * tq:(h + 1) * tq, :] = qi_ref[:, h * D_IDX:(h + 1) * D_IDX]
    w_t = (w_ref[...] * (D_IDX ** -0.5 * H_IDX ** -0.5)).T

    def scores_t(krows):
        acc = jnp.zeros((krows.shape[0], tq), F32)
        for hp in range(H_IDX // 2):
            s = _dot_t(krows, qstk_ref[2 * hp * tq:(2 * hp + 2) * tq, :])
            acc = (acc + jnp.maximum(s[:, :tq], 0.0) * w_t[2 * hp:2 * hp + 1, :]
                   + jnp.maximum(s[:, tq:], 0.0) * w_t[2 * hp + 1:2 * hp + 2, :])
        return acc

    rrow = lax.broadcasted_iota(jnp.int32, (2 * tq, tq), 0)
    ccol = lax.broadcasted_iota(jnp.int32, (2 * tq, tq), 1)
    valid_n = (tq + ccol - rrow >= 0) & ((i - 1) * tq + rrow >= 0)
    keyn_ref[...] = jnp.where(valid_n, _to_key(scores_t(kidx_ref[pl.ds(row0, 2 * tq), :])), jnp.int32(INT_MIN))

    def far_scores(kt, c):
        r0 = pl.multiple_of(tq + kt * TK, tq)
        s = scores_t(kidx_ref[pl.ds(r0, TK), :])
        pos = kt * TK + lax.broadcasted_iota(jnp.int32, (TK, tq), 0)
        keyf_ref[kt] = jnp.where(pos < far_end, _to_key(s), jnp.int32(INT_MIN))
        return c

    lax.fori_loop(0, nfar, far_scores, 0)

    def count_ge(cand):
        part = _fold_rows(jnp.where(keyn_ref[...] >= cand, 1.0, 0.0))

        def body(kt, a):
            return a + _fold_rows(jnp.where(keyf_ref[kt] >= cand, 1.0, 0.0))

        part = lax.fori_loop(0, nfar, body, part)
        return jnp.sum(part, axis=0, keepdims=True)

    thr = _kth_largest_key(count_ge, topk, (1, tq))

    def count2(pred):
        upos_n = lax.broadcasted_iota(jnp.int32, (2 * tq, tq), 0) + (i - 1) * tq + tq
        part = _fold_rows(jnp.where(pred(keyn_ref[...], upos_n), 1.0, 0.0))

        def body(kt, a):
            upos = kt * TK + tq + lax.broadcasted_iota(jnp.int32, (TK, tq), 0)
            return a + _fold_rows(jnp.where(pred(keyf_ref[kt], upos), 1.0, 0.0))

        return jnp.sum(lax.fori_loop(0, nfar, body, part), axis=0, keepdims=True)

    n_gt = count2(lambda kk, up: kk > thr)
    n_eq = count2(lambda kk, up: kk == thr)
    need = jnp.float32(topk) - n_gt
    surplus = n_eq > need

    @pl.when(jnp.max(jnp.where(surplus, 1.0, 0.0)) > 0.0)
    def _():
        def body(it, r):
            cand = r + jnp.left_shift(jnp.int32(1), jnp.int32(TIE_POS_BITS - 1) - it)
            below = count2(lambda kk, up: (kk == thr) & (up < cand))
            return jnp.where(below < need, cand, r)

        last = lax.fori_loop(0, TIE_POS_BITS, body, jnp.zeros((1, tq), jnp.int32))
        last = jnp.where(surplus, last, jnp.int32(2 ** TIE_POS_BITS))
        upos_n = lax.broadcasted_iota(jnp.int32, (2 * tq, tq), 0) + (i - 1) * tq + tq
        kn = keyn_ref[...]
        keyn_ref[...] = jnp.where((kn == thr) & (upos_n > last), jnp.int32(INT_MIN), kn)

        def demote(kt, c):
            upos = kt * TK + tq + lax.broadcasted_iota(jnp.int32, (TK, tq), 0)
            kf = keyf_ref[kt]
            keyf_ref[kt] = jnp.where((kf == thr) & (upos > last), jnp.int32(INT_MIN), kf)
            return c

        lax.fori_loop(0, nfar, demote, 0)

    for h in range(H_A):
        ql = _dot(q_ref[:, h * HEAD_DIM:(h + 1) * HEAD_DIM], wuk_ref[h]) * (HEAD_DIM ** -0.5)
        qlstk_ref[h * tq:(h + 1) * tq, :] = ql.astype(BF)

    for hp in range(npair):
        m_ref[hp] = jnp.full((1, 2 * tq), NEG, F32)
        l_ref[hp] = jnp.zeros((1, 2 * tq), F32)
        acc_ref[hp] = jnp.zeros((KV_LORA, 2 * tq), F32)

    pairs = range(npair)

    def update(kv, kv_t, bias, sel):
        sel2 = jnp.concatenate([sel, sel], axis=1)
        logits = [_dot_t(kv, qlstk_ref[2 * hp * tq:(2 * hp + 2) * tq, :]) + bias(hp) for hp in pairs]
        _online_update_t(m_ref, l_ref, acc_ref, pairs, logits, [sel2] * npair, [kv_t] * npair)

    update(ckv_ref[pl.ds(row0, 2 * tq), :], jnp.concatenate([kvtn_ref[i], kvtn_ref[i + 1]], axis=1),
           lambda hp: neart_ref[hp], keyn_ref[...] >= thr)

    def far_attn(kt, c):
        r0 = pl.multiple_of(tq + kt * TK, tq)
        update(ckv_ref[pl.ds(r0, TK), :], kvtf_ref[kt], lambda hp: brow_ref[hp], keyf_ref[kt] >= thr)
        return c

    lax.fori_loop(0, nfar, far_attn, 0)

    for hp in range(npair):
        l = l_ref[hp]
        o_lat_t = (acc_ref[hp] * jnp.where(l > 0.0, 1.0 / l, 0.0)).astype(BF)
        for hh in range(2):
            h = 2 * hp + hh
            o_t = _dot(wuvt_ref[h], o_lat_t[:, hh * tq:(hh + 1) * tq])
            o_ref[:, h * HEAD_DIM:(h + 1) * HEAD_DIM] = o_t.T.astype(o_ref.dtype)


def _dsa(qq, z, kidx_n, ckv_n, w_uk, w_uv, near_a, far_a, *, seq):
    bsz = qq.shape[0]
    topk = min(TOPK_MAX, seq // 4)
    nqi = H_IDX * D_IDX
    npair = H_A // 2
    kidx_pad = _pad_front(kidx_n, TQ)
    ckv_pad = _pad_front(ckv_n, TQ)
    kvt_far = jnp.transpose(ckv_n.reshape(bsz, seq // TK, TK, KV_LORA), (0, 1, 3, 2))
    kvt_near = jnp.transpose(ckv_pad.reshape(bsz, seq // TQ + 1, TQ, KV_LORA), (0, 1, 3, 2))
    wuk_t = jnp.transpose(w_uk, (1, 2, 0)).astype(BF)
    wuv_t = jnp.transpose(w_uv, (1, 2, 0)).astype(BF)
    brow = jnp.repeat(far_a.reshape(npair, 2), TQ, axis=1).reshape(npair, 1, 2 * TQ)
    near_t = jnp.transpose(jnp.transpose(near_a, (0, 2, 1)).reshape(npair, 2, 2 * TQ, TQ),
                           (0, 2, 1, 3)).reshape(npair, 2 * TQ, 2 * TQ)
    whole = lambda shape: pl.BlockSpec(shape, lambda b, i: (0,) * len(shape))
    perb = lambda shape: pl.BlockSpec((None,) + shape, lambda b, i: (b,) + (0,) * len(shape))
    return pl.pallas_call(
        functools.partial(_dsa_kernel, topk=topk),
        grid=(bsz, seq // TQ),
        in_specs=[whole((npair, 1, 2 * TQ)),
                  pl.BlockSpec((None, TQ, nqi), lambda b, i: (b, i, 0)),
                  pl.BlockSpec((None, TQ, H_A * HEAD_DIM), lambda b, i: (b, i, nqi // (H_A * HEAD_DIM))),
                  pl.BlockSpec((None, TQ, LANES), lambda b, i: (b, i, OFF_WIDX // LANES)),
                  perb((seq + TQ, D_IDX)), perb((seq + TQ, KV_LORA)),
                  perb((seq // TK, KV_LORA, TK)), perb((seq // TQ + 1, KV_LORA, TQ)),
                  whole((H_A, HEAD_DIM, KV_LORA)), whole((H_A, HEAD_DIM, KV_LORA)),
                  whole((npair, 2 * TQ, 2 * TQ))],
        out_specs=pl.BlockSpec((None, TQ, H_A * HEAD_DIM), lambda b, i: (b, i, 0)),
        out_shape=jax.ShapeDtypeStruct((bsz, seq, H_A * HEAD_DIM), BF),
        scratch_shapes=[pltpu.VMEM((seq // TK, TK, TQ), jnp.int32),
                        pltpu.VMEM((2 * TQ, TQ), jnp.int32),
                        pltpu.VMEM((H_IDX * TQ, D_IDX), BF),
                        pltpu.VMEM((H_A * TQ, KV_LORA), BF),
                        pltpu.VMEM((npair, 1, 2 * TQ), F32),
                        pltpu.VMEM((npair, 1, 2 * TQ), F32),
                        pltpu.VMEM((npair, KV_LORA, 2 * TQ), F32)],
        compiler_params=_cparams(("parallel", "parallel")),
        name="dsa_mixer",
    )(brow, qq, qq, z, kidx_pad, ckv_pad, kvt_far, kvt_near, wuk_t, wuv_t, near_t)


def _cmp_kernel(x_ref, pe_ref, w1_ref, w2_ref, o_ref, xp_ref, *, seq, ncp):
    xp_ref[0:seq, :] = x_ref[...]
    xp_ref[seq:seq + CMP_STRIDE, :] = jnp.zeros((CMP_STRIDE, HEAD_DIM), F32)
    acc = jnp.zeros((ncp, CMP_HIDDEN), F32)
    for l in range(CMP_BLOCK):
        rows = xp_ref[pl.ds(l, ncp, stride=CMP_STRIDE), :]
        blk = rows + pe_ref[l:l + 1, :]
        acc = acc + _dot(blk.astype(BF), w1_ref[l].astype(BF))
    hdn = jax.nn.gelu(acc)
    o_ref[...] = _dot(hdn.astype(BF), w2_ref[...].astype(BF)).astype(o_ref.dtype)


def _compress(z, cmp_pe, cmp_w1, cmp_w2, *, seq):
    bsz = z.shape[0]
    ncp = seq // CMP_STRIDE
    cmp_pe, lead_pe = _layered(cmp_pe)
    cmp_w1, lead_w1 = _layered(cmp_w1)
    cmp_w2, lead_w2 = _layered(cmp_w2)
    return pl.pallas_call(
        functools.partial(_cmp_kernel, seq=seq, ncp=ncp),
        grid=(bsz, 2, G_B),
        in_specs=[pl.BlockSpec((None, seq, HEAD_DIM), lambda b, c, g: (b, 0, OFF_KVB // HEAD_DIM + c * G_B + g)),
                  _wspec(lead_pe, (None, CMP_BLOCK, HEAD_DIM), lambda b, c, g: (c, 0, 0)),
                  _wspec(lead_w1, (None, CMP_BLOCK, HEAD_DIM, CMP_HIDDEN), lambda b, c, g: (c, 0, 0, 0)),
                  _wspec(lead_w2, (None, CMP_HIDDEN, HEAD_DIM), lambda b, c, g: (c, 0, 0))],
        out_specs=pl.BlockSpec((None, None, None, ncp, HEAD_DIM), lambda b, c, g: (b, c, g, 0, 0)),
        out_shape=jax.ShapeDtypeStruct((bsz, 2, G_B, ncp, HEAD_DIM), BF),
        scratch_shapes=[pltpu.VMEM((seq + CMP_STRIDE, HEAD_DIM), F32)],
        compiler_params=_cparams(("parallel", "parallel", "parallel")),
        name="nsa_compress",
    )(z, cmp_pe, cmp_w1, cmp_w2)


def _nsa_kernel(brow_ref, q_ref, gb_ref, kc_ref, vct_ref, ksl_ref, vstf_ref, vstn_ref, kw_ref, vwt_ref, et_ref, ovt_ref,
                nearts_ref, neartw_ref, o_ref, qstk_ref, imp_ref, mselt_ref, m_ref, l_ref, acc_ref, ot_ref, *, seq, npick):
    i = pl.program_id(1)
    tq = TQ
    hpg = H_B // G_B
    gl = hpg * tq
    ncp = seq // CMP_STRIDE
    nblk = SEL_BLOCK
    far_end = jnp.maximum(i - 1, 0) * tq
    nfar = (far_end + TK - 1) // TK
    row0 = pl.multiple_of(i * tq, tq)
    tile4 = lambda x: jnp.concatenate([x] * hpg, axis=1)
    tlane = i * tq + lax.broadcasted_iota(jnp.int32, (1, tq), 1)

    for h in range(H_B):
        qstk_ref[h * tq:(h + 1) * tq, :] = (q_ref[:, h * HEAD_DIM:(h + 1) * HEAD_DIM] * (HEAD_DIM ** -0.5)).astype(BF)
    sig_t = jax.nn.sigmoid(gb_ref[...]).T

    def gate_row(g, c):
        return jnp.concatenate([sig_t[H_IDX + 3 * (g * hpg + j) + c:H_IDX + 3 * (g * hpg + j) + c + 1, :]
                                for j in range(hpg)], axis=1)

    qgrp = lambda g: qstk_ref[g * gl:(g + 1) * gl, :]
    grows = lambda g: slice(g * HEAD_DIM, (g + 1) * HEAD_DIM)

    nrow = lax.broadcasted_iota(jnp.int32, (ncp, tq), 0)
    cmp_mask = tile4(nrow * CMP_STRIDE + (CMP_BLOCK - 1) <= tlane)
    anyc = tile4(jnp.where(tlane >= CMP_BLOCK - 1, 1.0, 0.0))
    brow = lax.broadcasted_iota(jnp.int32, (nblk, tq), 0)
    cur = tlane // SEL_BLOCK
    forced = (brow == 0) | (brow == cur) | (brow == cur - 1)
    sub = lax.broadcasted_iota(jnp.int32, (8, tq), 0)
    for g in range(G_B):
        s = jnp.where(cmp_mask, _dot_t(kc_ref[g], qgrp(g)), NEG)
        e = jnp.exp(s - jnp.max(s, axis=0, keepdims=True))
        p = e * (anyc / jnp.sum(e, axis=0, keepdims=True))
        ot_ref[g] = gate_row(g, 0) * _dot(vct_ref[g], p.astype(BF))
        psum = p[:, :tq]
        for j in range(1, hpg):
            psum = psum + p[:, j * tq:(j + 1) * tq]
        hi = psum.astype(BF)
        lo = (psum - hi.astype(F32)).astype(BF)
        imp = (_dot(ovt_ref[...], hi) + _dot(ovt_ref[...], lo))[:nblk]
        imp = jnp.where(brow > cur, NEG, jnp.where(forced, FORCE_SCORE, imp))
        imp_ref[...] = imp
        vals = [imp[8 * r:8 * r + 8] for r in range(nblk // 8)]
        rank = [jnp.zeros((8, tq), F32) for _ in vals]
        for mp in range(nblk):
            other = jnp.broadcast_to(imp_ref[mp:mp + 1, :], (8, tq))
            for r, v in enumerate(vals):
                if 8 * r > mp:
                    beats = other >= v
                elif 8 * r + 7 < mp:
                    beats = other > v
                else:
                    beats = (other > v) | ((other == v) & (sub + 8 * r > mp))
                rank[r] = rank[r] + jnp.where(beats, 1.0, 0.0)
        sel = jnp.concatenate([jnp.where(rk < npick, 1.0, 0.0) for rk in rank], axis=0)
        mselt_ref[g] = jnp.concatenate([sel, jnp.zeros((LANES - nblk, tq), F32)], axis=0).astype(BF)

    rrow = lax.broadcasted_iota(jnp.int32, (2 * tq, tq), 0)
    ccol = lax.broadcasted_iota(jnp.int32, (2 * tq, tq), 1)
    causal_n = tq + ccol - rrow >= 0
    etn = et_ref[pl.ds(row0, 2 * tq), :]
    groups = range(G_B)
    for g in groups:
        m_ref[g] = jnp.full((1, gl), NEG, F32)
        l_ref[g] = jnp.zeros((1, gl), F32)
        acc_ref[g] = jnp.zeros((HEAD_DIM, gl), F32)
    _online_update_t(
        m_ref, l_ref, acc_ref, groups,
        [_dot_t(ksl_ref[pl.ds(row0, 2 * tq), grows(g)], qgrp(g)) + nearts_ref[g] for g in groups],
        [tile4(causal_n & (_dot(etn, mselt_ref[g]) > 0.5)) for g in groups],
        [jnp.concatenate([vstn_ref[i, grows(g), :], vstn_ref[i + 1, grows(g), :]], axis=1) for g in groups])

    def far_attn(kt, c):
        r0 = pl.multiple_of(tq + kt * TK, tq)
        ett = et_ref[pl.ds(r0, TK), :]
        infar = kt * TK + lax.broadcasted_iota(jnp.int32, (TK, tq), 0) < far_end
        _online_update_t(
            m_ref, l_ref, acc_ref, groups,
            [_dot_t(ksl_ref[pl.ds(r0, TK), grows(g)], qgrp(g)) + brow_ref[g] for g in groups],
            [tile4(infar & (_dot(ett, mselt_ref[g]) > 0.5)) for g in groups],
            [vstf_ref[kt, grows(g), :] for g in groups])
        return c

    lax.fori_loop(0, nfar, far_attn, 0)
    for g in groups:
        l = l_ref[g]
        ot_ref[g] += gate_row(g, 1) * (acc_ref[g] * jnp.where(l > 0.0, 1.0 / l, 0.0))

    span = WIN_B + tq
    rw = lax.broadcasted_iota(jnp.int32, (span, tq), 0)
    cw = lax.broadcasted_iota(jnp.int32, (span, tq), 1)
    distw = WIN_B + cw - rw
    mask_w = tile4((distw >= 0) & (distw < WIN_B) & (i * tq - WIN_B + rw >= 0))
    for g in groups:
        s = jnp.where(mask_w, _dot_t(kw_ref[pl.ds(row0, span), grows(g)], qgrp(g)) + neartw_ref[g], NEG)
        e = jnp.exp(s - jnp.max(s, axis=0, keepdims=True))
        p = e / jnp.sum(e, axis=0, keepdims=True)
        vw_t = jnp.concatenate([vwt_ref[i + k, grows(g), :] for k in range(span // tq)], axis=1)
        ot_ref[g] += gate_row(g, 2) * _dot(vw_t, p.astype(BF))

    for h in range(H_B):
        g, j = divmod(h, hpg)
        o_ref[:, h * HEAD_DIM:(h + 1) * HEAD_DIM] = ot_ref[g][:, j * tq:(j + 1) * tq].T.astype(o_ref.dtype)


def _tiles_t(a, rows):
    bsz, n, w = a.shape
    return jnp.transpose(a.reshape(bsz, n // rows, rows, w), (0, 1, 3, 2))


def _lanes_by_group(near, groups):
    h, tq, w = near.shape
    t = jnp.transpose(near, (0, 2, 1)).reshape(groups, h // groups, w, tq)
    return jnp.transpose(t, (0, 2, 1, 3)).reshape(groups, w, (h // groups) * tq)


def _nsa(z, kcv, k_slc, v_slc, k_win, v_win, et_pad, overlap_t, near_s, near_w, far_s, *, seq):
    bsz = z.shape[0]
    ncp = seq // CMP_STRIDE
    npick = min(N_SEL, seq // SEL_BLOCK)
    assert seq // SEL_BLOCK <= SEL_BLOCK
    gw = G_B * HEAD_DIM
    hpg = H_B // G_B
    gl = hpg * TQ
    kc = kcv[:, 0]
    vc_t = jnp.transpose(kcv[:, 1], (0, 1, 3, 2))
    brow = jnp.repeat(far_s.reshape(G_B, hpg), TQ, axis=1).reshape(G_B, 1, gl)
    whole = lambda shape: pl.BlockSpec(shape, lambda b, i: (0,) * len(shape))
    perb = lambda shape: pl.BlockSpec((None,) + shape, lambda b, i: (b,) + (0,) * len(shape))
    return pl.pallas_call(
        functools.partial(_nsa_kernel, seq=seq, npick=npick),
        grid=(bsz, seq // TQ),
        in_specs=[whole((G_B, 1, gl)),
                  pl.BlockSpec((None, TQ, H_B * HEAD_DIM), lambda b, i: (b, i, OFF_QB // (H_B * HEAD_DIM))),
                  pl.BlockSpec((None, TQ, LANES), lambda b, i: (b, i, OFF_WIDX // LANES)),
                  perb((G_B, ncp, HEAD_DIM)), perb((G_B, HEAD_DIM, ncp)),
                  perb((seq + TQ, gw)), perb((seq // TK, gw, TK)), perb((seq // TQ + 1, gw, TQ)),
                  perb((seq + WIN_B, gw)), perb(((seq + WIN_B) // TQ, gw, TQ)),
                  whole((seq + TQ, LANES)), whole((LANES, ncp)),
                  whole((G_B, 2 * TQ, gl)), whole((G_B, WIN_B + TQ, gl))],
        out_specs=pl.BlockSpec((None, TQ, H_B * HEAD_DIM), lambda b, i: (b, i, 0)),
        out_shape=jax.ShapeDtypeStruct((bsz, seq, H_B * HEAD_DIM), BF),
        scratch_shapes=[pltpu.VMEM((H_B * TQ, HEAD_DIM), BF),
                        pltpu.VMEM((SEL_BLOCK, TQ), F32),
                        pltpu.VMEM((G_B, LANES, TQ), BF),
                        pltpu.VMEM((G_B, 1, gl), F32),
                        pltpu.VMEM((G_B, 1, gl), F32),
                        pltpu.VMEM((G_B, HEAD_DIM, gl), F32),
                        pltpu.VMEM((G_B, HEAD_DIM, gl), F32)],
        compiler_params=_cparams(("parallel", "parallel")),
        name="nsa_mixer",
    )(brow, z, z, kc, vc_t, _pad_front(k_slc, TQ), _tiles_t(v_slc, TK), _tiles_t(_pad_front(v_slc, TQ), TQ),
      _pad_front(k_win, WIN_B), _tiles_t(_pad_front(v_win, WIN_B), TQ), et_pad, overlap_t,
      _lanes_by_group(near_s, G_B), _lanes_by_group(near_w, G_B))


def _swa_kernel(sink_ref, q_ref, k_ref, vt_ref, neart_ref, o_ref, qstk_ref):
    i = pl.program_id(1)
    tq = TQ
    hpg = H_C // KV_C
    gl = hpg * tq
    row0 = pl.multiple_of(i * tq, tq)
    rrow = lax.broadcasted_iota(jnp.int32, (2 * tq, tq), 0)
    ccol = lax.broadcasted_iota(jnp.int32, (2 * tq, tq), 1)
    dist = WIN_C + ccol - rrow
    mask = jnp.concatenate([(dist >= 0) & (dist < WIN_C) & (i * tq - WIN_C + rrow >= 0)] * hpg, axis=1)
    for h in range(H_C):
        qstk_ref[h * tq:(h + 1) * tq, :] = (q_ref[:, h * HEAD_DIM_C:(h + 1) * HEAD_DIM_C]
                                            * (HEAD_DIM_C ** -0.5)).astype(BF)
    kk = k_ref[pl.ds(row0, 2 * tq), :]
    vt = jnp.concatenate([vt_ref[i], vt_ref[i + 1]], axis=1)
    for g in range(KV_C):
        cols = slice(g * HEAD_DIM_C, (g + 1) * HEAD_DIM_C)
        s = jnp.where(mask, _dot_t(kk[:, cols], qstk_ref[g * gl:(g + 1) * gl, :]) + neart_ref[g], NEG)
        sink = sink_ref[g]
        m = jnp.maximum(jnp.max(s, axis=0, keepdims=True), sink)
        e = jnp.exp(s - m)
        p = e / (jnp.sum(e, axis=0, keepdims=True) + jnp.exp(sink - m))
        o_t = _dot(vt[cols, :], p.astype(BF))
        for j in range(hpg):
            h = g * hpg + j
            o_ref[:, h * HEAD_DIM_C:(h + 1) * HEAD_DIM_C] = o_t[:, j * tq:(j + 1) * tq].T.astype(o_ref.dtype)


def _swa(z, k_c, v_c, near_c, sinks, *, seq):
    bsz = z.shape[0]
    kvw = KV_C * HEAD_DIM_C
    hpg = H_C // KV_C
    gl = hpg * TQ
    sink_rows = jnp.repeat(sinks.reshape(KV_C, hpg), TQ, axis=1).reshape(KV_C, 1, gl)
    whole = lambda shape: pl.BlockSpec(shape, lambda b, i: (0,) * len(shape))
    return pl.pallas_call(
        _swa_kernel,
        grid=(bsz, seq // TQ),
        in_specs=[whole((KV_C, 1, gl)),
                  pl.BlockSpec((None, TQ, BRANCH_W), lambda b, i: (b, i, OFF_QC // BRANCH_W)),
                  pl.BlockSpec((None, seq + WIN_C, kvw), lambda b, i: (b, 0, 0)),
                  pl.BlockSpec((None, (seq + WIN_C) // TQ, kvw, TQ), lambda b, i: (b, 0, 0, 0)),
                  whole((KV_C, 2 * TQ, gl))],
        out_specs=pl.BlockSpec((None, TQ, BRANCH_W), lambda b, i: (b, i, 0)),
        out_shape=jax.ShapeDtypeStruct((bsz, seq, BRANCH_W), BF),
        scratch_shapes=[pltpu.VMEM((H_C * TQ, HEAD_DIM_C), BF)],
        compiler_params=_cparams(("parallel", "parallel")),
        name="swa_mixer",
    )(sink_rows, z, _pad_front(k_c, WIN_C), _tiles_t(_pad_front(v_c, WIN_C), TQ), _lanes_by_group(near_c, KV_C))


def _merge_kernel(x_ref, wg_ref, o_ref, wbr_ref, out_ref, acc_ref):
    br = pl.program_id(2)
    gate = jax.nn.sigmoid(_dot_t(x_ref[...], wg_ref[...].astype(BF)))
    val = _dot(o_ref[...], wbr_ref[...].astype(BF))

    @pl.when(br == 0)
    def _():
        acc_ref[...] = gate * val

    @pl.when(br > 0)
    def _():
        acc_ref[...] += gate * val

    @pl.when(br == N_BRANCH - 1)
    def _():
        out_ref[...] = acc_ref[...].astype(out_ref.dtype)


def _merge(xb, w_in_t, li, gate_row0, o_all, w_br, *, tm, tn):
    m, d = xb.shape
    nj = d // tn
    w_br, lead = _layered(w_br)
    depth, n_in, _ = w_in_t.shape
    row0 = li * n_in + gate_row0
    return pl.pallas_call(
        _merge_kernel,
        grid=(m // tm, nj, N_BRANCH),
        in_specs=[pl.BlockSpec((tm, d), lambda i, j, r: (i, 0)),
                  pl.BlockSpec((pl.Element(tn), pl.Element(d)),
                               lambda i, j, r: (pl.multiple_of(row0 + r * d + j * tn, 8), 0)),
                  pl.BlockSpec((None, tm, BRANCH_W), lambda i, j, r: (r, i, 0)),
                  _wspec(lead, (None, BRANCH_W, tn), lambda i, j, r: (r, 0, j))],
        out_specs=pl.BlockSpec((tm, tn), lambda i, j, r: (i, j)),
        out_shape=jax.ShapeDtypeStruct((m, d), BF),
        scratch_shapes=[pltpu.VMEM((tm, tn), F32)],
        compiler_params=_cparams(("parallel", "parallel", "arbitrary")),
        name="merge_gates",
    )(xb, w_in_t.reshape(depth * n_in, d), o_all, w_br)


def _resln_kernel(*refs, nj, tn, alpha, gated):
    if gated:
        (a_ref, w_ref, res_ref, g_ref, b_ref, bias_ref, p_ref, wp_ref, add_ref,
         of_ref, ob_ref, y_ref, mu_ref, rs_ref) = refs
    else:
        a_ref, w_ref, res_ref, g_ref, b_ref, of_ref, ob_ref, y_ref, mu_ref, rs_ref = refs
    j = pl.program_id(1)

    @pl.when(j < nj)
    def _():
        y = _dot(a_ref[...], w_ref[...].astype(BF))
        if gated:
            y = jax.nn.sigmoid(y + bias_ref[...]) * _dot(p_ref[...], wp_ref[...].astype(BF)) + add_ref[...]
        y_ref[j] = alpha * res_ref[...] + y

    @pl.when(j == nj - 1)
    def _():
        tm = y_ref.shape[1]
        d = nj * tn
        tot = jnp.zeros((tm, 1), F32)
        for c in range(nj):
            tot = tot + jnp.sum(y_ref[c], axis=-1, keepdims=True)
        mu = tot / d
        sq = jnp.zeros((tm, 1), F32)
        for c in range(nj):
            dc = y_ref[c] - mu
            sq = sq + jnp.sum(dc * dc, axis=-1, keepdims=True)
        mu_ref[...] = mu
        rs_ref[...] = lax.rsqrt(sq / d + LN_EPS)

    @pl.when(j >= nj)
    def _():
        o = (y_ref[j - nj] - mu_ref[...]) * rs_ref[...] * g_ref[...] + b_ref[...]
        of_ref[...] = o
        ob_ref[...] = o.astype(BF)


def _resln(a, w, res, g, b, *, alpha, tm, tn, gated=None, name="res_ln"):
    m, kdim = a.shape
    w, lead = _layered(w)
    d = w.shape[-1]
    nj = d // tn
    mm_col = lambda i, j: (0, jnp.minimum(j, nj - 1))
    mm_blk = lambda i, j: (i, jnp.minimum(j, nj - 1))
    out_col = lambda i, j: (0, jnp.maximum(j - nj, 0))
    out_blk = lambda i, j: (i, jnp.maximum(j - nj, 0))
    in_specs = [pl.BlockSpec((tm, kdim), lambda i, j: (i, 0)),
                _wspec(lead, (kdim, tn), mm_col),
                pl.BlockSpec((tm, tn), mm_blk),
                pl.BlockSpec((1, tn), out_col),
                pl.BlockSpec((1, tn), out_col)]
    args = [a, w, res, g.reshape(1, d), b.reshape(1, d)]
    if gated is not None:
        bias, p, wp, add = gated
        wp, lead_p = _layered(wp)
        dp = p.shape[1]
        in_specs += [pl.BlockSpec((1, tn), mm_col),
                     pl.BlockSpec((tm, dp), lambda i, j: (i, 0)),
                     _wspec(lead_p, (dp, tn), mm_col),
                     pl.BlockSpec((tm, tn), mm_blk)]
        args += [bias.reshape(1, d), p, wp, add]
    return pl.pallas_call(
        functools.partial(_resln_kernel, nj=nj, tn=tn, alpha=alpha, gated=gated is not None),
        grid=(m // tm, 2 * nj),
        in_specs=in_specs,
        out_specs=[pl.BlockSpec((tm, tn), out_blk), pl.BlockSpec((tm, tn), out_blk)],
        out_shape=[jax.ShapeDtypeStruct((m, d), F32), jax.ShapeDtypeStruct((m, d), BF)],
        scratch_shapes=[pltpu.VMEM((nj, tm, tn), F32), pltpu.VMEM((tm, 1), F32), pltpu.VMEM((tm, 1), F32)],
        compiler_params=_cparams(("parallel", "arbitrary")),
        name=name,
    )(*args)


HALO = 8


def _ffn_up_kernel(x_ref, wg_ref, wv_ref, cwg_ref, cwv_ref, cbg_ref, cbv_ref, a_ref, hg_ref, hv_ref, *, tm, ch, tiles_per_seq):
    i = pl.program_id(1)
    first = (i % tiles_per_seq) == 0

    @pl.when(first)
    def _():
        hg_ref[0:HALO, :] = jnp.zeros((HALO, hg_ref.shape[1]), F32)
        hv_ref[0:HALO, :] = jnp.zeros((HALO, hv_ref.shape[1]), F32)

    @pl.when(jnp.logical_not(first))
    def _():
        hg_ref[0:HALO, :] = hg_ref[tm:tm + HALO, :]
        hv_ref[0:HALO, :] = hv_ref[tm:tm + HALO, :]

    wg = wg_ref[...].astype(BF)
    wv = wv_ref[...].astype(BF)

    def conv(h_ref, c, cw_ref, cb_ref):
        out = cb_ref[...]
        for k in range(CONV_W):
            off = HALO + c * ch - (CONV_W - 1) + k
            out = out + h_ref[off:off + ch, :] * cw_ref[k:k + 1, :]
        return out

    def up(c):
        xc = x_ref[c * ch:(c + 1) * ch, :]
        hg_ref[HALO + c * ch:HALO + (c + 1) * ch, :] = _dot(xc, wg)
        hv_ref[HALO + c * ch:HALO + (c + 1) * ch, :] = _dot(xc, wv)

    up(0)
    for c in range(tm // ch):
        if c + 1 < tm // ch:
            up(c + 1)
        a_ref[c * ch:(c + 1) * ch, :] = (jax.nn.gelu(conv(hg_ref, c, cwg_ref, cbg_ref))
                                         * conv(hv_ref, c, cwv_ref, cbv_ref)).astype(a_ref.dtype)


def _ffn_up(xb, w_up, conv_w, conv_b, *, seq, tm, tn, ch=256):
    m, d = xb.shape
    w_up, lead = _layered(w_up)
    conv_w, lead_c = _layered(conv_w)
    conv_b, lead_b = _layered(conv_b)
    dff = w_up.shape[-1] // 2
    nj = dff // tn
    assert dff % tn == 0 and seq % tm == 0 and tm % ch == 0
    cb = conv_b.reshape(conv_b.shape[:-1] + (1, 2 * dff))
    return pl.pallas_call(
        functools.partial(_ffn_up_kernel, tm=tm, ch=ch, tiles_per_seq=seq // tm),
        grid=(nj, m // tm),
        in_specs=[pl.BlockSpec((tm, d), lambda j, i: (i, 0)),
                  _wspec(lead, (d, tn), lambda j, i: (0, j)),
                  _wspec(lead, (d, tn), lambda j, i: (0, j + nj)),
                  _wspec(lead_c, (CONV_W, tn), lambda j, i: (0, j)),
                  _wspec(lead_c, (CONV_W, tn), lambda j, i: (0, j + nj)),
                  _wspec(lead_b, (1, tn), lambda j, i: (0, j)),
                  _wspec(lead_b, (1, tn), lambda j, i: (0, j + nj))],
        out_specs=pl.BlockSpec((tm, tn), lambda j, i: (i, j)),
        out_shape=jax.ShapeDtypeStruct((m, dff), BF),
        scratch_shapes=[pltpu.VMEM((tm + HALO, tn), F32), pltpu.VMEM((tm + HALO, tn), F32)],
        compiler_params=_cparams(("parallel", "arbitrary")),
        name="ffn_up_conv_geglu",
    )(xb, w_up, w_up, conv_w, conv_w, cb, cb)


def _t5_bucket(dist):
    dist = jnp.maximum(dist, 0)
    max_exact = N_BUCKETS // 2
    d = jnp.maximum(dist, 1).astype(F32)
    large = max_exact + (jnp.log(d / max_exact) / math.log(MAX_DISTANCE / max_exact)
                         * (N_BUCKETS - max_exact)).astype(jnp.int32)
    large = jnp.minimum(large, N_BUCKETS - 1)
    return jnp.where(dist < max_exact, dist, large)


def _near_bias(tab, offset, width):
    period = TQ + width
    idx = np.arange(period)
    k = np.where(idx < width, idx, idx - period)
    dist = np.clip(offset - k, 0, MAX_DISTANCE)
    u = tab[_t5_bucket(jnp.asarray(dist, jnp.int32))].T
    flat = jnp.tile(u, (1, TQ))[:, :TQ * (period - 1)]
    return flat.reshape(-1, TQ, period - 1)[:, :, :width]


def _pad_front(a, n):
    return jnp.pad(a, ((0, 0), (n, 0), (0, 0)))


def _in_splits():
    s_kidx = Q_LORA + KV_LORA
    s_widx = s_kidx + D_IDX
    s_qb = s_widx + H_IDX
    s_kvb = s_qb + H_B * HEAD_DIM
    s_gb = s_kvb + 6 * G_B * HEAD_DIM
    s_qc = s_gb + 3 * H_B
    s_kc = s_qc + H_C * HEAD_DIM_C
    s_vc = s_kc + KV_C * HEAD_DIM_C
    s_gm = s_vc + KV_C * HEAD_DIM_C
    segs = [(s_qb, s_kvb), (s_qc, s_kc), (0, s_kidx), (s_kidx, s_widx), (s_kvb, s_gb),
            (s_kc, s_vc), (s_vc, s_gm), (s_widx, s_qb), (s_gb, s_qc)]
    return segs, s_gm


def _small_weights_t(w_in_t, li):
    segs, s_gm = _in_splits()
    d = w_in_t.shape[-1]
    parts = [w_in_t[li, a:b] for a, b in segs] + [jnp.zeros((N_SMALL - s_gm, d), w_in_t.dtype)]
    return jnp.concatenate(parts, axis=0)


def _layer(li, x, xb, p_b, w_in_t, g_cq, g_ckv, g_kidx, b_kidx, w_qidx, w_uq, w_uk, w_uv,
           cmp_pe, cmp_w1, cmp_w2, sinks, w_br, w_o, ln1_g, ln1_b,
           w_up, conv_w, conv_b, w_down_b, w_pg, b_pg, w_pp, ln2_g, ln2_b, consts, *, bsz, seq, alpha):
    m, d = x.shape
    z, z_b = _mm(xb, _small_weights_t(w_in_t, li), tm=1024, tn=512, also=BF, trans_b=True, name="in_proj")
    cq_n, ckv_n, kidx_n = _prep(z, g_cq[li], g_ckv[li], g_kidx[li], b_kidx[li])
    w_qq = jnp.concatenate([w_qidx[li], w_uq[li]], axis=1).astype(BF)
    qq = _mm(cq_n, w_qq, tm=1024, tn=1280, out_dtype=BF, name="q_proj")

    z3 = z.reshape(bsz, seq, N_SMALL)
    b3 = lambda a: a.reshape(bsz, seq, a.shape[-1])
    zb3 = z_b.reshape(bsz, seq, N_SMALL)
    seg = lambda off, w: zb3[:, :, off:off + w]

    o_a = _dsa(b3(qq), z3, b3(kidx_n), b3(ckv_n), w_uk[li], w_uv[li], consts["near_a"], consts["far_a"], seq=seq)

    gw = G_B * HEAD_DIM
    kcv = _compress(z3, (cmp_pe, li), (cmp_w1, li), (cmp_w2, li), seq=seq)
    o_b = _nsa(z3, kcv, seg(OFF_KVB + 2 * gw, gw), seg(OFF_KVB + 3 * gw, gw), seg(OFF_KVB + 4 * gw, gw),
               seg(OFF_KVB + 5 * gw, gw), consts["et_pad"], consts["overlap_t"],
               consts["near_s"], consts["near_w"], consts["far_s"], seq=seq)

    kvw = KV_C * HEAD_DIM_C
    o_c = _swa(z3, seg(OFF_KC, kvw), seg(OFF_VC, kvw), consts["near_c"], sinks[li], seq=seq)

    o_all = jnp.stack([o_a.reshape(m, BRANCH_W), o_b.reshape(m, BRANCH_W), o_c.reshape(m, BRANCH_W)])
    merged = _merge(xb, w_in_t, li, _in_splits()[1], o_all, (w_br, li), tm=1024, tn=512)
    x1, x1b = _resln(merged, (w_o, li), x, ln1_g[li], ln1_b[li], alpha=alpha, tm=1024, tn=256, name="attn_out_ln1")

    a = _ffn_up(x1b, (w_up, li), (conv_w, li), (conv_b, li), seq=seq, tm=min(1024, seq), tn=256)
    ffn = _mm(a, (w_down_b, li), tm=512, tn=512, name="ffn_down")
    return _resln(x1b, (w_pg, li), x1, ln2_g[li], ln2_b[li], alpha=alpha, tm=1024, tn=256,
                  gated=(b_pg[li], p_b, (w_pp, li), ffn), name="ple_ln2")


def kernel(x, p, w_in, g_cq, g_ckv, g_kidx, b_kidx, w_qidx, w_uq, w_uk, w_uv, cmp_pe, cmp_w1, cmp_w2, sinks, w_br, w_o, ln1_g, ln1_b, w_up, conv_w, conv_b, w_down, w_pg, b_pg, w_pp, ln2_g, ln2_b, rel_bias):
    bsz, seq, d = x.shape
    depth = p.shape[0]
    alpha = (2 * depth) ** 0.25
    m = bsz * seq
    assert seq % TK == 0 and d % 512 == 0 and seq + TQ <= 2 ** TIE_POS_BITS

    key = np.arange(seq)[:, None]
    blk = np.arange(LANES)[None, :]
    et = (key // SEL_BLOCK == blk).astype(np.float32)
    et_pad = jnp.asarray(np.concatenate([np.zeros((TQ, LANES), np.float32), et]), BF)
    ncp = seq // CMP_STRIDE
    cs = np.arange(ncp)[:, None] * CMP_STRIDE
    bs = blk * SEL_BLOCK
    n_cmp = (seq - CMP_BLOCK) // CMP_STRIDE + 1
    ov = ((cs < bs + SEL_BLOCK) & (cs + CMP_BLOCK > bs) & (blk < seq // SEL_BLOCK)
          & (np.arange(ncp)[:, None] < n_cmp)).astype(np.float32)
    tab_a, tab_b, tab_c = rel_bias[:, :H_A], rel_bias[:, H_A:H_A + H_B], rel_bias[:, H_A + H_B:]
    consts = {"et_pad": et_pad, "overlap_t": jnp.asarray(ov.T, BF),
              "near_a": _near_bias(tab_a, TQ, 2 * TQ), "far_a": tab_a[N_BUCKETS - 1],
              "near_s": _near_bias(tab_b, TQ, 2 * TQ), "near_w": _near_bias(tab_b, WIN_B, WIN_B + TQ),
              "far_s": tab_b[N_BUCKETS - 1], "near_c": _near_bias(tab_c, WIN_C, 2 * TQ)}

    xf = x.reshape(m, d)
    xb = xf.astype(BF)
    w_down_b = w_down.astype(BF)
    w_in_t = jnp.swapaxes(w_in, 1, 2)
    for li in range(depth):
        xf, xb = _layer(li, xf, xb, p[li].reshape(m, -1).astype(BF), w_in_t, g_cq, g_ckv, g_kidx, b_kidx,
                        w_qidx, w_uq, w_uk, w_uv, cmp_pe, cmp_w1, cmp_w2, sinks,
                        w_br, w_o, ln1_g, ln1_b, w_up, conv_w, conv_b, w_down_b,
                        w_pg, b_pg, w_pp, ln2_g, ln2_b, consts, bsz=bsz, seq=seq, alpha=alpha)
    return xf.reshape(bsz, seq, d)
```

```python
import functools
import math

import jax
import jax.numpy as jnp
import numpy as np
from jax import lax
from jax.experimental import pallas as pl
from jax.experimental.pallas import tpu as pltpu

HEAD_DIM = 128
NEG = -1e30
FORCE_SCORE = 1e4
LN_EPS = 1e-5
N_BUCKETS = 32
MAX_DISTANCE = 128
H_A = 8
Q_LORA = 768
KV_LORA = 256
H_IDX = 32
D_IDX = 128
TOPK_MAX = 256
H_B = 8
G_B = 2
CMP_BLOCK = 32
CMP_STRIDE = 16
CMP_HIDDEN = 128
SEL_BLOCK = 64
N_SEL = 16
WIN_B = 512
H_C = 16
KV_C = 2
HEAD_DIM_C = 64
WIN_C = 128
N_BRANCH = 3
BRANCH_W = 1024
CONV_W = 3

LANES = 128
TQ = 128
TK = 512
VMEM_LIMIT = 56 * 1024 * 1024
INT_MIN = -2 ** 31

OFF_QB = 0
OFF_QC = 1024
OFF_CQ = 2048
OFF_CKV = 2816
OFF_KIDX = 3072
OFF_KVB = 3200
OFF_KC = 4736
OFF_VC = 4864
OFF_WIDX = 4992
N_SMALL = 5120

BF = jnp.bfloat16
F32 = jnp.float32


def _cparams(sem, vmem=VMEM_LIMIT):
    return pltpu.CompilerParams(dimension_semantics=sem, vmem_limit_bytes=vmem)


def _dot(a, b):
    return jnp.dot(a, b, preferred_element_type=F32)


def _layered(w):
    return (w[0], (w[1],)) if isinstance(w, tuple) else (w, ())


def _wspec(lead, block, imap):
    return pl.BlockSpec((None,) * len(lead) + block, lambda *g: lead + imap(*g))


def _dot_t(a, b):
    return lax.dot_general(a, b, (((1,), (1,)), ((), ())), preferred_element_type=F32)


def _mm_kernel(a_ref, b_ref, *refs, nk, trans_b):
    *o_refs, acc_ref = refs
    b = b_ref[...].astype(BF)
    part = _dot_t(a_ref[...], b) if trans_b else _dot(a_ref[...], b)

    def emit(val):
        for o_ref in o_refs:
            o_ref[...] = val.astype(o_ref.dtype)

    if nk == 1:
        emit(part)
    else:
        k = pl.program_id(2)

        @pl.when(k == 0)
        def _():
            acc_ref[...] = part

        @pl.when(k > 0)
        def _():
            acc_ref[...] += part

        @pl.when(k == nk - 1)
        def _():
            emit(acc_ref[...])


def _mm(a, b, *, tm, tn, tk=None, out_dtype=F32, also=None, trans_b=False, name="mm"):
    m, kdim = a.shape
    b, lead = _layered(b)
    n = b.shape[-2] if trans_b else b.shape[-1]
    tk = kdim if tk is None else tk
    nk = kdim // tk
    assert m % tm == 0 and kdim % tk == 0
    acc_shape = (tm, tn) if nk > 1 else (8, LANES)
    b_spec = (_wspec(lead, (tn, tk), lambda i, j, k: (j, k)) if trans_b
              else _wspec(lead, (tk, tn), lambda i, j, k: (k, j)))
    dtypes = [out_dtype] + ([also] if also is not None else [])
    out = pl.pallas_call(
        functools.partial(_mm_kernel, nk=nk, trans_b=trans_b),
        grid=(m // tm, pl.cdiv(n, tn), nk),
        in_specs=[pl.BlockSpec((tm, tk), lambda i, j, k: (i, k)), b_spec],
        out_specs=[pl.BlockSpec((tm, tn), lambda i, j, k: (i, j)) for _ in dtypes],
        out_shape=[jax.ShapeDtypeStruct((m, n), dt) for dt in dtypes],
        scratch_shapes=[pltpu.VMEM(acc_shape, F32)],
        compiler_params=_cparams(("parallel", "parallel", "arbitrary")),
        name=name,
    )(a, b)
    return out if also is not None else out[0]


def _prep_kernel(c_ref, k_ref, gq_ref, gkv_ref, gk_ref, bk_ref, cq_o, ckv_o, kidx_o):
    c = c_ref[...]
    cq = c[:, :Q_LORA]
    ckv = c[:, Q_LORA:]
    cq = cq * lax.rsqrt(jnp.mean(cq * cq, axis=-1, keepdims=True) + LN_EPS) * gq_ref[...]
    ckv = ckv * lax.rsqrt(jnp.mean(ckv * ckv, axis=-1, keepdims=True) + LN_EPS) * gkv_ref[...]
    k = k_ref[...]
    mu = jnp.mean(k, axis=-1, keepdims=True)
    kc = k - mu
    var = jnp.mean(kc * kc, axis=-1, keepdims=True)
    kn = kc * lax.rsqrt(var + LN_EPS) * gk_ref[...] + bk_ref[...]
    cq_o[...] = cq.astype(BF)
    ckv_o[...] = ckv.astype(BF)
    kidx_o[...] = kn.astype(BF)


def _prep(z, g_cq, g_ckv, g_kidx, b_kidx, *, tm=512):
    m = z.shape[0]
    wc = Q_LORA + KV_LORA
    row = lambda a: a.reshape(1, -1)
    full = lambda n: pl.BlockSpec((1, n), lambda i: (0, 0))
    return pl.pallas_call(
        _prep_kernel,
        grid=(m // tm,),
        in_specs=[pl.BlockSpec((tm, wc), lambda i: (i, OFF_CQ // wc)),
                  pl.BlockSpec((tm, D_IDX), lambda i: (i, OFF_KIDX // D_IDX)),
                  full(Q_LORA), full(KV_LORA), full(D_IDX), full(D_IDX)],
        out_specs=[pl.BlockSpec((tm, Q_LORA), lambda i: (i, 0)),
                   pl.BlockSpec((tm, KV_LORA), lambda i: (i, 0)),
                   pl.BlockSpec((tm, D_IDX), lambda i: (i, 0))],
        out_shape=[jax.ShapeDtypeStruct((m, Q_LORA), BF),
                   jax.ShapeDtypeStruct((m, KV_LORA), BF),
                   jax.ShapeDtypeStruct((m, D_IDX), BF)],
        compiler_params=_cparams(("parallel",)),
        name="prep_norms",
    )(z, z, row(g_cq), row(g_ckv), row(g_kidx), row(b_kidx))


def _to_key(x):
    b = lax.bitcast_convert_type(x, jnp.int32)
    return b ^ ((b >> 31) & jnp.int32(0x7FFFFFFF))


def _kth_largest_key(count_ge, k, shape):
    kf = jnp.float32(k)
    zero = jnp.zeros(shape, jnp.int32)
    t0 = jnp.where(count_ge(zero) >= kf, zero, jnp.full(shape, INT_MIN, jnp.int32))

    def body(it, t):
        cand = t + jnp.left_shift(jnp.int32(1), jnp.int32(30) - it)
        return jnp.where(count_ge(cand) >= kf, cand, t)

    t = lax.fori_loop(0, 31, body, t0)
    return jnp.maximum(t, jnp.int32(INT_MIN + 1))


TIE_POS_BITS = 14
FOLD_ROWS = 64


def _fold_rows(x):
    n, w = x.shape
    return x.reshape(n // FOLD_ROWS, FOLD_ROWS, w).sum(axis=0)


MASKED = 2.0 * NEG


def _online_update_t(m_ref, l_ref, acc_ref, slots, logits, masks, v_ts):
    scaled = []
    for sl, s, mask in zip(slots, logits, masks):
        s = jnp.where(mask, s, MASKED)
        m_old = m_ref[sl]
        m_new = jnp.maximum(m_old, jnp.max(s, axis=0, keepdims=True))
        alpha = jnp.exp(m_old - m_new)
        p = jnp.exp(s - m_new)
        l_ref[sl] = alpha * l_ref[sl] + jnp.sum(p, axis=0, keepdims=True)
        m_ref[sl] = m_new
        scaled.append((alpha, p.astype(BF)))
    for sl, (alpha, p), v_t in zip(slots, scaled, v_ts):
        acc_ref[sl] = alpha * acc_ref[sl] + _dot(v_t, p)


def _dsa_kernel(qi_ref, q_ref, w_ref, kidx_ref, ckv_ref, kvtf_ref, kvtn_ref, wuk_ref, wuvt_ref, neart_ref,
                o_ref, keyf_ref, keyn_ref, qstk_ref, qlstk_ref, m_ref, l_ref, acc_ref, *, topk):
    i = pl.program_id(1)
    tq = TQ
    npair = H_A // 2
    far_end = jnp.maximum(i - 1, 0) * tq
    nfar = (far_end + TK - 1) // TK
    row0 = pl.multiple_of(i * tq, tq)

    for h in range(H_IDX):
        qstk_ref[h * tq:(h + 1) * tq, :] = qi_ref[:, h * D_IDX:(h + 1) * D_IDX]
    w_t = (w_ref[...] * (D_IDX ** -0.5 * H_IDX ** -0.5)).T

    def scores_t(krows):
        acc = jnp.zeros((krows.shape[0], tq), F32)
        for hp in range(H_IDX // 2):
            s = _dot_t(krows, qstk_ref[2 * hp * tq:(2 * hp + 2) * tq, :])
            acc = (acc + jnp.maximum(s[:, :tq], 0.0) * w_t[2 * hp:2 * hp + 1, :]
                   + jnp.maximum(s[:, tq:], 0.0) * w_t[2 * hp + 1:2 * hp + 2, :])
        return acc

    rrow = lax.broadcasted_iota(jnp.int32, (2 * tq, tq), 0)
    ccol = lax.broadcasted_iota(jnp.int32, (2 * tq, tq), 1)
    valid_n = (tq + ccol - rrow >= 0) & ((i - 1) * tq + rrow >= 0)
    keyn_ref[...] = jnp.where(valid_n, _to_key(scores_t(kidx_ref[pl.ds(row0, 2 * tq), :])), jnp.int32(INT_MIN))

    def far_scores(kt, c):
        r0 = pl.multiple_of(tq + kt * TK, tq)
        s = scores_t(kidx_ref[pl.ds(r0, TK), :])
        pos = kt * TK + lax.broadcasted_iota(jnp.int32, (TK, tq), 0)
        keyf_ref[kt] = jnp.where(pos < far_end, _to_key(s), jnp.int32(INT_MIN))
        return c

    lax.fori_loop(0, nfar, far_scores, 0)

    def count_ge(cand):
        part = _fold_rows(jnp.where(keyn_ref[...] >= cand, 1.0, 0.0))

        def body(kt, a):
            return a + _fold_rows(jnp.where(keyf_ref[kt] >= cand, 1.0, 0.0))

        part = lax.fori_loop(0, nfar, body, part)
        return jnp.sum(part, axis=0, keepdims=True)

    thr = _kth_largest_key(count_ge, topk, (1, tq))

    def count2(pred):
        upos_n = lax.broadcasted_iota(jnp.int32, (2 * tq, tq), 0) + (i - 1) * tq + tq
        part = _fold_rows(jnp.where(pred(keyn_ref[...], upos_n), 1.0, 0.0))

        def body(kt, a):
            upos = kt * TK + tq + lax.broadcasted_iota(jnp.int32, (TK, tq), 0)
            return a + _fold_rows(jnp.where(pred(keyf_ref[kt], upos), 1.0, 0.0))

        return jnp.sum(lax.fori_loop(0, nfar, body, part), axis=0, keepdims=True)

    n_gt = count2(lambda kk, up: kk > thr)
    n_eq = count2(lambda kk, up: kk == thr)
    need = jnp.float32(topk) - n_gt
    surplus = n_eq > need

    @pl.when(jnp.max(jnp.where(surplus, 1.0, 0.0)) > 0.0)
    def _():
        def body(it, r):
            cand = r + jnp.left_shift(jnp.int32(1), jnp.int32(TIE_POS_BITS - 1) - it)
            below = count2(lambda kk, up: (kk == thr) & (up < cand))
            return jnp.where(below < need, cand, r)

        last = lax.fori_loop(0, TIE_POS_BITS, body, jnp.zeros((1, tq), jnp.int32))
        last = jnp.where(surplus, last, jnp.int32(2 ** TIE_POS_BITS))
        upos_n = lax.broadcasted_iota(jnp.int32, (2 * tq, tq), 0) + (i - 1) * tq + tq
        kn = keyn_ref[...]
        keyn_ref[...] = jnp.where((kn == thr) & (upos_n > last), jnp.int32(INT_MIN), kn)

        def demote(kt, c):
            upos = kt * TK + tq + lax.broadcasted_iota(jnp.int32, (TK, tq), 0)
            kf = keyf_ref[kt]
            keyf_ref[kt] = jnp.where((kf == thr) & (upos > last), jnp.int32(INT_MIN), kf)
            return c

        lax.fori_loop(0, nfar, demote, 0)

    for h in range(H_A):
        ql = _dot(q_ref[:, h * HEAD_DIM:(h + 1) * HEAD_DIM], wuk_ref[h]) * (HEAD_DIM ** -0.5)
        qlstk_ref[h * tq:(h + 1) * tq, :] = ql.astype(BF)

    for hp in range(npair):
        m_ref[hp] = jnp.full((1, 2 * tq), NEG, F32)
        l_ref[hp] = jnp.zeros((1, 2 * tq), F32)
        acc_ref[hp] = jnp.zeros((KV_LORA, 2 * tq), F32)

    pairs = range(npair)

    def update(kv, kv_t, bias, sel):
        sel2 = jnp.concatenate([sel, sel], axis=1)
        logits = [_dot_t(kv, qlstk_ref[2 * hp * tq:(2 * hp + 2) * tq, :]) for hp in pairs]
        if bias is not None:
            logits = [s + bias[hp] for hp, s in zip(pairs, logits)]
        _online_update_t(m_ref, l_ref, acc_ref, pairs, logits, [sel2] * npair, [kv_t] * npair)

    update(ckv_ref[pl.ds(row0, 2 * tq), :], jnp.concatenate([kvtn_ref[i], kvtn_ref[i + 1]], axis=1),
           neart_ref, keyn_ref[...] >= thr)

    def far_attn(kt, c):
        r0 = pl.multiple_of(tq + kt * TK, tq)
        update(ckv_ref[pl.ds(r0, TK), :], kvtf_ref[kt], None, keyf_ref[kt] >= thr)
        return c

    lax.fori_loop(0, nfar, far_attn, 0)

    for hp in range(npair):
        l = l_ref[hp]
        o_lat_t = (acc_ref[hp] * jnp.where(l > 0.0, 1.0 / l, 0.0)).astype(BF)
        for hh in range(2):
            h = 2 * hp + hh
            o_t = _dot(wuvt_ref[h], o_lat_t[:, hh * tq:(hh + 1) * tq])
            o_ref[:, h * HEAD_DIM:(h + 1) * HEAD_DIM] = o_t.T.astype(o_ref.dtype)


def _dsa(qq, z, kidx_n, ckv_n, w_uk, w_uv, near_a, far_a, *, seq):
    bsz = qq.shape[0]
    topk = min(TOPK_MAX, seq // 4)
    nqi = H_IDX * D_IDX
    npair = H_A // 2
    kidx_pad = _pad_front(kidx_n, TQ)
    ckv_pad = _pad_front(ckv_n, TQ)
    kvt_far = jnp.transpose(ckv_n.reshape(bsz, seq // TK, TK, KV_LORA), (0, 1, 3, 2))
    kvt_near = jnp.transpose(ckv_pad.reshape(bsz, seq // TQ + 1, TQ, KV_LORA), (0, 1, 3, 2))
    wuk_t = jnp.transpose(w_uk, (1, 2, 0)).astype(BF)
    wuv_t = jnp.transpose(w_uv, (1, 2, 0)).astype(BF)
    near_t = jnp.transpose(jnp.transpose(near_a - far_a[:, None, None], (0, 2, 1)).reshape(npair, 2, 2 * TQ, TQ),
                           (0, 2, 1, 3)).reshape(npair, 2 * TQ, 2 * TQ)
    whole = lambda shape: pl.BlockSpec(shape, lambda b, i: (0,) * len(shape))
    perb = lambda shape: pl.BlockSpec((None,) + shape, lambda b, i: (b,) + (0,) * len(shape))
    return pl.pallas_call(
        functools.partial(_dsa_kernel, topk=topk),
        grid=(bsz, seq // TQ),
        in_specs=[pl.BlockSpec((None, TQ, nqi), lambda b, i: (b, i, 0)),
                  pl.BlockSpec((None, TQ, H_A * HEAD_DIM), lambda b, i: (b, i, nqi // (H_A * HEAD_DIM))),
                  pl.BlockSpec((None, TQ, LANES), lambda b, i: (b, i, OFF_WIDX // LANES)),
                  perb((seq + TQ, D_IDX)), perb((seq + TQ, KV_LORA)),
                  perb((seq // TK, KV_LORA, TK)), perb((seq // TQ + 1, KV_LORA, TQ)),
                  whole((H_A, HEAD_DIM, KV_LORA)), whole((H_A, HEAD_DIM, KV_LORA)),
                  whole((npair, 2 * TQ, 2 * TQ))],
        out_specs=pl.BlockSpec((None, TQ, H_A * HEAD_DIM), lambda b, i: (b, i, 0)),
        out_shape=jax.ShapeDtypeStruct((bsz, seq, H_A * HEAD_DIM), BF),
        scratch_shapes=[pltpu.VMEM((seq // TK, TK, TQ), jnp.int32),
                        pltpu.VMEM((2 * TQ, TQ), jnp.int32),
                        pltpu.VMEM((H_IDX * TQ, D_IDX), BF),
                        pltpu.VMEM((H_A * TQ, KV_LORA), BF),
                        pltpu.VMEM((npair, 1, 2 * TQ), F32),
                        pltpu.VMEM((npair, 1, 2 * TQ), F32),
                        pltpu.VMEM((npair, KV_LORA, 2 * TQ), F32)],
        compiler_params=_cparams(("parallel", "parallel")),
        name="dsa_mixer",
    )(qq, qq, z, kidx_pad, ckv_pad, kvt_far, kvt_near, wuk_t, wuv_t, near_t)


def _cmp_kernel(x_ref, pe_ref, w1_ref, w2_ref, o_ref, xp_ref, *, seq, ncp):
    xp_ref[0:seq, :] = x_ref[...]
    xp_ref[seq:seq + CMP_STRIDE, :] = jnp.zeros((CMP_STRIDE, HEAD_DIM), F32)
    acc = jnp.zeros((ncp, CMP_HIDDEN), F32)
    for l in range(CMP_BLOCK):
        rows = xp_ref[pl.ds(l, ncp, stride=CMP_STRIDE), :]
        blk = rows + pe_ref[l:l + 1, :]
        acc = acc + _dot(blk.astype(BF), w1_ref[l].astype(BF))
    hdn = jax.nn.gelu(acc)
    o_ref[...] = _dot(hdn.astype(BF), w2_ref[...].astype(BF)).astype(o_ref.dtype)


def _compress(z, cmp_pe, cmp_w1, cmp_w2, *, seq):
    bsz = z.shape[0]
    ncp = seq // CMP_STRIDE
    cmp_pe, lead_pe = _layered(cmp_pe)
    cmp_w1, lead_w1 = _layered(cmp_w1)
    cmp_w2, lead_w2 = _layered(cmp_w2)
    return pl.pallas_call(
        functools.partial(_cmp_kernel, seq=seq, ncp=ncp),
        grid=(bsz, 2, G_B),
        in_specs=[pl.BlockSpec((None, seq, HEAD_DIM), lambda b, c, g: (b, 0, OFF_KVB // HEAD_DIM + c * G_B + g)),
                  _wspec(lead_pe, (None, CMP_BLOCK, HEAD_DIM), lambda b, c, g: (c, 0, 0)),
                  _wspec(lead_w1, (None, CMP_BLOCK, HEAD_DIM, CMP_HIDDEN), lambda b, c, g: (c, 0, 0, 0)),
                  _wspec(lead_w2, (None, CMP_HIDDEN, HEAD_DIM), lambda b, c, g: (c, 0, 0))],
        out_specs=pl.BlockSpec((None, None, None, ncp, HEAD_DIM), lambda b, c, g: (b, c, g, 0, 0)),
        out_shape=jax.ShapeDtypeStruct((bsz, 2, G_B, ncp, HEAD_DIM), BF),
        scratch_shapes=[pltpu.VMEM((seq + CMP_STRIDE, HEAD_DIM), F32)],
        compiler_params=_cparams(("parallel", "parallel", "parallel")),
        name="nsa_compress",
    )(z, cmp_pe, cmp_w1, cmp_w2)


def _nsa_kernel(q_ref, gb_ref, kc_ref, vct_ref, ksl_ref, vstf_ref, vstn_ref, kw_ref, vwt_ref, et_ref, ovt_ref,
                nearts_ref, neartw_ref, o_ref, qstk_ref, imp_ref, mselt_ref, m_ref, l_ref, acc_ref, ot_ref, *, seq, npick):
    i = pl.program_id(1)
    tq = TQ
    hpg = H_B // G_B
    gl = hpg * tq
    ncp = seq // CMP_STRIDE
    nblk = SEL_BLOCK
    far_end = jnp.maximum(i - 1, 0) * tq
    nfar = (far_end + TK - 1) // TK
    row0 = pl.multiple_of(i * tq, tq)
    tile4 = lambda x: jnp.concatenate([x] * hpg, axis=1)
    tlane = i * tq + lax.broadcasted_iota(jnp.int32, (1, tq), 1)

    for h in range(H_B):
        qstk_ref[h * tq:(h + 1) * tq, :] = (q_ref[:, h * HEAD_DIM:(h + 1) * HEAD_DIM] * (HEAD_DIM ** -0.5)).astype(BF)
    sig_t = jax.nn.sigmoid(gb_ref[...]).T

    def gate_row(g, c):
        return jnp.concatenate([sig_t[H_IDX + 3 * (g * hpg + j) + c:H_IDX + 3 * (g * hpg + j) + c + 1, :]
                                for j in range(hpg)], axis=1)

    qgrp = lambda g: qstk_ref[g * gl:(g + 1) * gl, :]
    grows = lambda g: slice(g * HEAD_DIM, (g + 1) * HEAD_DIM)

    nrow = lax.broadcasted_iota(jnp.int32, (ncp, tq), 0)
    cmp_mask = tile4(nrow * CMP_STRIDE + (CMP_BLOCK - 1) <= tlane)
    anyc = tile4(jnp.where(tlane >= CMP_BLOCK - 1, 1.0, 0.0))
    brow = lax.broadcasted_iota(jnp.int32, (nblk, tq), 0)
    cur = tlane // SEL_BLOCK
    forced = (brow == 0) | (brow == cur) | (brow == cur - 1)
    sub = lax.broadcasted_iota(jnp.int32, (8, tq), 0)
    for g in range(G_B):
        s = jnp.where(cmp_mask, _dot_t(kc_ref[g], qgrp(g)), NEG)
        e = jnp.exp(s - jnp.max(s, axis=0, keepdims=True))
        p = e * (anyc / jnp.sum(e, axis=0, keepdims=True))
        ot_ref[g] = gate_row(g, 0) * _dot(vct_ref[g], p.astype(BF))
        psum = p[:, :tq]
        for j in range(1, hpg):
            psum = psum + p[:, j * tq:(j + 1) * tq]
        hi = psum.astype(BF)
        lo = (psum - hi.astype(F32)).astype(BF)
        imp = (_dot(ovt_ref[...], hi) + _dot(ovt_ref[...], lo))[:nblk]
        imp = jnp.where(brow > cur, NEG, jnp.where(forced, FORCE_SCORE, imp))
        imp_ref[...] = imp
        vals = [imp[8 * r:8 * r + 8] for r in range(nblk // 8)]
        rank = [jnp.zeros((8, tq), F32) for _ in vals]
        for mp in range(nblk):
            other = jnp.broadcast_to(imp_ref[mp:mp + 1, :], (8, tq))
            for r, v in enumerate(vals):
                if 8 * r > mp:
                    beats = other >= v
                elif 8 * r + 7 < mp:
                    beats = other > v
                else:
                    beats = (other > v) | ((other == v) & (sub + 8 * r > mp))
                rank[r] = rank[r] + jnp.where(beats, 1.0, 0.0)
        sel = jnp.concatenate([jnp.where(rk < npick, 1.0, 0.0) for rk in rank], axis=0)
        mselt_ref[g] = jnp.concatenate([sel, jnp.zeros((LANES - nblk, tq), F32)], axis=0).astype(BF)

    rrow = lax.broadcasted_iota(jnp.int32, (2 * tq, tq), 0)
    ccol = lax.broadcasted_iota(jnp.int32, (2 * tq, tq), 1)
    causal_n = tq + ccol - rrow >= 0
    etn = et_ref[pl.ds(row0, 2 * tq), :]
    groups = range(G_B)
    for g in groups:
        m_ref[g] = jnp.full((1, gl), NEG, F32)
        l_ref[g] = jnp.zeros((1, gl), F32)
        acc_ref[g] = jnp.zeros((HEAD_DIM, gl), F32)
    _online_update_t(
        m_ref, l_ref, acc_ref, groups,
        [_dot_t(ksl_ref[pl.ds(row0, 2 * tq), grows(g)], qgrp(g)) + nearts_ref[g] for g in groups],
        [tile4(causal_n & (_dot(etn, mselt_ref[g]) > 0.5)) for g in groups],
        [jnp.concatenate([vstn_ref[i, grows(g), :], vstn_ref[i + 1, grows(g), :]], axis=1) for g in groups])

    def far_attn(kt, c):
        r0 = pl.multiple_of(tq + kt * TK, tq)
        ett = et_ref[pl.ds(r0, TK), :]
        infar = kt * TK + lax.broadcasted_iota(jnp.int32, (TK, tq), 0) < far_end
        _online_update_t(
            m_ref, l_ref, acc_ref, groups,
            [_dot_t(ksl_ref[pl.ds(r0, TK), grows(g)], qgrp(g)) for g in groups],
            [tile4(infar & (_dot(ett, mselt_ref[g]) > 0.5)) for g in groups],
            [vstf_ref[kt, grows(g), :] for g in groups])
        return c

    lax.fori_loop(0, nfar, far_attn, 0)
    for g in groups:
        l = l_ref[g]
        ot_ref[g] += gate_row(g, 1) * (acc_ref[g] * jnp.where(l > 0.0, 1.0 / l, 0.0))

    span = WIN_B + tq
    rw = lax.broadcasted_iota(jnp.int32, (span, tq), 0)
    cw = lax.broadcasted_iota(jnp.int32, (span, tq), 1)
    distw = WIN_B + cw - rw
    mask_w = tile4((distw >= 0) & (distw < WIN_B) & (i * tq - WIN_B + rw >= 0))
    for g in groups:
        s = jnp.where(mask_w, _dot_t(kw_ref[pl.ds(row0, span), grows(g)], qgrp(g)) + neartw_ref[g], NEG)
        e = jnp.exp(s - jnp.max(s, axis=0, keepdims=True))
        p = e / jnp.sum(e, axis=0, keepdims=True)
        vw_t = jnp.concatenate([vwt_ref[i + k, grows(g), :] for k in range(span // tq)], axis=1)
        ot_ref[g] += gate_row(g, 2) * _dot(vw_t, p.astype(BF))

    for h in range(H_B):
        g, j = divmod(h, hpg)
        o_ref[:, h * HEAD_DIM:(h + 1) * HEAD_DIM] = ot_ref[g][:, j * tq:(j + 1) * tq].T.astype(o_ref.dtype)


def _tiles_t(a, rows):
    bsz, n, w = a.shape
    return jnp.transpose(a.reshape(bsz, n // rows, rows, w), (0, 1, 3, 2))


def _lanes_by_group(near, groups):
    h, tq, w = near.shape
    t = jnp.transpose(near, (0, 2, 1)).reshape(groups, h // groups, w, tq)
    return jnp.transpose(t, (0, 2, 1, 3)).reshape(groups, w, (h // groups) * tq)


def _nsa(z, kcv, k_slc, v_slc, k_win, v_win, et_pad, overlap_t, near_s, near_w, far_s, *, seq):
    bsz = z.shape[0]
    ncp = seq // CMP_STRIDE
    npick = min(N_SEL, seq // SEL_BLOCK)
    assert seq // SEL_BLOCK <= SEL_BLOCK
    gw = G_B * HEAD_DIM
    hpg = H_B // G_B
    gl = hpg * TQ
    kc = kcv[:, 0]
    vc_t = jnp.transpose(kcv[:, 1], (0, 1, 3, 2))
    whole = lambda shape: pl.BlockSpec(shape, lambda b, i: (0,) * len(shape))
    perb = lambda shape: pl.BlockSpec((None,) + shape, lambda b, i: (b,) + (0,) * len(shape))
    return pl.pallas_call(
        functools.partial(_nsa_kernel, seq=seq, npick=npick),
        grid=(bsz, seq // TQ),
        in_specs=[pl.BlockSpec((None, TQ, H_B * HEAD_DIM), lambda b, i: (b, i, OFF_QB // (H_B * HEAD_DIM))),
                  pl.BlockSpec((None, TQ, LANES), lambda b, i: (b, i, OFF_WIDX // LANES)),
                  perb((G_B, ncp, HEAD_DIM)), perb((G_B, HEAD_DIM, ncp)),
                  perb((seq + TQ, gw)), perb((seq // TK, gw, TK)), perb((seq // TQ + 1, gw, TQ)),
                  perb((seq + WIN_B, gw)), perb(((seq + WIN_B) // TQ, gw, TQ)),
                  whole((seq + TQ, LANES)), whole((LANES, ncp)),
                  whole((G_B, 2 * TQ, gl)), whole((G_B, WIN_B + TQ, gl))],
        out_specs=pl.BlockSpec((None, TQ, H_B * HEAD_DIM), lambda b, i: (b, i, 0)),
        out_shape=jax.ShapeDtypeStruct((bsz, seq, H_B * HEAD_DIM), BF),
        scratch_shapes=[pltpu.VMEM((H_B * TQ, HEAD_DIM), BF),
                        pltpu.VMEM((SEL_BLOCK, TQ), F32),
                        pltpu.VMEM((G_B, LANES, TQ), BF),
                        pltpu.VMEM((G_B, 1, gl), F32),
                        pltpu.VMEM((G_B, 1, gl), F32),
                        pltpu.VMEM((G_B, HEAD_DIM, gl), F32),
                        pltpu.VMEM((G_B, HEAD_DIM, gl), F32)],
        compiler_params=_cparams(("parallel", "parallel")),
        name="nsa_mixer",
    )(z, z, kc, vc_t, _pad_front(k_slc, TQ), _tiles_t(v_slc, TK), _tiles_t(_pad_front(v_slc, TQ), TQ),
      _pad_front(k_win, WIN_B), _tiles_t(_pad_front(v_win, WIN_B), TQ), et_pad, overlap_t,
      _lanes_by_group(near_s - far_s[:, None, None], G_B), _lanes_by_group(near_w, G_B))


def _swa_kernel(sink_ref, q_ref, k_ref, vt_ref, neart_ref, o_ref, qstk_ref):
    i = pl.program_id(1)
    tq = TQ
    hpg = H_C // KV_C
    gl = hpg * tq
    row0 = pl.multiple_of(i * tq, tq)
    rrow = lax.broadcasted_iota(jnp.int32, (2 * tq, tq), 0)
    ccol = lax.broadcasted_iota(jnp.int32, (2 * tq, tq), 1)
    dist = WIN_C + ccol - rrow
    mask = jnp.concatenate([(dist >= 0) & (dist < WIN_C) & (i * tq - WIN_C + rrow >= 0)] * hpg, axis=1)
    for h in range(H_C):
        qstk_ref[h * tq:(h + 1) * tq, :] = (q_ref[:, h * HEAD_DIM_C:(h + 1) * HEAD_DIM_C]
                                            * (HEAD_DIM_C ** -0.5)).astype(BF)
    kk = k_ref[pl.ds(row0, 2 * tq), :]
    vt = jnp.concatenate([vt_ref[i], vt_ref[i + 1]], axis=1)
    for g in range(KV_C):
        cols = slice(g * HEAD_DIM_C, (g + 1) * HEAD_DIM_C)
        s = jnp.where(mask, _dot_t(kk[:, cols], qstk_ref[g * gl:(g + 1) * gl, :]) + neart_ref[g], NEG)
        sink = sink_ref[g]
        m = jnp.maximum(jnp.max(s, axis=0, keepdims=True), sink)
        e = jnp.exp(s - m)
        p = e / (jnp.sum(e, axis=0, keepdims=True) + jnp.exp(sink - m))
        o_t = _dot(vt[cols, :], p.astype(BF))
        for j in range(hpg):
            h = g * hpg + j
            o_ref[:, h * HEAD_DIM_C:(h + 1) * HEAD_DIM_C] = o_t[:, j * tq:(j + 1) * tq].T.astype(o_ref.dtype)


def _swa(z, k_c, v_c, near_c, sinks, *, seq):
    bsz = z.shape[0]
    kvw = KV_C * HEAD_DIM_C
    hpg = H_C // KV_C
    gl = hpg * TQ
    sink_rows = jnp.repeat(sinks.reshape(KV_C, hpg), TQ, axis=1).reshape(KV_C, 1, gl)
    whole = lambda shape: pl.BlockSpec(shape, lambda b, i: (0,) * len(shape))
    return pl.pallas_call(
        _swa_kernel,
        grid=(bsz, seq // TQ),
        in_specs=[whole((KV_C, 1, gl)),
                  pl.BlockSpec((None, TQ, BRANCH_W), lambda b, i: (b, i, OFF_QC // BRANCH_W)),
                  pl.BlockSpec((None, seq + WIN_C, kvw), lambda b, i: (b, 0, 0)),
                  pl.BlockSpec((None, (seq + WIN_C) // TQ, kvw, TQ), lambda b, i: (b, 0, 0, 0)),
                  whole((KV_C, 2 * TQ, gl))],
        out_specs=pl.BlockSpec((None, TQ, BRANCH_W), lambda b, i: (b, i, 0)),
        out_shape=jax.ShapeDtypeStruct((bsz, seq, BRANCH_W), BF),
        scratch_shapes=[pltpu.VMEM((H_C * TQ, HEAD_DIM_C), BF)],
        compiler_params=_cparams(("parallel", "parallel")),
        name="swa_mixer",
    )(sink_rows, z, _pad_front(k_c, WIN_C), _tiles_t(_pad_front(v_c, WIN_C), TQ), _lanes_by_group(near_c, KV_C))


def _merge_kernel(x_ref, wg_ref, o_ref, wbr_ref, out_ref, acc_ref):
    br = pl.program_id(2)
    gate = jax.nn.sigmoid(_dot_t(x_ref[...], wg_ref[...].astype(BF)))
    val = _dot(o_ref[...], wbr_ref[...].astype(BF))

    @pl.when(br == 0)
    def _():
        acc_ref[...] = gate * val

    @pl.when(br > 0)
    def _():
        acc_ref[...] += gate * val

    @pl.when(br == N_BRANCH - 1)
    def _():
        out_ref[...] = acc_ref[...].astype(out_ref.dtype)


def _merge(xb, w_in_t, li, gate_row0, o_all, w_br, *, tm, tn):
    m, d = xb.shape
    nj = d // tn
    w_br, lead = _layered(w_br)
    depth, n_in, _ = w_in_t.shape
    row0 = li * n_in + gate_row0
    return pl.pallas_call(
        _merge_kernel,
        grid=(m // tm, nj, N_BRANCH),
        in_specs=[pl.BlockSpec((tm, d), lambda i, j, r: (i, 0)),
                  pl.BlockSpec((pl.Element(tn), pl.Element(d)),
                               lambda i, j, r: (pl.multiple_of(row0 + r * d + j * tn, 8), 0)),
                  pl.BlockSpec((None, tm, BRANCH_W), lambda i, j, r: (r, i, 0)),
                  _wspec(lead, (None, BRANCH_W, tn), lambda i, j, r: (r, 0, j))],
        out_specs=pl.BlockSpec((tm, tn), lambda i, j, r: (i, j)),
        out_shape=jax.ShapeDtypeStruct((m, d), BF),
        scratch_shapes=[pltpu.VMEM((tm, tn), F32)],
        compiler_params=_cparams(("parallel", "parallel", "arbitrary")),
        name="merge_gates",
    )(xb, w_in_t.reshape(depth * n_in, d), o_all, w_br)


def _resln_kernel(*refs, nj, tn, alpha, gated):
    if gated:
        (a_ref, w_ref, res_ref, g_ref, b_ref, bias_ref, p_ref, wp_ref, add_ref,
         of_ref, ob_ref, y_ref, mu_ref, rs_ref) = refs
    else:
        a_ref, w_ref, res_ref, g_ref, b_ref, of_ref, ob_ref, y_ref, mu_ref, rs_ref = refs
    j = pl.program_id(1)

    @pl.when(j < nj)
    def _():
        y = _dot(a_ref[...], w_ref[...].astype(BF))
        if gated:
            y = jax.nn.sigmoid(y + bias_ref[...]) * _dot(p_ref[...], wp_ref[...].astype(BF)) + add_ref[...]
        y_ref[j] = alpha * res_ref[...] + y

    @pl.when(j == nj - 1)
    def _():
        tm = y_ref.shape[1]
        d = nj * tn
        tot = jnp.zeros((tm, 1), F32)
        for c in range(nj):
            tot = tot + jnp.sum(y_ref[c], axis=-1, keepdims=True)
        mu = tot / d
        sq = jnp.zeros((tm, 1), F32)
        for c in range(nj):
            dc = y_ref[c] - mu
            sq = sq + jnp.sum(dc * dc, axis=-1, keepdims=True)
        mu_ref[...] = mu
        rs_ref[...] = lax.rsqrt(sq / d + LN_EPS)

    @pl.when(j >= nj)
    def _():
        o = (y_ref[j - nj] - mu_ref[...]) * rs_ref[...] * g_ref[...] + b_ref[...]
        of_ref[...] = o
        ob_ref[...] = o.astype(BF)


def _resln(a, w, res, g, b, *, alpha, tm, tn, gated=None, name="res_ln"):
    m, kdim = a.shape
    w, lead = _layered(w)
    d = w.shape[-1]
    nj = d // tn
    mm_col = lambda i, j: (0, jnp.minimum(j, nj - 1))
    mm_blk = lambda i, j: (i, jnp.minimum(j, nj - 1))
    out_col = lambda i, j: (0, jnp.maximum(j - nj, 0))
    out_blk = lambda i, j: (i, jnp.maximum(j - nj, 0))
    in_specs = [pl.BlockSpec((tm, kdim), lambda i, j: (i, 0)),
                _wspec(lead, (kdim, tn), mm_col),
                pl.BlockSpec((tm, tn), mm_blk),
                pl.BlockSpec((1, tn), out_col),
                pl.BlockSpec((1, tn), out_col)]
    args = [a, w, res, g.reshape(1, d), b.reshape(1, d)]
    if gated is not None:
        bias, p, wp, add = gated
        wp, lead_p = _layered(wp)
        dp = p.shape[1]
        in_specs += [pl.BlockSpec((1, tn), mm_col),
                     pl.BlockSpec((tm, dp), lambda i, j: (i, 0)),
                     _wspec(lead_p, (dp, tn), mm_col),
                     pl.BlockSpec((tm, tn), mm_blk)]
        args += [bias.reshape(1, d), p, wp, add]
    return pl.pallas_call(
        functools.partial(_resln_kernel, nj=nj, tn=tn, alpha=alpha, gated=gated is not None),
        grid=(m // tm, 2 * nj),
        in_specs=in_specs,
        out_specs=[pl.BlockSpec((tm, tn), out_blk), pl.BlockSpec((tm, tn), out_blk)],
        out_shape=[jax.ShapeDtypeStruct((m, d), F32), jax.ShapeDtypeStruct((m, d), BF)],
        scratch_shapes=[pltpu.VMEM((nj, tm, tn), F32), pltpu.VMEM((tm, 1), F32), pltpu.VMEM((tm, 1), F32)],
        compiler_params=_cparams(("parallel", "arbitrary")),
        name=name,
    )(*args)


HALO = 8


def _ffn_up_kernel(x_ref, wg_ref, wv_ref, cwg_ref, cwv_ref, cbg_ref, cbv_ref, a_ref, hg_ref, hv_ref, *, tm, ch, tiles_per_seq):
    i = pl.program_id(1)
    first = (i % tiles_per_seq) == 0

    @pl.when(first)
    def _():
        hg_ref[0:HALO, :] = jnp.zeros((HALO, hg_ref.shape[1]), F32)
        hv_ref[0:HALO, :] = jnp.zeros((HALO, hv_ref.shape[1]), F32)

    @pl.when(jnp.logical_not(first))
    def _():
        hg_ref[0:HALO, :] = hg_ref[tm:tm + HALO, :]
        hv_ref[0:HALO, :] = hv_ref[tm:tm + HALO, :]

    wg = wg_ref[...].astype(BF)
    wv = wv_ref[...].astype(BF)

    def conv(h_ref, c, cw_ref, cb_ref):
        out = cb_ref[...]
        for k in range(CONV_W):
            off = HALO + c * ch - (CONV_W - 1) + k
            out = out + h_ref[off:off + ch, :] * cw_ref[k:k + 1, :]
        return out

    def up(c):
        xc = x_ref[c * ch:(c + 1) * ch, :]
        hg_ref[HALO + c * ch:HALO + (c + 1) * ch, :] = _dot(xc, wg)
        hv_ref[HALO + c * ch:HALO + (c + 1) * ch, :] = _dot(xc, wv)

    up(0)
    for c in range(tm // ch):
        if c + 1 < tm // ch:
            up(c + 1)
        a_ref[c * ch:(c + 1) * ch, :] = (jax.nn.gelu(conv(hg_ref, c, cwg_ref, cbg_ref))
                                         * conv(hv_ref, c, cwv_ref, cbv_ref)).astype(a_ref.dtype)


def _ffn_up(xb, w_up, conv_w, conv_b, *, seq, tm, tn, ch=256):
    m, d = xb.shape
    w_up, lead = _layered(w_up)
    conv_w, lead_c = _layered(conv_w)
    conv_b, lead_b = _layered(conv_b)
    dff = w_up.shape[-1] // 2
    nj = dff // tn
    assert dff % tn == 0 and seq % tm == 0 and tm % ch == 0
    cb = conv_b.reshape(conv_b.shape[:-1] + (1, 2 * dff))
    return pl.pallas_call(
        functools.partial(_ffn_up_kernel, tm=tm, ch=ch, tiles_per_seq=seq // tm),
        grid=(nj, m // tm),
        in_specs=[pl.BlockSpec((tm, d), lambda j, i: (i, 0)),
                  _wspec(lead, (d, tn), lambda j, i: (0, j)),
                  _wspec(lead, (d, tn), lambda j, i: (0, j + nj)),
                  _wspec(lead_c, (CONV_W, tn), lambda j, i: (0, j)),
                  _wspec(lead_c, (CONV_W, tn), lambda j, i: (0, j + nj)),
                  _wspec(lead_b, (1, tn), lambda j, i: (0, j)),
                  _wspec(lead_b, (1, tn), lambda j, i: (0, j + nj))],
        out_specs=pl.BlockSpec((tm, tn), lambda j, i: (i, j)),
        out_shape=jax.ShapeDtypeStruct((m, dff), BF),
        scratch_shapes=[pltpu.VMEM((tm + HALO, tn), F32), pltpu.VMEM((tm + HALO, tn), F32)],
        compiler_params=_cparams(("parallel", "arbitrary")),
        name="ffn_up_conv_geglu",
    )(xb, w_up, w_up, conv_w, conv_w, cb, cb)


def _t5_bucket(dist):
    dist = jnp.maximum(dist, 0)
    max_exact = N_BUCKETS // 2
    d = jnp.maximum(dist, 1).astype(F32)
    large = max_exact + (jnp.log(d / max_exact) / math.log(MAX_DISTANCE / max_exact)
                         * (N_BUCKETS - max_exact)).astype(jnp.int32)
    large = jnp.minimum(large, N_BUCKETS - 1)
    return jnp.where(dist < max_exact, dist, large)


def _near_bias(tab, offset, width):
    period = TQ + width
    idx = np.arange(period)
    k = np.where(idx < width, idx, idx - period)
    dist = np.clip(offset - k, 0, MAX_DISTANCE)
    u = tab[_t5_bucket(jnp.asarray(dist, jnp.int32))].T
    flat = jnp.tile(u, (1, TQ))[:, :TQ * (period - 1)]
    return flat.reshape(-1, TQ, period - 1)[:, :, :width]


def _pad_front(a, n):
    return jnp.pad(a, ((0, 0), (n, 0), (0, 0)))


def _in_splits():
    s_kidx = Q_LORA + KV_LORA
    s_widx = s_kidx + D_IDX
    s_qb = s_widx + H_IDX
    s_kvb = s_qb + H_B * HEAD_DIM
    s_gb = s_kvb + 6 * G_B * HEAD_DIM
    s_qc = s_gb + 3 * H_B
    s_kc = s_qc + H_C * HEAD_DIM_C
    s_vc = s_kc + KV_C * HEAD_DIM_C
    s_gm = s_vc + KV_C * HEAD_DIM_C
    segs = [(s_qb, s_kvb), (s_qc, s_kc), (0, s_kidx), (s_kidx, s_widx), (s_kvb, s_gb),
            (s_kc, s_vc), (s_vc, s_gm), (s_widx, s_qb), (s_gb, s_qc)]
    return segs, s_gm


def _small_weights_t(w_in_t, li):
    segs, s_gm = _in_splits()
    d = w_in_t.shape[-1]
    parts = [w_in_t[li, a:b] for a, b in segs] + [jnp.zeros((N_SMALL - s_gm, d), w_in_t.dtype)]
    return jnp.concatenate(parts, axis=0)


def _layer(li, x, xb, p_b, w_in_t, g_cq, g_ckv, g_kidx, b_kidx, w_qidx, w_uq, w_uk, w_uv,
           cmp_pe, cmp_w1, cmp_w2, sinks, w_br, w_o, ln1_g, ln1_b,
           w_up, conv_w, conv_b, w_down_b, w_pg, b_pg, w_pp, ln2_g, ln2_b, consts, *, bsz, seq, alpha):
    m, d = x.shape
    z, z_b = _mm(xb, _small_weights_t(w_in_t, li), tm=1024, tn=512, also=BF, trans_b=True, name="in_proj")
    cq_n, ckv_n, kidx_n = _prep(z, g_cq[li], g_ckv[li], g_kidx[li], b_kidx[li])
    w_qq = jnp.concatenate([w_qidx[li], w_uq[li]], axis=1).astype(BF)
    qq = _mm(cq_n, w_qq, tm=1024, tn=1280, out_dtype=BF, name="q_proj")

    z3 = z.reshape(bsz, seq, N_SMALL)
    b3 = lambda a: a.reshape(bsz, seq, a.shape[-1])
    zb3 = z_b.reshape(bsz, seq, N_SMALL)
    seg = lambda off, w: zb3[:, :, off:off + w]

    o_a = _dsa(b3(qq), z3, b3(kidx_n), b3(ckv_n), w_uk[li], w_uv[li], consts["near_a"], consts["far_a"], seq=seq)

    gw = G_B * HEAD_DIM
    kcv = _compress(z3, (cmp_pe, li), (cmp_w1, li), (cmp_w2, li), seq=seq)
    o_b = _nsa(z3, kcv, seg(OFF_KVB + 2 * gw, gw), seg(OFF_KVB + 3 * gw, gw), seg(OFF_KVB + 4 * gw, gw),
               seg(OFF_KVB + 5 * gw, gw), consts["et_pad"], consts["overlap_t"],
               consts["near_s"], consts["near_w"], consts["far_s"], seq=seq)

    kvw = KV_C * HEAD_DIM_C
    o_c = _swa(z3, seg(OFF_KC, kvw), seg(OFF_VC, kvw), consts["near_c"], sinks[li], seq=seq)

    o_all = jnp.stack([o_a.reshape(m, BRANCH_W), o_b.reshape(m, BRANCH_W), o_c.reshape(m, BRANCH_W)])
    merged = _merge(xb, w_in_t, li, _in_splits()[1], o_all, (w_br, li), tm=1024, tn=512)
    x1, x1b = _resln(merged, (w_o, li), x, ln1_g[li], ln1_b[li], alpha=alpha, tm=1024, tn=256, name="attn_out_ln1")

    a = _ffn_up(x1b, (w_up, li), (conv_w, li), (conv_b, li), seq=seq, tm=min(1024, seq), tn=256)
    ffn = _mm(a, (w_down_b, li), tm=512, tn=512, name="ffn_down")
    return _resln(x1b, (w_pg, li), x1, ln2_g[li], ln2_b[li], alpha=alpha, tm=1024, tn=256,
                  gated=(b_pg[li], p_b, (w_pp, li), ffn), name="ple_ln2")


def kernel(x, p, w_in, g_cq, g_ckv, g_kidx, b_kidx, w_qidx, w_uq, w_uk, w_uv, cmp_pe, cmp_w1, cmp_w2, sinks, w_br, w_o, ln1_g, ln1_b, w_up, conv_w, conv_b, w_down, w_pg, b_pg, w_pp, ln2_g, ln2_b, rel_bias):
    bsz, seq, d = x.shape
    depth = p.shape[0]
    alpha = (2 * depth) ** 0.25
    m = bsz * seq
    assert seq % TK == 0 and d % 512 == 0 and seq + TQ <= 2 ** TIE_POS_BITS

    key = np.arange(seq)[:, None]
    blk = np.arange(LANES)[None, :]
    et = (key // SEL_BLOCK == blk).astype(np.float32)
    et_pad = jnp.asarray(np.concatenate([np.zeros((TQ, LANES), np.float32), et]), BF)
    ncp = seq // CMP_STRIDE
    cs = np.arange(ncp)[:, None] * CMP_STRIDE
    bs = blk * SEL_BLOCK
    n_cmp = (seq - CMP_BLOCK) // CMP_STRIDE + 1
    ov = ((cs < bs + SEL_BLOCK) & (cs + CMP_BLOCK > bs) & (blk < seq // SEL_BLOCK)
          & (np.arange(ncp)[:, None] < n_cmp)).astype(np.float32)
    tab_a, tab_b, tab_c = rel_bias[:, :H_A], rel_bias[:, H_A:H_A + H_B], rel_bias[:, H_A + H_B:]
    consts = {"et_pad": et_pad, "overlap_t": jnp.asarray(ov.T, BF),
              "near_a": _near_bias(tab_a, TQ, 2 * TQ), "far_a": tab_a[N_BUCKETS - 1],
              "near_s": _near_bias(tab_b, TQ, 2 * TQ), "near_w": _near_bias(tab_b, WIN_B, WIN_B + TQ),
              "far_s": tab_b[N_BUCKETS - 1], "near_c": _near_bias(tab_c, WIN_C, 2 * TQ)}

    xf = x.reshape(m, d)
    xb = xf.astype(BF)
    w_down_b = w_down.astype(BF)
    w_in_t = jnp.swapaxes(w_in, 1, 2)
    for li in range(depth):
        xf, xb = _layer(li, xf, xb, p[li].reshape(m, -1).astype(BF), w_in_t, g_cq, g_ckv, g_kidx, b_kidx,
                        w_qidx, w_uq, w_uk, w_uv, cmp_pe, cmp_w1, cmp_w2, sinks,
                        w_br, w_o, ln1_g, ln1_b, w_up, conv_w, conv_b, w_down_b,
                        w_pg, b_pg, w_pp, ln2_g, ln2_b, consts, bsz=bsz, seq=seq, alpha=alpha)
    return xf.reshape(bsz, seq, d)
```
